```python
import math
import jax, jax.numpy as jnp
from jax import lax
import numpy as np

D_MODEL = 1024
BATCH = 4
SEQ = 8192
DEPTH = 2
DEC_BATCH = 1
DEC_SEQ = 16384
PAST_LEN = 128

GRID_W = 64
Q_BLOCK = 128
NORM_EPS = 1e-6
D_FF = 2816
D_HY = 512
HY_IN = 3 * D_HY
SHORT_CONV = 3
POS_BANDS = 16
POS_EMB = 1 + 2 * POS_BANDS
FILT_HID = 64
DECAY_TARGET = 1e-2
FAST_DECAY_PCT = 0.3
SLOW_DECAY_PCT = 1.5
HEAD_DIM = 64
N_Q_HEADS = 8
N_KV_HEADS = 2
GQA_GROUP = N_Q_HEADS // N_KV_HEADS
D_GQA = N_Q_HEADS * HEAD_DIM
D_GQA_KV = N_KV_HEADS * HEAD_DIM
ROPE_THETA = 10000.0
N_DIFF_HEADS = 4
D_DIFF = N_DIFF_HEADS * 2 * HEAD_DIM
DIFF_SUBLN_EPS = 1e-5
N_BUCKETS = 32
MAX_DISTANCE = 128
N_BRANCHES = 3
D_BRANCH = 512
OFF_HY = 0
OFF_GQ = OFF_HY + HY_IN
OFF_GK = OFF_GQ + D_GQA
OFF_GV = OFF_GK + D_GQA_KV
OFF_DQ = OFF_GV + D_GQA_KV
OFF_DK = OFF_DQ + D_DIFF
OFF_DV = OFF_DK + D_DIFF
OFF_GATE = OFF_DV + D_DIFF
IN_COLS = OFF_GATE + N_BRANCHES * D_MODEL

kernel_name = "hybrid_hyena_gqa_diffattn_encoder"

F32 = jnp.float32


def rms_norm(x, g, eps=NORM_EPS):
    xf = x.astype(F32)
    y = xf * lax.rsqrt(jnp.mean(xf * xf, axis=-1, keepdims=True) + eps)
    return (y * g.astype(F32)).astype(x.dtype)


def swiglu_ffn(x, norm_g, w_in, w_out):
    h = rms_norm(x, norm_g)
    gate, up = jnp.split(h @ w_in, 2, axis=-1)
    return (jax.nn.silu(gate) * up) @ w_out


def short_conv(u, w, b):
    up = jnp.pad(u, ((0, 0), (1, 1), (0, 0)))
    return up[:, :-2] * w[0] + up[:, 1:-1] * w[1] + up[:, 2:] * w[2] + b


def hyena_kernel(L, w1, b1, w2, b2, w3, freq):
    t = jnp.linspace(0.0, 1.0, L, dtype=F32)[:, None]
    band = jnp.linspace(1e-4, POS_BANDS - 1, POS_BANDS, dtype=F32)
    ang = (2.0 * math.pi / L) * jnp.arange(L, dtype=F32)[:, None] * band[None, :]
    feats = jnp.concatenate([t, jnp.cos(ang), -jnp.sin(ang)], axis=-1)
    fr = freq.astype(F32)
    h = jnp.sin(fr * (feats @ w1.astype(F32) + b1.astype(F32)))
    h = jnp.sin(fr * (h @ w2.astype(F32) + b2.astype(F32)))
    h = h @ w3.astype(F32)
    max_decay = math.log(DECAY_TARGET) / FAST_DECAY_PCT
    min_decay = math.log(DECAY_TARGET) / SLOW_DECAY_PCT
    deltas = jnp.linspace(min_decay, max_decay, D_HY, dtype=F32)
    decay = jnp.exp(-t * jnp.abs(deltas)[None, :])
    h_fwd = h[:, :D_HY] * decay
    h_bwd = h[:, D_HY:] * decay
    return jnp.concatenate([h_fwd, jnp.zeros((1, D_HY), F32), h_bwd[:0:-1]], axis=0)


def hyena_mixer(u, conv_w, conv_b, kernel, skip):
    L = u.shape[1]
    u = short_conv(u, conv_w, conv_b)
    x0, x1, v = jnp.split(u, 3, axis=-1)
    z = (v * x1).astype(F32)
    n = 2 * L
    zf = jnp.fft.rfft(z, n=n, axis=1)
    kf = jnp.fft.rfft(kernel, n=n, axis=0)
    y = jnp.fft.irfft(zf * kf[None], n=n, axis=1)[:, :L]
    y = y + z * skip.astype(F32)
    return x0 * y.astype(u.dtype)


def axial_rope_tables(L):
    rows = L // GRID_W
    row = jnp.repeat(jnp.arange(rows, dtype=F32), GRID_W)
    col = jnp.tile(jnp.arange(GRID_W, dtype=F32), rows)
    half = HEAD_DIM // 2
    inv = ROPE_THETA ** (-jnp.arange(0, half, 2, dtype=F32) / half)
    ang = jnp.concatenate([row[:, None] * inv, col[:, None] * inv], axis=-1)
    return jnp.cos(ang), jnp.sin(ang)


def apply_rope(x, cos, sin):
    shp = x.shape
    xr = x.astype(F32).reshape(shp[:-1] + (shp[-1] // 2, 2))
    c = cos[None, :, None, :]
    s = sin[None, :, None, :]
    a, b = xr[..., 0], xr[..., 1]
    out = jnp.stack([a * c - b * s, a * s + b * c], axis=-1).reshape(shp)
    return out.astype(x.dtype)


def t5_bucket(rel):
    nb = N_BUCKETS // 2
    max_exact = nb // 2
    ret = jnp.where(rel > 0, nb, 0)
    n = jnp.abs(rel)
    nf = jnp.maximum(n, 1).astype(F32)
    large = max_exact + (jnp.log(nf / max_exact) / math.log(MAX_DISTANCE / max_exact)
                         * (nb - max_exact)).astype(jnp.int32)
    large = jnp.minimum(large, nb - 1)
    return ret + jnp.where(n < max_exact, n, large)


def sweep_query_blocks(block_fn, qs):
    B, L = qs[0].shape[:2]
    nb = L // Q_BLOCK
    blocks = tuple(jnp.moveaxis(q.reshape((B, nb, Q_BLOCK) + q.shape[2:]), 1, 0) for q in qs)
    starts = jnp.arange(nb, dtype=jnp.int32) * Q_BLOCK
    out = lax.map(lambda a: block_fn(a[0], *a[1]), (starts, blocks))
    return jnp.moveaxis(out, 0, 1).reshape((B, L) + out.shape[3:])


def gqa_mixer(q, k, v, q_norm, k_norm, cos, sin):
    B, L = q.shape[:2]
    q = q.reshape(B, L, N_Q_HEADS, HEAD_DIM)
    k = k.reshape(B, L, N_KV_HEADS, HEAD_DIM)
    v = v.reshape(B, L, N_KV_HEADS, HEAD_DIM)
    q = apply_rope(rms_norm(q, q_norm), cos, sin) * (HEAD_DIM ** -0.5)
    k = apply_rope(rms_norm(k, k_norm), cos, sin)
    q = q.reshape(B, L, N_KV_HEADS, GQA_GROUP, HEAD_DIM)

    def block(start, qb):
        s = jnp.einsum('bqhgd,bkhd->bhgqk', qb, k).astype(F32)
        p = jax.nn.softmax(s, axis=-1).astype(v.dtype)
        return jnp.einsum('bhgqk,bkhd->bqhgd', p, v)

    o = sweep_query_blocks(block, (q,))
    return o.reshape(B, L, D_GQA)


def diff_mixer(q, k, v, lam_params, subln_g, rel_bias, lam_init):
    B, L = q.shape[:2]
    q = q.reshape(B, L, N_DIFF_HEADS, 2, HEAD_DIM) * (HEAD_DIM ** -0.5)
    k = k.reshape(B, L, N_DIFF_HEADS, 2, HEAD_DIM)
    v = v.reshape(B, L, N_DIFF_HEADS, 2 * HEAD_DIM)
    lp = lam_params.astype(F32)
    lam = jnp.exp(jnp.sum(lp[0] * lp[1])) - jnp.exp(jnp.sum(lp[2] * lp[3])) + lam_init
    kpos = jnp.arange(L, dtype=jnp.int32)
    table = rel_bias.astype(F32)

    def block(start, qb):
        qpos = start + jnp.arange(Q_BLOCK, dtype=jnp.int32)
        bucket = t5_bucket(kpos[None, :] - qpos[:, None])
        bias = jnp.moveaxis(table[bucket], -1, 0)
        s = jnp.einsum('bqhcd,bkhcd->bchqk', qb, k).astype(F32) + bias
        p = jax.nn.softmax(s, axis=-1)
        a = (p[:, 0] - lam * p[:, 1]).astype(v.dtype)
        return jnp.einsum('bhqk,bkhe->bqhe', a, v)

    o = sweep_query_blocks(block, (q,))
    o = rms_norm(o, subln_g, DIFF_SUBLN_EPS) * (1.0 - lam_init)
    return o.reshape(B, L, D_DIFF)


def encoder_trunk(x, P):
    L = x.shape[1]
    cos, sin = axial_rope_tables(L)
    for l in range(DEPTH):
        x = x + 0.5 * swiglu_ffn(x, P["ffn1_norm"][l], P["ffn1_w_in"][l], P["ffn1_w_out"][l])
        h = rms_norm(x, P["mix_norm"][l])
        p = h @ P["w_in"][l]
        kernel = hyena_kernel(L, P["hy_filt_w1"][l], P["hy_filt_b1"][l], P["hy_filt_w2"][l],
                              P["hy_filt_b2"][l], P["hy_filt_w3"][l], P["hy_filt_freq"][l])
        y_hy = hyena_mixer(p[..., OFF_HY:OFF_GQ], P["hy_conv_w"][l], P["hy_conv_b"][l],
                           kernel, P["hy_skip"][l])
        y_gqa = gqa_mixer(p[..., OFF_GQ:OFF_GK], p[..., OFF_GK:OFF_GV], p[..., OFF_GV:OFF_DQ],
                          P["gqa_q_norm"][l], P["gqa_k_norm"][l], cos, sin)
        lam_init = 0.8 - 0.6 * math.exp(-0.3 * l)
        y_diff = diff_mixer(p[..., OFF_DQ:OFF_DK], p[..., OFF_DK:OFF_DV], p[..., OFF_DV:OFF_GATE],
                            P["diff_lambda"][l], P["diff_subln"][l], P["rel_bias"], lam_init)
        gates = jax.nn.sigmoid(p[..., OFF_GATE:].astype(F32)).astype(x.dtype)
        wb = P["w_branch"][l]
        merged = (gates[..., 0:D_MODEL] * (y_hy @ wb[0])
                  + gates[..., D_MODEL:2 * D_MODEL] * (y_gqa @ wb[1])
                  + gates[..., 2 * D_MODEL:3 * D_MODEL] * (y_diff @ wb[2]))
        x = x + merged @ P["w_out"][l]
        x = x + 0.5 * swiglu_ffn(x, P["ffn2_norm"][l], P["ffn2_w_in"][l], P["ffn2_w_out"][l])
    return rms_norm(x, P["final_norm"])


def setup_inputs(seed: int = 0) -> dict:
    key = jax.random.key(seed)
    ks = jax.random.split(key, 32)

    def nrm(k, shape, scale):
        return jax.random.normal(k, shape, F32) * scale

    def gain(k, shape):
        return 1.0 + 0.02 * jax.random.normal(k, shape, F32)

    return {
        "x_prompt": nrm(ks[0], (BATCH, SEQ, D_MODEL), 1.0),
        "x_sample": nrm(ks[1], (DEC_BATCH, DEC_SEQ, D_MODEL), 1.0),
        "ffn1_norm": gain(ks[2], (DEPTH, D_MODEL)),
        "ffn1_w_in": nrm(ks[3], (DEPTH, D_MODEL, 2 * D_FF), D_MODEL ** -0.5),
        "ffn1_w_out": nrm(ks[4], (DEPTH, D_FF, D_MODEL), D_FF ** -0.5),
        "mix_norm": gain(ks[5], (DEPTH, D_MODEL)),
        "w_in": nrm(ks[6], (DEPTH, D_MODEL, IN_COLS), D_MODEL ** -0.5),
        "hy_conv_w": nrm(ks[7], (DEPTH, SHORT_CONV, HY_IN), SHORT_CONV ** -0.5),
        "hy_conv_b": nrm(ks[8], (DEPTH, HY_IN), 0.02),
        "hy_filt_w1": nrm(ks[9], (DEPTH, POS_EMB, FILT_HID), POS_EMB ** -0.5),
        "hy_filt_b1": nrm(ks[10], (DEPTH, FILT_HID), 0.02),
        "hy_filt_w2": nrm(ks[11], (DEPTH, FILT_HID, FILT_HID), FILT_HID ** -0.5),
        "hy_filt_b2": nrm(ks[12], (DEPTH, FILT_HID), 0.02),
        "hy_filt_w3": nrm(ks[13], (DEPTH, FILT_HID, 2 * D_HY), 0.05 * FILT_HID ** -0.5),
        "hy_filt_freq": gain(ks[14], (DEPTH, FILT_HID)),
        "hy_skip": nrm(ks[15], (DEPTH, D_HY), 1.0),
        "gqa_q_norm": gain(ks[16], (DEPTH, HEAD_DIM)),
        "gqa_k_norm": gain(ks[17], (DEPTH, HEAD_DIM)),
        "diff_lambda": nrm(ks[18], (DEPTH, 4, HEAD_DIM), 0.1),
        "diff_subln": gain(ks[19], (DEPTH, 2 * HEAD_DIM)),
        "rel_bias": nrm(ks[20], (N_BUCKETS, N_DIFF_HEADS), 0.5),
        "w_branch": nrm(ks[21], (DEPTH, N_BRANCHES, D_BRANCH, D_MODEL), D_BRANCH ** -0.5),
        "w_out": nrm(ks[22], (DEPTH, D_MODEL, D_MODEL), D_MODEL ** -0.5),
        "ffn2_norm": gain(ks[23], (DEPTH, D_MODEL)),
        "ffn2_w_in": nrm(ks[24], (DEPTH, D_MODEL, 2 * D_FF), D_MODEL ** -0.5),
        "ffn2_w_out": nrm(ks[25], (DEPTH, D_FF, D_MODEL), D_FF ** -0.5),
        "final_norm": gain(ks[26], (D_MODEL,)),
    }


def reference(x_prompt, x_sample, ffn1_norm, ffn1_w_in, ffn1_w_out, mix_norm, w_in,
              hy_conv_w, hy_conv_b, hy_filt_w1, hy_filt_b1, hy_filt_w2, hy_filt_b2,
              hy_filt_w3, hy_filt_freq, hy_skip, gqa_q_norm, gqa_k_norm, diff_lambda,
              diff_subln, rel_bias, w_branch, w_out, ffn2_norm, ffn2_w_in, ffn2_w_out,
              final_norm):
    params = {
        "ffn1_norm": ffn1_norm, "ffn1_w_in": ffn1_w_in, "ffn1_w_out": ffn1_w_out,
        "mix_norm": mix_norm, "w_in": w_in,
        "hy_conv_w": hy_conv_w, "hy_conv_b": hy_conv_b,
        "hy_filt_w1": hy_filt_w1, "hy_filt_b1": hy_filt_b1, "hy_filt_w2": hy_filt_w2,
        "hy_filt_b2": hy_filt_b2, "hy_filt_w3": hy_filt_w3, "hy_filt_freq": hy_filt_freq,
        "hy_skip": hy_skip, "gqa_q_norm": gqa_q_norm, "gqa_k_norm": gqa_k_norm,
        "diff_lambda": diff_lambda, "diff_subln": diff_subln, "rel_bias": rel_bias,
        "w_branch": w_branch, "w_out": w_out,
        "ffn2_norm": ffn2_norm, "ffn2_w_in": ffn2_w_in, "ffn2_w_out": ffn2_w_out,
        "final_norm": final_norm,
    }
    y_prompt = encoder_trunk(x_prompt, params)
    y_sample = encoder_trunk(x_sample, params)
    return (y_prompt, y_sample)
```

```python
import functools
import math

import numpy as np
import jax
import jax.numpy as jnp
from jax import lax
from jax.experimental import pallas as pl
from jax.experimental.pallas import tpu as pltpu

F32 = jnp.float32
BF16 = jnp.bfloat16

NORM_EPS = 1e-6
GRID_W = 64
HEAD_DIM = 64
N_Q_HEADS = 8
N_KV_HEADS = 2
GQA_GROUP = N_Q_HEADS // N_KV_HEADS
ROPE_THETA = 10000.0
N_DIFF_HEADS = 4
DIFF_SUBLN_EPS = 1e-5
N_BUCKETS = 32
MAX_DISTANCE = 128
POS_BANDS = 16
DECAY_TARGET = 1e-2
FAST_DECAY_PCT = 0.3
SLOW_DECAY_PCT = 1.5

LANES = 128
VMEM_LIMIT_BYTES = 56 * 1024 * 1024

FFN_TM = 512
PROJ_TM = 256
MERGE_TM = 512
CONV_TM = 512
GQA_TQ = 128
GQA_TK = 512
DIFF_TQ = 512
DIFF_TK = 512
FFT_N2 = 128
FFT_TC = 2048
FILT_TM = 512

HIGHEST = lax.Precision.HIGHEST
NEG_BIG = -1e30


def _params(n_axes):
    return pltpu.CompilerParams(dimension_semantics=("arbitrary",) * n_axes,
                                vmem_limit_bytes=VMEM_LIMIT_BYTES)


def _dot(a, b):
    return jnp.dot(a, b, preferred_element_type=F32)


def _dot_hp(a, b):
    return jnp.dot(a, b, precision=HIGHEST, preferred_element_type=F32)


def _rms(x, g, eps):
    ms = jnp.mean(x * x, axis=-1, keepdims=True)
    return x * lax.rsqrt(ms + eps) * g


def _ffn_body(*refs, n_ff, final):
    if final:
        x_ref, g_ref, wg_ref, wu_ref, wo_ref, gf_ref, o_ref, h_scr, acc_scr = refs
    else:
        x_ref, g_ref, wg_ref, wu_ref, wo_ref, o_ref, h_scr, acc_scr = refs
    j = pl.program_id(1)

    @pl.when(j == 0)
    def _():
        h_scr[...] = _rms(x_ref[...], g_ref[...], NORM_EPS).astype(BF16)
        acc_scr[...] = jnp.zeros_like(acc_scr)

    h = h_scr[...]
    gate = _dot(h, wg_ref[...])
    up = _dot(h, wu_ref[...])
    a = (gate * jax.nn.sigmoid(gate) * up).astype(BF16)
    acc_scr[...] += _dot(a, wo_ref[...])

    @pl.when(j == n_ff - 1)
    def _():
        y = x_ref[...] + 0.5 * acc_scr[...]
        if final:
            y = _rms(y, gf_ref[...], NORM_EPS)
        o_ref[...] = y


def _ffn_tile(d_ff):
    for n in (2, 4, 1):
        if d_ff % n == 0 and (d_ff // n) % LANES == 0:
            return d_ff // n
    return d_ff


def _ffn(x, norm_g, w_in, w_out, final_g=None):
    t, d = x.shape
    d_ff = w_out.shape[0]
    tf = _ffn_tile(d_ff)
    n_ff = d_ff // tf
    tm = min(FFN_TM, t)
    final = final_g is not None
    in_specs = [
        pl.BlockSpec((tm, d), lambda i, j: (i, 0)),
        pl.BlockSpec((1, d), lambda i, j: (0, 0)),
        pl.BlockSpec((d, tf), lambda i, j: (0, j)),
        pl.BlockSpec((d, tf), lambda i, j: (0, j + n_ff)),
        pl.BlockSpec((tf, d), lambda i, j: (j, 0)),
    ]
    args = [x, norm_g.reshape(1, d), w_in, w_in, w_out]
    if final:
        in_specs.append(pl.BlockSpec((1, d), lambda i, j: (0, 0)))
        args.append(final_g.reshape(1, d))
    return pl.pallas_call(
        functools.partial(_ffn_body, n_ff=n_ff, final=final),
        grid=(t // tm, n_ff),
        in_specs=in_specs,
        out_specs=pl.BlockSpec((tm, d), lambda i, j: (i, 0)),
        out_shape=jax.ShapeDtypeStruct((t, d), F32),
        scratch_shapes=[pltpu.VMEM((tm, d), BF16), pltpu.VMEM((tm, d), F32)],
        compiler_params=_params(2),
        name="ffn",
    )(*args)


def _head_rms(x, bd):
    x2 = x * x
    hi = x2.astype(BF16)
    lo = (x2 - hi.astype(F32)).astype(BF16)
    ms = _dot(hi, bd) + _dot(lo, bd)
    return x * lax.rsqrt(ms + NORM_EPS)


def _rope(x, cos, sin_signed):
    lane = lax.broadcasted_iota(jnp.int32, x.shape, 1)
    partner = jnp.where((lane % HEAD_DIM) < HEAD_DIM // 2,
                        pltpu.roll(x, LANES - HEAD_DIM // 2, 1),
                        pltpu.roll(x, HEAD_DIM // 2, 1))
    return x * cos + partner * sin_signed


def _inproj_body(x_ref, g_ref, w_ref, cos_ref, sin_ref, bd_ref, qn_ref, kn_ref,
                 hy_ref, gq_ref, gk_ref, gv_ref, dq_ref, dk_ref, dv_ref, gate_ref, *, offs):
    o_hy, o_gq, o_gk, o_gv, o_dq, o_dk, o_dv, o_gate, o_end = offs
    h = _rms(x_ref[...], g_ref[...], NORM_EPS).astype(BF16)

    def seg(a, b):
        return _dot(h, w_ref[:, a:b])

    hy_ref[...] = seg(o_hy, o_gq)
    cos = cos_ref[...]
    sin = sin_ref[...]
    bd = bd_ref[...]
    scale = HEAD_DIM ** -0.5
    for c in range((o_gk - o_gq) // LANES):
        a = o_gq + c * LANES
        y = _head_rms(seg(a, a + LANES), bd) * qn_ref[:, c * LANES:(c + 1) * LANES]
        gq_ref[:, c * LANES:(c + 1) * LANES] = (_rope(y, cos, sin) * scale).astype(BF16)
    for c in range((o_gv - o_gk) // LANES):
        a = o_gk + c * LANES
        y = _head_rms(seg(a, a + LANES), bd) * kn_ref[:, c * LANES:(c + 1) * LANES]
        gk_ref[:, c * LANES:(c + 1) * LANES] = _rope(y, cos, sin).astype(BF16)
    gv_ref[...] = seg(o_gv, o_dq).astype(BF16)
    dq_ref[...] = (seg(o_dq, o_dk) * scale).astype(BF16)
    dk_ref[...] = seg(o_dk, o_dv).astype(BF16)
    dv_ref[...] = seg(o_dv, o_gate).astype(BF16)
    gate_ref[...] = jax.nn.sigmoid(seg(o_gate, o_end)).astype(BF16)


def _inproj(x, norm_g, w, cos_t, sin_t, bd, qn, kn, offs, groups):
    t, d = x.shape
    tm = min(PROJ_TM, t)
    widths = [offs[i + 1] - offs[i] for i in range(8)]
    (bl, l1), (_, l2) = groups
    nbp = bl // tm

    def pos_map(i):
        return (jnp.where(i < nbp, i % (l1 // tm), (i - nbp) % (l2 // tm)), 0)

    full = lambda i: (0, 0)
    row = lambda i: (i, 0)
    out_dtypes = [F32] + [BF16] * 7
    return pl.pallas_call(
        functools.partial(_inproj_body, offs=tuple(offs)),
        grid=(t // tm,),
        in_specs=[
            pl.BlockSpec((tm, d), row),
            pl.BlockSpec((1, d), full),
            pl.BlockSpec(w.shape, full),
            pl.BlockSpec((tm, LANES), pos_map),
            pl.BlockSpec((tm, LANES), pos_map),
            pl.BlockSpec(bd.shape, full),
            pl.BlockSpec(qn.shape, full),
            pl.BlockSpec(kn.shape, full),
        ],
        out_specs=[pl.BlockSpec((tm, wd), row) for wd in widths],
        out_shape=[jax.ShapeDtypeStruct((t, wd), dt) for wd, dt in zip(widths, out_dtypes)],
        compiler_params=_params(1),
        name="inproj",
    )(x, norm_g.reshape(1, d), w, cos_t, sin_t, bd, qn, kn)


def _shortconv_body(u_ref, prev_ref, next_ref, w_ref, b_ref, z_ref, x0_ref, *, tm, groups, d_hy):
    i = pl.program_id(0)
    row0 = i * tm
    (bl, l1), (_, l2) = groups
    pos = jnp.where(row0 < bl, row0 % l1, (row0 - bl) % l2)
    seq_len = jnp.where(row0 < bl, l1, l2)
    u = u_ref[...]
    rows = lax.broadcasted_iota(jnp.int32, u.shape, 0)
    before = jnp.where(pos == 0, 0.0, prev_ref[7:8, :])
    after = jnp.where(pos + tm == seq_len, 0.0, next_ref[0:1, :])
    up = jnp.where(rows == 0, before, pltpu.roll(u, 1, 0))
    dn = jnp.where(rows == tm - 1, after, pltpu.roll(u, tm - 1, 0))
    y = up * w_ref[0:1, :] + u * w_ref[1:2, :] + dn * w_ref[2:3, :] + b_ref[...]
    x0_ref[...] = y[:, :d_hy]
    z_ref[...] = y[:, 2 * d_hy:] * y[:, d_hy:2 * d_hy]


def _shortconv(u, w, b, groups):
    t, c3 = u.shape
    d_hy = c3 // 3
    tm = min(CONV_TM, t)
    nb8 = t // 8
    return pl.pallas_call(
        functools.partial(_shortconv_body, tm=tm, groups=groups, d_hy=d_hy),
        grid=(t // tm,),
        in_specs=[
            pl.BlockSpec((tm, c3), lambda i: (i, 0)),
            pl.BlockSpec((8, c3), lambda i: (jnp.maximum(i * (tm // 8) - 1, 0), 0)),
            pl.BlockSpec((8, c3), lambda i: (jnp.minimum((i + 1) * (tm // 8), nb8 - 1), 0)),
            pl.BlockSpec((3, c3), lambda i: (0, 0)),
            pl.BlockSpec((1, c3), lambda i: (0, 0)),
        ],
        out_specs=[pl.BlockSpec((tm, d_hy), lambda i: (i, 0))] * 2,
        out_shape=[jax.ShapeDtypeStruct((t, d_hy), F32)] * 2,
        compiler_params=_params(1),
        name="shortconv",
    )(u, u, u, w, b.reshape(1, c3))


def _filter_body(f_ref, w1_ref, b1_ref, w2_ref, b2_ref, w3_ref, fr_ref, dl_ref, o_ref, *, n_feat):
    f = f_ref[...]
    fr = fr_ref[...]
    h = jnp.sin(fr * (_dot_hp(f, w1_ref[...]) + b1_ref[...]))
    h = jnp.sin(fr * (_dot_hp(h, w2_ref[...]) + b2_ref[...]))
    h = _dot_hp(h, w3_ref[...])
    tpos = f[:, 0:1]
    valid = f[:, n_feat:n_feat + 1]
    o_ref[...] = h * jnp.exp(-tpos * dl_ref[...]) * valid


def _hyena_filter(feats, w1p, b1, w2, b2, w3, freq, deltas_abs, seq_len, n_feat):
    n2l, fw = feats.shape
    d_hy = w3.shape[1] // 2
    hid = w2.shape[0]
    tm = min(FILT_TM, seq_len)
    nbl = seq_len // tm
    full = lambda i: (0, 0)
    return pl.pallas_call(
        functools.partial(_filter_body, n_feat=n_feat),
        grid=(n2l // tm,),
        in_specs=[
            pl.BlockSpec((tm, fw), lambda i: (i, 0)),
            pl.BlockSpec(w1p.shape, full),
            pl.BlockSpec((1, hid), full),
            pl.BlockSpec(w2.shape, full),
            pl.BlockSpec((1, hid), full),
            pl.BlockSpec((hid, d_hy), lambda i: (0, i // nbl)),
            pl.BlockSpec((1, hid), full),
            pl.BlockSpec((1, d_hy), full),
        ],
        out_specs=pl.BlockSpec((tm, d_hy), lambda i: (i, 0)),
        out_shape=jax.ShapeDtypeStruct((n2l, d_hy), F32),
        compiler_params=_params(1),
        name="hyena_filter",
    )(feats, w1p, b1.reshape(1, hid), w2, b2.reshape(1, hid), w3, freq.reshape(1, hid),
      deltas_abs.reshape(1, d_hy))


def _fft_outer_body(f_ref, x_ref, o_ref):
    o_ref[0] = _dot_hp(f_ref[...], x_ref[0])


def _fft_first(fmat, xview):
    nseq, k, cols = xview.shape
    m = fmat.shape[0]
    tc = min(FFT_TC, cols)
    return pl.pallas_call(
        _fft_outer_body,
        grid=(nseq, cols // tc),
        in_specs=[pl.BlockSpec((m, k), lambda b, j: (0, 0)),
                  pl.BlockSpec((1, k, tc), lambda b, j: (b, 0, j))],
        out_specs=pl.BlockSpec((1, m, tc), lambda b, j: (b, 0, j)),
        out_shape=jax.ShapeDtypeStruct((nseq, m, cols), F32),
        compiler_params=_params(2),
        name="fft_first",
    )(fmat, xview)


def _stack_complex(re, im):
    return jnp.concatenate([jnp.concatenate([re, -im], axis=1),
                            jnp.concatenate([im, re], axis=1)], axis=0)


def _fft_mid_body(*refs, conv, scale):
    if conv:
        a_ref, kf_ref, twr_ref, twi_ref, f2r_ref, f2i_ref, o_ref = refs
    else:
        a_ref, twr_ref, twi_ref, f2r_ref, f2i_ref, o_ref = refs
    n2 = f2r_ref.shape[0]
    c = a_ref.shape[-1]
    f2r = f2r_ref[...]
    f2i = f2i_ref[...]
    twr = twr_ref[0]
    twi = twi_ref[0]
    gr = f2r * twr - f2i * twi
    gi = f2r * twi + f2i * twr
    a = a_ref[0].reshape(2 * n2, c)
    x = _dot_hp(_stack_complex(gr, gi), a)
    if not conv:
        o_ref[0] = (x * scale).reshape(2, n2, c)
        return
    xr, xi = x[:n2], x[n2:]
    kr, ki = kf_ref[0], kf_ref[1]
    y = jnp.concatenate([xr * kr - xi * ki, xr * ki + xi * kr], axis=0)
    twr_c = jnp.transpose(jnp.broadcast_to(twr, (n2, n2)))
    twi_c = jnp.transpose(jnp.broadcast_to(twi, (n2, n2)))
    gtr = f2r * twr_c - f2i * twi_c
    gti = f2r * twi_c + f2i * twr_c
    o_ref[0] = _dot_hp(_stack_complex(gtr, -gti), y).reshape(2, n2, c)


def _fft_mid(a, tw_r, tw_i, f2r, f2i, kf=None, scale=1.0):
    nseq, _, n, c = a.shape
    n2 = f2r.shape[0]
    n1 = n // n2
    conv = kf is not None
    blk = pl.BlockSpec((1, 2, n2, c), lambda k, b: (b, 0, k, 0))
    tw_spec = pl.BlockSpec((1, 1, n2), lambda k, b: (k, 0, 0))
    f_spec = pl.BlockSpec((n2, n2), lambda k, b: (0, 0))
    in_specs = [blk]
    args = [a]
    if conv:
        in_specs.append(pl.BlockSpec((2, n2, c), lambda k, b: (0, k, 0)))
        args.append(kf)
    in_specs += [tw_spec, tw_spec, f_spec, f_spec]
    args += [tw_r.reshape(n1, 1, n2), tw_i.reshape(n1, 1, n2), f2r, f2i]
    return pl.pallas_call(
        functools.partial(_fft_mid_body, conv=conv, scale=scale),
        grid=(n1, nseq),
        in_specs=in_specs,
        out_specs=blk,
        out_shape=jax.ShapeDtypeStruct(a.shape, F32),
        compiler_params=_params(2),
        name="fft_mid",
    )(*args)


def _fft_last_body(f_ref, c_ref, x0_ref, z_ref, skip_ref, o_ref):
    m2, tc = c_ref.shape[1] * c_ref.shape[2], c_ref.shape[3]
    y = _dot_hp(f_ref[...], c_ref[0].reshape(m2, tc))
    o_ref[0] = (x0_ref[0] * (y + z_ref[0] * skip_ref[...])).astype(o_ref.dtype)


def _fft_last(finv, cview, x0view, zview, skip_row):
    nseq, _, n1, cols = cview.shape
    n1h = finv.shape[0]
    tc = min(FFT_TC, cols)
    rowblk = pl.BlockSpec((1, n1h, tc), lambda b, j: (b, 0, j))
    return pl.pallas_call(
        _fft_last_body,
        grid=(nseq, cols // tc),
        in_specs=[pl.BlockSpec(finv.shape, lambda b, j: (0, 0)),
                  pl.BlockSpec((1, 2, n1, tc), lambda b, j: (b, 0, 0, j)),
                  rowblk, rowblk,
                  pl.BlockSpec((1, tc), lambda b, j: (0, 0))],
        out_specs=rowblk,
        out_shape=jax.ShapeDtypeStruct((nseq, n1h, cols), BF16),
        compiler_params=_params(2),
        name="fft_last",
    )(finv, cview, x0view, zview, skip_row)


def _dft_tables(seq_len):
    n = 2 * seq_len
    n2 = FFT_N2
    n1 = n // n2
    n1h = n1 // 2
    k1 = np.arange(n1)[:, None]
    ang1 = 2.0 * np.pi * k1 * np.arange(n1)[None, :] / n1
    f1 = np.concatenate([np.cos(ang1), -np.sin(ang1)], axis=0)
    finv = np.concatenate([np.cos(ang1[:n1h]), -np.sin(ang1[:n1h])], axis=1)
    ang2 = 2.0 * np.pi * np.arange(n2)[:, None] * np.arange(n2)[None, :] / n2
    angt = 2.0 * np.pi * k1 * np.arange(n2)[None, :] / n
    f = lambda a: jnp.asarray(a, F32)
    return dict(n=n, n1=n1, n1h=n1h, f1_full=f(f1), f1_half=f(f1[:, :n1h]), finv=f(finv),
                f2r=f(np.cos(ang2)), f2i=f(-np.sin(ang2)),
                twr=f(np.cos(angt)), twi=f(-np.sin(angt)))


def _filter_features(seq_len, fw):
    idx = np.arange(2 * seq_len)
    lag = np.where(idx < seq_len, idx, 2 * seq_len - idx).astype(np.int64)
    lag = np.minimum(lag, seq_len - 1)
    t = jnp.linspace(0.0, 1.0, seq_len, dtype=F32)[:, None]
    band = jnp.linspace(1e-4, POS_BANDS - 1, POS_BANDS, dtype=F32)
    ang = (2.0 * math.pi / seq_len) * jnp.arange(seq_len, dtype=F32)[:, None] * band[None, :]
    feats = jnp.concatenate([t, jnp.cos(ang), -jnp.sin(ang)], axis=-1)
    n_feat = feats.shape[1]
    valid = jnp.asarray((idx != seq_len).astype(np.float32))[:, None]
    rows = jnp.concatenate([feats[lag], valid], axis=1)
    return jnp.pad(rows, ((0, 0), (0, fw - n_feat - 1))), n_feat


def _hyena_kf(seq_len, tabs, w1, b1, w2, b2, w3, freq):
    d_hy = w3.shape[1] // 2
    feats, n_feat = _filter_features(seq_len, LANES)
    w1p = jnp.pad(w1, ((0, LANES - w1.shape[0]), (0, 0)))
    max_decay = math.log(DECAY_TARGET) / FAST_DECAY_PCT
    min_decay = math.log(DECAY_TARGET) / SLOW_DECAY_PCT
    deltas = jnp.abs(jnp.linspace(min_decay, max_decay, d_hy, dtype=F32))
    kern = _hyena_filter(feats, w1p, b1, w2, b2, w3, freq, deltas, seq_len, n_feat)
    n, n1, n2 = tabs["n"], tabs["n1"], FFT_N2
    a = _fft_first(tabs["f1_full"], kern.reshape(1, n1, n2 * d_hy))
    kf = _fft_mid(a.reshape(1, 2, n, d_hy), tabs["twr"], tabs["twi"], tabs["f2r"], tabs["f2i"],
                  scale=1.0 / n)
    return kf[0]


def _hyena_longconv(z, x0, kf, skip, nseq, seq_len, tabs):
    c = z.shape[1]
    n, n1, n1h, n2 = tabs["n"], tabs["n1"], tabs["n1h"], FFT_N2
    zview = z.reshape(nseq, n1h, n2 * c)
    a = _fft_first(tabs["f1_half"], zview)
    cm = _fft_mid(a.reshape(nseq, 2, n, c), tabs["twr"], tabs["twi"], tabs["f2r"], tabs["f2i"], kf=kf)
    tc = min(FFT_TC, n2 * c)
    skip_row = jnp.tile(skip.astype(F32), tc // c).reshape(1, tc)
    y = _fft_last(tabs["finv"], cm.reshape(nseq, 2, n1, n2 * c), x0.reshape(nseq, n1h, n2 * c),
                  zview, skip_row)
    return y.reshape(nseq * seq_len, c)


def _stack_halves(q_cols, lhs_scr, tq):
    lane = lax.broadcasted_iota(jnp.int32, (tq, LANES), 1)
    low = lane < HEAD_DIM
    zero = jnp.zeros((tq, LANES), lhs_scr.dtype)
    for c, q in enumerate(q_cols):
        lhs_scr[(2 * c) * tq:(2 * c + 1) * tq, :] = jnp.where(low, q, zero)
        lhs_scr[(2 * c + 1) * tq:(2 * c + 2) * tq, :] = jnp.where(low, zero, q)


def _online_softmax_step(s, v, m_scr, l_scr, acc_scr):
    tk = s.shape[1]
    m_prev = m_scr[...]
    m_new = jnp.maximum(m_prev, jnp.max(s, axis=1, keepdims=True))
    p = jnp.exp(s - jnp.concatenate([m_new] * (tk // LANES), axis=1))
    alpha = jnp.exp(m_prev - m_new)
    l_scr[...] = alpha * l_scr[...] + jnp.sum(p, axis=1, keepdims=True)
    acc_scr[...] = alpha * acc_scr[...] + _dot(p.astype(BF16), v)
    m_scr[...] = m_new


def _init_softmax(m_scr, l_scr, acc_scr):
    m_scr[...] = jnp.full_like(m_scr, NEG_BIG)
    l_scr[...] = jnp.zeros_like(l_scr)
    acc_scr[...] = jnp.zeros_like(acc_scr)


def _qk(lhs, k):
    return lax.dot_general(lhs, k, (((1,), (1,)), ((), ())), preferred_element_type=F32)


def _gqa_body(q_ref, k_ref, v_ref, o_ref, lhs_scr, m_scr, l_scr, acc_scr, *, tq, tk, n_kv):
    n_col = q_ref.shape[1] // LANES
    _stack_halves([q_ref[:, c * LANES:(c + 1) * LANES] for c in range(n_col)], lhs_scr, tq)
    _init_softmax(m_scr, l_scr, acc_scr)

    def step(t, carry):
        rows = pl.ds(pl.multiple_of(t * tk, tk), tk)
        s = _qk(lhs_scr[...], k_ref[rows, :])
        _online_softmax_step(s, v_ref[rows, :], m_scr, l_scr, acc_scr)
        return carry

    lax.fori_loop(0, n_kv, step, 0)
    o = acc_scr[...] / l_scr[...]
    low = lax.broadcasted_iota(jnp.int32, (tq, LANES), 1) < HEAD_DIM
    for c in range(n_col):
        o_ref[:, c * LANES:(c + 1) * LANES] = jnp.where(
            low, o[(2 * c) * tq:(2 * c + 1) * tq], o[(2 * c + 1) * tq:(2 * c + 2) * tq]
        ).astype(o_ref.dtype)


def _gqa(q, k, v, nseq, seq_len, row_off):
    dq = q.shape[1]
    tq = min(GQA_TQ, seq_len)
    tk = min(GQA_TK, seq_len)
    nq = seq_len // tq
    m = 2 * (dq // LANES) * tq
    qoff = row_off // tq
    soff = row_off // seq_len
    return pl.pallas_call(
        functools.partial(_gqa_body, tq=tq, tk=tk, n_kv=seq_len // tk),
        grid=(nseq, nq),
        in_specs=[pl.BlockSpec((tq, dq), lambda b, i: (qoff + b * nq + i, 0)),
                  pl.BlockSpec((seq_len, LANES), lambda b, i: (soff + b, 0)),
                  pl.BlockSpec((seq_len, LANES), lambda b, i: (soff + b, 0))],
        out_specs=pl.BlockSpec((tq, dq), lambda b, i: (b * nq + i, 0)),
        out_shape=jax.ShapeDtypeStruct((nseq * seq_len, dq), BF16),
        scratch_shapes=[pltpu.VMEM((m, LANES), BF16)] + [pltpu.VMEM((m, LANES), F32)] * 3,
        compiler_params=_params(2),
        name="gqa_attn",
    )(q, k, v)


def _bias_span(tq, tk):
    u_lo = -((MAX_DISTANCE + tk - 1 + tq - 1) // tq)
    u_hi = (MAX_DISTANCE + tq - 1 + tq - 1) // tq
    return u_lo, u_hi


def _t5_bucket(rel):
    nb = N_BUCKETS // 2
    max_exact = nb // 2
    ret = jnp.where(rel > 0, nb, 0)
    n = jnp.abs(rel)
    nf = jnp.maximum(n, 1).astype(F32)
    large = max_exact + (jnp.log(nf / max_exact) / math.log(MAX_DISTANCE / max_exact)
                         * (nb - max_exact)).astype(jnp.int32)
    large = jnp.minimum(large, nb - 1)
    return ret + jnp.where(n < max_exact, n, large)


def _bias_body(tab_ref, o_ref, *, tq, tk, u_lo):
    h = pl.program_id(0)
    u = pl.program_id(1) + u_lo
    rel = (u * tq + lax.broadcasted_iota(jnp.int32, (tq, tk), 1)
           - lax.broadcasted_iota(jnp.int32, (tq, tk), 0))
    bucket = _t5_bucket(rel)
    bias = jnp.zeros((tq, tk), F32)
    for b in range(N_BUCKETS):
        bias = jnp.where(bucket == b, tab_ref[b, h], bias)
    o_ref[0, 0] = bias


def _bias_tiles(rel_bias, tq, tk):
    u_lo, u_hi = _bias_span(tq, tk)
    n_off = u_hi - u_lo + 1
    n_heads = rel_bias.shape[1]
    return pl.pallas_call(
        functools.partial(_bias_body, tq=tq, tk=tk, u_lo=u_lo),
        grid=(n_heads, n_off),
        in_specs=[pl.BlockSpec(memory_space=pltpu.SMEM)],
        out_specs=pl.BlockSpec((1, 1, tq, tk), lambda h, u: (h, u, 0, 0)),
        out_shape=jax.ShapeDtypeStruct((n_heads, n_off, tq, tk), F32),
        compiler_params=_params(2),
        name="t5_bias",
    )(rel_bias.astype(F32))


def _diff_body(lam_ref, q_ref, k_ref, v_ref, bias_ref, g_ref, o_ref,
               lhs_scr, m_scr, l_scr, acc_scr, *, tq, tk, n_kv, u_lo, u_hi, out_scale):
    qi = pl.program_id(2)
    _stack_halves([q_ref[...]], lhs_scr, tq)
    _init_softmax(m_scr, l_scr, acc_scr)

    def step(t, carry):
        rows = pl.ds(pl.multiple_of(t * tk, tk), tk)
        s = _qk(lhs_scr[...], k_ref[rows, :])
        u = jnp.clip(t * (tk // tq) - qi, u_lo, u_hi) - u_lo
        s = (s.reshape(2, tq, tk) + bias_ref[0, u][None]).reshape(2 * tq, tk)
        _online_softmax_step(s, v_ref[rows, :], m_scr, l_scr, acc_scr)
        return carry

    lax.fori_loop(0, n_kv, step, 0)
    o = acc_scr[...] / l_scr[...]
    o = o[:tq] - lam_ref[0] * o[tq:]
    o_ref[...] = (_rms(o, g_ref[...], DIFF_SUBLN_EPS) * out_scale).astype(o_ref.dtype)


def _diff(lam, q, k, v, bias, subln_g, nseq, seq_len, row_off, lam_init, tq, tk):
    n_heads = q.shape[1] // LANES
    nq = seq_len // tq
    qoff = row_off // tq
    soff = row_off // seq_len
    u_lo, u_hi = _bias_span(tq, tk)
    n_off = u_hi - u_lo + 1
    kv_spec = pl.BlockSpec((seq_len, LANES), lambda b, h, i: (soff + b, h))
    return pl.pallas_call(
        functools.partial(_diff_body, tq=tq, tk=tk, n_kv=seq_len // tk, u_lo=u_lo, u_hi=u_hi,
                          out_scale=1.0 - lam_init),
        grid=(nseq, n_heads, nq),
        in_specs=[pl.BlockSpec(memory_space=pltpu.SMEM),
                  pl.BlockSpec((tq, LANES), lambda b, h, i: (qoff + b * nq + i, h)),
                  kv_spec, kv_spec,
                  pl.BlockSpec((1, n_off, tq, tk), lambda b, h, i: (h, 0, 0, 0)),
                  pl.BlockSpec((1, LANES), lambda b, h, i: (0, 0))],
        out_specs=pl.BlockSpec((tq, LANES), lambda b, h, i: (b * nq + i, h)),
        out_shape=jax.ShapeDtypeStruct((nseq * seq_len, q.shape[1]), BF16),
        scratch_shapes=[pltpu.VMEM((2 * tq, LANES), BF16)] + [pltpu.VMEM((2 * tq, LANES), F32)] * 3,
        compiler_params=_params(3),
        name="diff_attn",
    )(lam, q, k, v, bias, subln_g.reshape(1, LANES))


def _merge_body(x_ref, yh_ref, yg_ref, yd_ref, gate_ref, wb_ref, wo_ref, o_ref):
    d = x_ref.shape[1]
    merged = (gate_ref[:, 0:d].astype(F32) * _dot(yh_ref[...], wb_ref[0])
              + gate_ref[:, d:2 * d].astype(F32) * _dot(yg_ref[...], wb_ref[1])
              + gate_ref[:, 2 * d:3 * d].astype(F32) * _dot(yd_ref[...], wb_ref[2]))
    o_ref[...] = x_ref[...] + _dot(merged.astype(BF16), wo_ref[...])


def _merge(x, y_hy, y_gqa, y_diff, gates, wb, wo):
    t, d = x.shape
    db = y_hy.shape[1]
    tm = min(MERGE_TM, t)
    row = lambda i: (i, 0)
    return pl.pallas_call(
        _merge_body,
        grid=(t // tm,),
        in_specs=[pl.BlockSpec((tm, d), row),
                  pl.BlockSpec((tm, db), row), pl.BlockSpec((tm, db), row), pl.BlockSpec((tm, db), row),
                  pl.BlockSpec((tm, 3 * d), row),
                  pl.BlockSpec(wb.shape, lambda i: (0, 0, 0)),
                  pl.BlockSpec(wo.shape, lambda i: (0, 0))],
        out_specs=pl.BlockSpec((tm, d), row),
        out_shape=jax.ShapeDtypeStruct((t, d), F32),
        compiler_params=_params(1),
        name="merge",
    )(x, y_hy, y_gqa, y_diff, gates, wb, wo)


def _rope_tables(max_len):
    pos = np.arange(max_len)
    half = HEAD_DIM // 2
    inv = ROPE_THETA ** (-jnp.arange(0, half, 2, dtype=F32) / half)
    row = jnp.asarray(pos // GRID_W, F32)
    col = jnp.asarray(pos % GRID_W, F32)
    ang = jnp.concatenate([row[:, None] * inv, col[:, None] * inv], axis=-1)
    cos, sin = jnp.cos(ang), jnp.sin(ang)
    cos_h = jnp.concatenate([cos, cos], axis=1)
    sin_h = jnp.concatenate([-sin, sin], axis=1)
    return jnp.tile(cos_h, (1, LANES // HEAD_DIM)), jnp.tile(sin_h, (1, LANES // HEAD_DIM))


def _column_layout(d_model, d_hy):
    d_gqa = N_Q_HEADS * HEAD_DIM
    d_kv = N_KV_HEADS * HEAD_DIM
    d_diff = N_DIFF_HEADS * 2 * HEAD_DIM
    widths = [3 * d_hy, d_gqa, d_kv, d_kv, d_diff, d_diff, d_diff, 3 * d_model]
    offs = [0]
    for w in widths:
        offs.append(offs[-1] + w)
    deint = np.concatenate([np.arange(0, HEAD_DIM, 2), np.arange(1, HEAD_DIM, 2)])
    q_heads = [kv * GQA_GROUP + g for g in range(GQA_GROUP) for kv in range(N_KV_HEADS)]
    q_perm = np.concatenate([h * HEAD_DIM + deint for h in q_heads])
    k_perm = np.concatenate([h * HEAD_DIM + deint for h in range(N_KV_HEADS)])
    perm = np.arange(offs[-1])
    perm[offs[1]:offs[2]] = offs[1] + q_perm
    perm[offs[2]:offs[3]] = offs[2] + k_perm
    out_rows = np.concatenate([h * HEAD_DIM + np.arange(HEAD_DIM) for h in q_heads])
    return offs, perm, deint, q_heads, out_rows


def kernel(x_prompt, x_sample, ffn1_norm, ffn1_w_in, ffn1_w_out, mix_norm, w_in, hy_conv_w, hy_conv_b, hy_filt_w1, hy_filt_b1, hy_filt_w2, hy_filt_b2, hy_filt_w3, hy_filt_freq, hy_skip, gqa_q_norm, gqa_k_norm, diff_lambda, diff_subln, rel_bias, w_branch, w_out, ffn2_norm, ffn2_w_in, ffn2_w_out, final_norm):
    b1, l1, d = x_prompt.shape
    b2, l2, _ = x_sample.shape
    depth = w_in.shape[0]
    d_hy = hy_skip.shape[1]
    bl1 = b1 * l1
    groups = ((bl1, l1), (b2 * l2, l2))
    assert bl1 % l2 == 0 and l1 % GRID_W == 0 and l2 % GRID_W == 0

    x = jnp.concatenate([x_prompt.reshape(bl1, d), x_sample.reshape(b2 * l2, d)], axis=0)

    offs, perm, deint, q_heads, out_rows = _column_layout(d, d_hy)
    cos_t, sin_t = _rope_tables(max(l1, l2))
    bd = jnp.asarray(np.kron(np.eye(LANES // HEAD_DIM), np.full((HEAD_DIM, HEAD_DIM), 1.0 / HEAD_DIM)), BF16)
    tabs = {l: _dft_tables(l) for l in sorted({l1, l2})}
    seq_groups = [(b1, l1, 0), (b2, l2, bl1)]
    diff_tiles = {l: (min(DIFF_TQ, l), min(DIFF_TK, l)) for l in (l1, l2)}
    bias = {tt: _bias_tiles(rel_bias, *tt) for tt in sorted(set(diff_tiles.values()))}

    for l in range(depth):
        x = _ffn(x, ffn1_norm[l], ffn1_w_in[l].astype(BF16), ffn1_w_out[l].astype(BF16))

        w_l = w_in[l][:, perm].astype(BF16)
        qn = jnp.tile(gqa_q_norm[l][deint], N_Q_HEADS).reshape(1, -1)
        kn = jnp.tile(gqa_k_norm[l][deint], N_KV_HEADS).reshape(1, -1)
        hy, gq, gk, gv, dq, dk, dv, gates = _inproj(x, mix_norm[l], w_l, cos_t, sin_t, bd, qn, kn,
                                                     offs, groups)

        z, x0 = _shortconv(hy, hy_conv_w[l], hy_conv_b[l], groups)
        lp = diff_lambda[l].astype(F32)
        lam_init = 0.8 - 0.6 * math.exp(-0.3 * l)
        lam = (jnp.exp(jnp.sum(lp[0] * lp[1])) - jnp.exp(jnp.sum(lp[2] * lp[3])) + lam_init).reshape(1)

        y_hy, y_gqa, y_diff = [], [], []
        for nseq, sl, off in seq_groups:
            rows = slice(off, off + nseq * sl)
            kf = _hyena_kf(sl, tabs[sl], hy_filt_w1[l], hy_filt_b1[l], hy_filt_w2[l], hy_filt_b2[l],
                           hy_filt_w3[l], hy_filt_freq[l])
            y_hy.append(_hyena_longconv(z[rows], x0[rows], kf, hy_skip[l], nseq, sl, tabs[sl]))
            y_gqa.append(_gqa(gq, gk, gv, nseq, sl, off))
            tq, tk = diff_tiles[sl]
            y_diff.append(_diff(lam, dq, dk, dv, bias[(tq, tk)], diff_subln[l], nseq, sl, off,
                                lam_init, tq, tk))
        y_hy = jnp.concatenate(y_hy, axis=0)
        y_gqa = jnp.concatenate(y_gqa, axis=0)
        y_diff = jnp.concatenate(y_diff, axis=0)

        wb = w_branch[l].at[1].set(w_branch[l][1][out_rows]).astype(BF16)
        x = _merge(x, y_hy, y_gqa, y_diff, gates, wb, w_out[l].astype(BF16))
        x = _ffn(x, ffn2_norm[l], ffn2_w_in[l].astype(BF16), ffn2_w_out[l].astype(BF16),
                 final_g=final_norm if l == depth - 1 else None)

    return x[:bl1].reshape(b1, l1, d), x[bl1:].reshape(b2, l2, d)
```

```python
import functools
import math

import numpy as np
import jax
import jax.numpy as jnp
from jax import lax
from jax.experimental import pallas as pl
from jax.experimental.pallas import tpu as pltpu

F32 = jnp.float32
BF16 = jnp.bfloat16

NORM_EPS = 1e-6
GRID_W = 64
HEAD_DIM = 64
N_Q_HEADS = 8
N_KV_HEADS = 2
GQA_GROUP = N_Q_HEADS // N_KV_HEADS
ROPE_THETA = 10000.0
N_DIFF_HEADS = 4
DIFF_SUBLN_EPS = 1e-5
N_BUCKETS = 32
MAX_DISTANCE = 128
POS_BANDS = 16
DECAY_TARGET = 1e-2
FAST_DECAY_PCT = 0.3
SLOW_DECAY_PCT = 1.5

LANES = 128
VMEM_LIMIT_BYTES = 56 * 1024 * 1024

FFN_TM = 512
PROJ_TM = 256
MERGE_TM = 512
CONV_TM = 512
GQA_TQ = 128
GQA_TK = 1024
DIFF_TQ = 512
DIFF_TK = 1024
FFT_N2 = 128
FFT_TC = 2048
FILT_TM = 512

HIGHEST = lax.Precision.HIGHEST
NEG_BIG = -1e30
LOG2E = math.log2(math.e)
KV_UNROLL = 4


def _params(n_axes):
    return pltpu.CompilerParams(dimension_semantics=("arbitrary",) * n_axes,
                                vmem_limit_bytes=VMEM_LIMIT_BYTES)


def _dot(a, b):
    return jnp.dot(a, b, preferred_element_type=F32)


def _dot_hp(a, b):
    return jnp.dot(a, b, precision=HIGHEST, preferred_element_type=F32)


def _split_bf16(x):
    hi = x.astype(BF16)
    return hi, (x - hi.astype(F32)).astype(BF16)


def _dot3(a, b):
    ah, al = _split_bf16(a)
    bh, bl = _split_bf16(b)
    return _dot(ah, bh) + (_dot(ah, bl) + _dot(al, bh))


def _rms(x, g, eps):
    ms = jnp.mean(x * x, axis=-1, keepdims=True)
    return x * lax.rsqrt(ms + eps) * g


def _ffn_body(*refs, n_ff, final):
    if final:
        x_ref, g_ref, wg_ref, wu_ref, wo_ref, gf_ref, o_ref, h_scr, acc_scr = refs
    else:
        x_ref, g_ref, wg_ref, wu_ref, wo_ref, o_ref, h_scr, acc_scr = refs
    j = pl.program_id(1)

    @pl.when(j == 0)
    def _():
        h_scr[...] = _rms(x_ref[...], g_ref[...], NORM_EPS).astype(BF16)
        acc_scr[...] = jnp.zeros_like(acc_scr)

    h = h_scr[...]
    gate = _dot(h, wg_ref[...])
    up = _dot(h, wu_ref[...])
    a = (gate * jax.nn.sigmoid(gate) * up).astype(BF16)
    acc_scr[...] += _dot(a, wo_ref[...])

    @pl.when(j == n_ff - 1)
    def _():
        y = x_ref[...] + 0.5 * acc_scr[...]
        if final:
            y = _rms(y, gf_ref[...], NORM_EPS)
        o_ref[...] = y


def _ffn_tile(d_ff):
    for n in (2, 4, 1):
        if d_ff % n == 0 and (d_ff // n) % LANES == 0:
            return d_ff // n
    return d_ff


def _ffn(x, norm_g, w_in, w_out, final_g=None):
    t, d = x.shape
    d_ff = w_out.shape[0]
    tf = _ffn_tile(d_ff)
    n_ff = d_ff // tf
    tm = min(FFN_TM, t)
    final = final_g is not None
    in_specs = [
        pl.BlockSpec((tm, d), lambda i, j: (i, 0)),
        pl.BlockSpec((1, d), lambda i, j: (0, 0)),
        pl.BlockSpec((d, tf), lambda i, j: (0, j)),
        pl.BlockSpec((d, tf), lambda i, j: (0, j + n_ff)),
        pl.BlockSpec((tf, d), lambda i, j: (j, 0)),
    ]
    args = [x, norm_g.reshape(1, d), w_in, w_in, w_out]
    if final:
        in_specs.append(pl.BlockSpec((1, d), lambda i, j: (0, 0)))
        args.append(final_g.reshape(1, d))
    return pl.pallas_call(
        functools.partial(_ffn_body, n_ff=n_ff, final=final),
        grid=(t // tm, n_ff),
        in_specs=in_specs,
        out_specs=pl.BlockSpec((tm, d), lambda i, j: (i, 0)),
        out_shape=jax.ShapeDtypeStruct((t, d), F32),
        scratch_shapes=[pltpu.VMEM((tm, d), BF16), pltpu.VMEM((tm, d), F32)],
        compiler_params=_params(2),
        name="ffn",
    )(*args)


def _head_rms(x, bd):
    x2 = x * x
    hi = x2.astype(BF16)
    lo = (x2 - hi.astype(F32)).astype(BF16)
    ms = _dot(hi, bd) + _dot(lo, bd)
    return x * lax.rsqrt(ms + NORM_EPS)


def _rope(x, cos, sin_signed):
    lane = lax.broadcasted_iota(jnp.int32, x.shape, 1)
    partner = jnp.where((lane % HEAD_DIM) < HEAD_DIM // 2,
                        pltpu.roll(x, LANES - HEAD_DIM // 2, 1),
                        pltpu.roll(x, HEAD_DIM // 2, 1))
    return x * cos + partner * sin_signed


def _inproj_body(x_ref, g_ref, w_ref, cos_ref, sin_ref, bd_ref, qn_ref, kn_ref,
                 hy_ref, gq_ref, gk_ref, gv_ref, dq_ref, dk_ref, dv_ref, gate_ref, *, offs):
    o_hy, o_gq, o_gk, o_gv, o_dq, o_dk, o_dv, o_gate, o_end = offs
    h = _rms(x_ref[...], g_ref[...], NORM_EPS).astype(BF16)

    def seg(a, b):
        return _dot(h, w_ref[:, a:b])

    hy_ref[...] = seg(o_hy, o_gq)
    cos = cos_ref[...]
    sin = sin_ref[...]
    bd = bd_ref[...]
    scale = HEAD_DIM ** -0.5 * LOG2E
    for c in range((o_gk - o_gq) // LANES):
        a = o_gq + c * LANES
        y = _head_rms(seg(a, a + LANES), bd) * qn_ref[:, c * LANES:(c + 1) * LANES]
        gq_ref[:, c * LANES:(c + 1) * LANES] = (_rope(y, cos, sin) * scale).astype(BF16)
    for c in range((o_gv - o_gk) // LANES):
        a = o_gk + c * LANES
        y = _head_rms(seg(a, a + LANES), bd) * kn_ref[:, c * LANES:(c + 1) * LANES]
        gk_ref[:, c * LANES:(c + 1) * LANES] = _rope(y, cos, sin).astype(BF16)
    gv_ref[...] = seg(o_gv, o_dq).astype(BF16)
    dq_ref[...] = (seg(o_dq, o_dk) * scale).astype(BF16)
    dk_ref[...] = seg(o_dk, o_dv).astype(BF16)
    dv_ref[...] = seg(o_dv, o_gate).astype(BF16)
    gate_ref[...] = jax.nn.sigmoid(seg(o_gate, o_end)).astype(BF16)


def _inproj(x, norm_g, w, cos_t, sin_t, bd, qn, kn, offs, groups):
    t, d = x.shape
    tm = min(PROJ_TM, t)
    widths = [offs[i + 1] - offs[i] for i in range(8)]
    (bl, l1), (_, l2) = groups
    nbp = bl // tm

    def pos_map(i):
        return (jnp.where(i < nbp, i % (l1 // tm), (i - nbp) % (l2 // tm)), 0)

    full = lambda i: (0, 0)
    row = lambda i: (i, 0)
    out_dtypes = [F32] + [BF16] * 7
    return pl.pallas_call(
        functools.partial(_inproj_body, offs=tuple(offs)),
        grid=(t // tm,),
        in_specs=[
            pl.BlockSpec((tm, d), row),
            pl.BlockSpec((1, d), full),
            pl.BlockSpec(w.shape, full),
            pl.BlockSpec((tm, LANES), pos_map),
            pl.BlockSpec((tm, LANES), pos_map),
            pl.BlockSpec(bd.shape, full),
            pl.BlockSpec(qn.shape, full),
            pl.BlockSpec(kn.shape, full),
        ],
        out_specs=[pl.BlockSpec((tm, wd), row) for wd in widths],
        out_shape=[jax.ShapeDtypeStruct((t, wd), dt) for wd, dt in zip(widths, out_dtypes)],
        compiler_params=_params(1),
        name="inproj",
    )(x, norm_g.reshape(1, d), w, cos_t, sin_t, bd, qn, kn)


def _shortconv_body(u_ref, prev_ref, next_ref, w_ref, b_ref, z_ref, x0_ref, *, tm, groups, d_hy):
    i = pl.program_id(0)
    row0 = i * tm
    (bl, l1), (_, l2) = groups
    pos = jnp.where(row0 < bl, row0 % l1, (row0 - bl) % l2)
    seq_len = jnp.where(row0 < bl, l1, l2)
    u = u_ref[...]
    rows = lax.broadcasted_iota(jnp.int32, u.shape, 0)
    before = jnp.where(pos == 0, 0.0, prev_ref[7:8, :])
    after = jnp.where(pos + tm == seq_len, 0.0, next_ref[0:1, :])
    up = jnp.where(rows == 0, before, pltpu.roll(u, 1, 0))
    dn = jnp.where(rows == tm - 1, after, pltpu.roll(u, tm - 1, 0))
    y = up * w_ref[0:1, :] + u * w_ref[1:2, :] + dn * w_ref[2:3, :] + b_ref[...]
    x0_ref[...] = y[:, :d_hy]
    z_ref[...] = y[:, 2 * d_hy:] * y[:, d_hy:2 * d_hy]


def _shortconv(u, w, b, groups):
    t, c3 = u.shape
    d_hy = c3 // 3
    tm = min(CONV_TM, t)
    nb8 = t // 8
    return pl.pallas_call(
        functools.partial(_shortconv_body, tm=tm, groups=groups, d_hy=d_hy),
        grid=(t // tm,),
        in_specs=[
            pl.BlockSpec((tm, c3), lambda i: (i, 0)),
            pl.BlockSpec((8, c3), lambda i: (jnp.maximum(i * (tm // 8) - 1, 0), 0)),
            pl.BlockSpec((8, c3), lambda i: (jnp.minimum((i + 1) * (tm // 8), nb8 - 1), 0)),
            pl.BlockSpec((3, c3), lambda i: (0, 0)),
            pl.BlockSpec((1, c3), lambda i: (0, 0)),
        ],
        out_specs=[pl.BlockSpec((tm, d_hy), lambda i: (i, 0))] * 2,
        out_shape=[jax.ShapeDtypeStruct((t, d_hy), F32)] * 2,
        compiler_params=_params(1),
        name="shortconv",
    )(u, u, u, w, b.reshape(1, c3))


def _filter_body(f_ref, w1_ref, b1_ref, w2_ref, b2_ref, w3_ref, fr_ref, dl_ref, o_ref, *, n_feat):
    f = f_ref[...]
    fr = fr_ref[...]
    h = jnp.sin(fr * (_dot_hp(f, w1_ref[...]) + b1_ref[...]))
    h = jnp.sin(fr * (_dot_hp(h, w2_ref[...]) + b2_ref[...]))
    h = _dot_hp(h, w3_ref[...])
    tpos = f[:, 0:1]
    valid = f[:, n_feat:n_feat + 1]
    o_ref[...] = h * jnp.exp(-tpos * dl_ref[...]) * valid


def _hyena_filter(feats, w1p, b1, w2, b2, w3, freq, deltas_abs, seq_len, n_feat):
    n2l, fw = feats.shape
    d_hy = w3.shape[1] // 2
    hid = w2.shape[0]
    tm = min(FILT_TM, seq_len)
    nbl = seq_len // tm
    full = lambda i: (0, 0)
    return pl.pallas_call(
        functools.partial(_filter_body, n_feat=n_feat),
        grid=(n2l // tm,),
        in_specs=[
            pl.BlockSpec((tm, fw), lambda i: (i, 0)),
            pl.BlockSpec(w1p.shape, full),
            pl.BlockSpec((1, hid), full),
            pl.BlockSpec(w2.shape, full),
            pl.BlockSpec((1, hid), full),
            pl.BlockSpec((hid, d_hy), lambda i: (0, i // nbl)),
            pl.BlockSpec((1, hid), full),
            pl.BlockSpec((1, d_hy), full),
        ],
        out_specs=pl.BlockSpec((tm, d_hy), lambda i: (i, 0)),
        out_shape=jax.ShapeDtypeStruct((n2l, d_hy), F32),
        compiler_params=_params(1),
        name="hyena_filter",
    )(feats, w1p, b1.reshape(1, hid), w2, b2.reshape(1, hid), w3, freq.reshape(1, hid),
      deltas_abs.reshape(1, d_hy))


def _fft_outer_body(f_ref, x_ref, o_ref):
    o_ref[0] = _dot3(f_ref[...], x_ref[0])


def _fft_first(fmat, xview):
    nseq, k, cols = xview.shape
    m = fmat.shape[0]
    tc = min(FFT_TC, cols)
    return pl.pallas_call(
        _fft_outer_body,
        grid=(nseq, cols // tc),
        in_specs=[pl.BlockSpec((m, k), lambda b, j: (0, 0)),
                  pl.BlockSpec((1, k, tc), lambda b, j: (b, 0, j))],
        out_specs=pl.BlockSpec((1, m, tc), lambda b, j: (b, 0, j)),
        out_shape=jax.ShapeDtypeStruct((nseq, m, cols), F32),
        compiler_params=_params(2),
        name="fft_first",
    )(fmat, xview)


def _stack_complex(re, im):
    return jnp.concatenate([jnp.concatenate([re, -im], axis=1),
                            jnp.concatenate([im, re], axis=1)], axis=0)


def _fft_mid_body(*refs, conv, scale):
    if conv:
        a_ref, kf_ref, twr_ref, twi_ref, f2r_ref, f2i_ref, o_ref = refs
    else:
        a_ref, twr_ref, twi_ref, f2r_ref, f2i_ref, o_ref = refs
    n2 = f2r_ref.shape[0]
    c = a_ref.shape[-1]
    f2r = f2r_ref[...]
    f2i = f2i_ref[...]
    twr = twr_ref[0]
    twi = twi_ref[0]
    gr = f2r * twr - f2i * twi
    gi = f2r * twi + f2i * twr
    a = a_ref[0].reshape(2 * n2, c)
    x = _dot3(_stack_complex(gr, gi), a)
    if not conv:
        o_ref[0] = (x * scale).reshape(2, n2, c)
        return
    xr, xi = x[:n2], x[n2:]
    kr, ki = kf_ref[0], kf_ref[1]
    y = jnp.concatenate([xr * kr - xi * ki, xr * ki + xi * kr], axis=0)
    twr_c = jnp.transpose(jnp.broadcast_to(twr, (n2, n2)))
    twi_c = jnp.transpose(jnp.broadcast_to(twi, (n2, n2)))
    gtr = f2r * twr_c - f2i * twi_c
    gti = f2r * twi_c + f2i * twr_c
    o_ref[0] = _dot3(_stack_complex(gtr, -gti), y).reshape(2, n2, c)


def _fft_mid(a, tw_r, tw_i, f2r, f2i, kf=None, scale=1.0):
    nseq, _, n, c = a.shape
    n2 = f2r.shape[0]
    n1 = n // n2
    conv = kf is not None
    blk = pl.BlockSpec((1, 2, n2, c), lambda k, b: (b, 0, k, 0))
    tw_spec = pl.BlockSpec((1, 1, n2), lambda k, b: (k, 0, 0))
    f_spec = pl.BlockSpec((n2, n2), lambda k, b: (0, 0))
    in_specs = [blk]
    args = [a]
    if conv:
        in_specs.append(pl.BlockSpec((2, n2, c), lambda k, b: (0, k, 0)))
        args.append(kf)
    in_specs += [tw_spec, tw_spec, f_spec, f_spec]
    args += [tw_r.reshape(n1, 1, n2), tw_i.reshape(n1, 1, n2), f2r, f2i]
    return pl.pallas_call(
        functools.partial(_fft_mid_body, conv=conv, scale=scale),
        grid=(n1, nseq),
        in_specs=in_specs,
        out_specs=blk,
        out_shape=jax.ShapeDtypeStruct(a.shape, F32),
        compiler_params=_params(2),
        name="fft_mid",
    )(*args)


def _fft_last_body(f_ref, c_ref, x0_ref, z_ref, skip_ref, o_ref):
    m2, tc = c_ref.shape[1] * c_ref.shape[2], c_ref.shape[3]
    y = _dot3(f_ref[...], c_ref[0].reshape(m2, tc))
    o_ref[0] = (x0_ref[0] * (y + z_ref[0] * skip_ref[...])).astype(o_ref.dtype)


def _fft_last(finv, cview, x0view, zview, skip_row):
    nseq, _, n1, cols = cview.shape
    n1h = finv.shape[0]
    tc = min(FFT_TC, cols)
    rowblk = pl.BlockSpec((1, n1h, tc), lambda b, j: (b, 0, j))
    return pl.pallas_call(
        _fft_last_body,
        grid=(nseq, cols // tc),
        in_specs=[pl.BlockSpec(finv.shape, lambda b, j: (0, 0)),
                  pl.BlockSpec((1, 2, n1, tc), lambda b, j: (b, 0, 0, j)),
                  rowblk, rowblk,
                  pl.BlockSpec((1, tc), lambda b, j: (0, 0))],
        out_specs=rowblk,
        out_shape=jax.ShapeDtypeStruct((nseq, n1h, cols), BF16),
        compiler_params=_params(2),
        name="fft_last",
    )(finv, cview, x0view, zview, skip_row)


def _dft_tables(seq_len):
    n = 2 * seq_len
    n2 = FFT_N2
    n1 = n // n2
    n1h = n1 // 2
    k1 = np.arange(n1)[:, None]
    ang1 = 2.0 * np.pi * k1 * np.arange(n1)[None, :] / n1
    f1 = np.concatenate([np.cos(ang1), -np.sin(ang1)], axis=0)
    finv = np.concatenate([np.cos(ang1[:n1h]), -np.sin(ang1[:n1h])], axis=1)
    ang2 = 2.0 * np.pi * np.arange(n2)[:, None] * np.arange(n2)[None, :] / n2
    angt = 2.0 * np.pi * k1 * np.arange(n2)[None, :] / n
    f = lambda a: jnp.asarray(a, F32)
    return dict(n=n, n1=n1, n1h=n1h, f1_full=f(f1), f1_half=f(f1[:, :n1h]), finv=f(finv),
                f2r=f(np.cos(ang2)), f2i=f(-np.sin(ang2)),
                twr=f(np.cos(angt)), twi=f(-np.sin(angt)))


def _filter_features(seq_len, fw):
    idx = np.arange(2 * seq_len)
    lag = np.where(idx < seq_len, idx, 2 * seq_len - idx).astype(np.int64)
    lag = np.minimum(lag, seq_len - 1)
    t = jnp.linspace(0.0, 1.0, seq_len, dtype=F32)[:, None]
    band = jnp.linspace(1e-4, POS_BANDS - 1, POS_BANDS, dtype=F32)
    ang = (2.0 * math.pi / seq_len) * jnp.arange(seq_len, dtype=F32)[:, None] * band[None, :]
    feats = jnp.concatenate([t, jnp.cos(ang), -jnp.sin(ang)], axis=-1)
    n_feat = feats.shape[1]
    valid = jnp.asarray((idx != seq_len).astype(np.float32))[:, None]
    rows = jnp.concatenate([feats[lag], valid], axis=1)
    return jnp.pad(rows, ((0, 0), (0, fw - n_feat - 1))), n_feat


def _hyena_kf(seq_len, tabs, w1, b1, w2, b2, w3, freq):
    d_hy = w3.shape[1] // 2
    feats, n_feat = _filter_features(seq_len, LANES)
    w1p = jnp.pad(w1, ((0, LANES - w1.shape[0]), (0, 0)))
    max_decay = math.log(DECAY_TARGET) / FAST_DECAY_PCT
    min_decay = math.log(DECAY_TARGET) / SLOW_DECAY_PCT
    deltas = jnp.abs(jnp.linspace(min_decay, max_decay, d_hy, dtype=F32))
    kern = _hyena_filter(feats, w1p, b1, w2, b2, w3, freq, deltas, seq_len, n_feat)
    n, n1, n2 = tabs["n"], tabs["n1"], FFT_N2
    a = _fft_first(tabs["f1_full"], kern.reshape(1, n1, n2 * d_hy))
    kf = _fft_mid(a.reshape(1, 2, n, d_hy), tabs["twr"], tabs["twi"], tabs["f2r"], tabs["f2i"],
                  scale=1.0 / n)
    return kf[0]


def _hyena_longconv(z, x0, kf, skip, nseq, seq_len, tabs):
    c = z.shape[1]
    n, n1, n1h, n2 = tabs["n"], tabs["n1"], tabs["n1h"], FFT_N2
    zview = z.reshape(nseq, n1h, n2 * c)
    a = _fft_first(tabs["f1_half"], zview)
    cm = _fft_mid(a.reshape(nseq, 2, n, c), tabs["twr"], tabs["twi"], tabs["f2r"], tabs["f2i"], kf=kf)
    tc = min(FFT_TC, n2 * c)
    skip_row = jnp.tile(skip.astype(F32), tc // c).reshape(1, tc)
    y = _fft_last(tabs["finv"], cm.reshape(nseq, 2, n1, n2 * c), x0.reshape(nseq, n1h, n2 * c),
                  zview, skip_row)
    return y.reshape(nseq * seq_len, c)


def _stack_halves(q_cols, lhs_scr, tq):
    lane = lax.broadcasted_iota(jnp.int32, (tq, LANES), 1)
    low = lane < HEAD_DIM
    zero = jnp.zeros((tq, LANES), lhs_scr.dtype)
    for c, q in enumerate(q_cols):
        lhs_scr[(2 * c) * tq:(2 * c + 1) * tq, :] = jnp.where(low, q, zero)
        lhs_scr[(2 * c + 1) * tq:(2 * c + 2) * tq, :] = jnp.where(low, zero, q)


def _online_softmax_step(s, v, m_scr, l_scr, acc_scr):
    tk = s.shape[1]
    m_prev = m_scr[...]
    m_new = jnp.maximum(m_prev, jnp.max(s, axis=1, keepdims=True))
    p = jnp.exp2(s - jnp.concatenate([m_new] * (tk // LANES), axis=1))
    alpha = jnp.exp2(m_prev - m_new)
    l_scr[...] = alpha * l_scr[...] + jnp.sum(p, axis=1, keepdims=True)
    acc_scr[...] = alpha * acc_scr[...] + _dot(p.astype(BF16), v)
    m_scr[...] = m_new


def _init_softmax(m_scr, l_scr, acc_scr):
    m_scr[...] = jnp.full_like(m_scr, NEG_BIG)
    l_scr[...] = jnp.zeros_like(l_scr)
    acc_scr[...] = jnp.zeros_like(acc_scr)


def _qk(lhs, k):
    return lax.dot_general(lhs, k, (((1,), (1,)), ((), ())), preferred_element_type=F32)


def _gqa_body(q_ref, k_ref, v_ref, o_ref, lhs_scr, m_scr, l_scr, acc_scr, *, tq, tk, n_kv):
    n_col = q_ref.shape[1] // LANES
    _stack_halves([q_ref[:, c * LANES:(c + 1) * LANES] for c in range(n_col)], lhs_scr, tq)
    _init_softmax(m_scr, l_scr, acc_scr)

    def step(t, carry):
        rows = pl.ds(pl.multiple_of(t * tk, tk), tk)
        s = _qk(lhs_scr[...], k_ref[rows, :])
        _online_softmax_step(s, v_ref[rows, :], m_scr, l_scr, acc_scr)
        return carry

    lax.fori_loop(0, n_kv, step, 0, unroll=min(KV_UNROLL, n_kv))
    o = acc_scr[...] / l_scr[...]
    low = lax.broadcasted_iota(jnp.int32, (tq, LANES), 1) < HEAD_DIM
    for c in range(n_col):
        o_ref[:, c * LANES:(c + 1) * LANES] = jnp.where(
            low, o[(2 * c) * tq:(2 * c + 1) * tq], o[(2 * c + 1) * tq:(2 * c + 2) * tq]
        ).astype(o_ref.dtype)


def _gqa(q, k, v, nseq, seq_len, row_off):
    dq = q.shape[1]
    tq = min(GQA_TQ, seq_len)
    tk = min(GQA_TK, seq_len)
    nq = seq_len // tq
    m = 2 * (dq // LANES) * tq
    qoff = row_off // tq
    soff = row_off // seq_len
    return pl.pallas_call(
        functools.partial(_gqa_body, tq=tq, tk=tk, n_kv=seq_len // tk),
        grid=(nseq, nq),
        in_specs=[pl.BlockSpec((tq, dq), lambda b, i: (qoff + b * nq + i, 0)),
                  pl.BlockSpec((seq_len, LANES), lambda b, i: (soff + b, 0)),
                  pl.BlockSpec((seq_len, LANES), lambda b, i: (soff + b, 0))],
        out_specs=pl.BlockSpec((tq, dq), lambda b, i: (b * nq + i, 0)),
        out_shape=jax.ShapeDtypeStruct((nseq * seq_len, dq), BF16),
        scratch_shapes=[pltpu.VMEM((m, LANES), BF16)] + [pltpu.VMEM((m, LANES), F32)] * 3,
        compiler_params=_params(2),
        name="gqa_attn",
    )(q, k, v)


def _bias_span(tq, tk):
    u_lo = -((MAX_DISTANCE + tk - 1 + tq - 1) // tq)
    u_hi = (MAX_DISTANCE + tq - 1 + tq - 1) // tq
    return u_lo, u_hi


def _t5_bucket(rel):
    nb = N_BUCKETS // 2
    max_exact = nb // 2
    ret = jnp.where(rel > 0, nb, 0)
    n = jnp.abs(rel)
    nf = jnp.maximum(n, 1).astype(F32)
    large = max_exact + (jnp.log(nf / max_exact) / math.log(MAX_DISTANCE / max_exact)
                         * (nb - max_exact)).astype(jnp.int32)
    large = jnp.minimum(large, nb - 1)
    return ret + jnp.where(n < max_exact, n, large)


def _bias_body(tab_ref, o_ref, *, tq, tk, u_lo):
    h = pl.program_id(0)
    u = pl.program_id(1) + u_lo
    rel = (u * tq + lax.broadcasted_iota(jnp.int32, (tq, tk), 1)
           - lax.broadcasted_iota(jnp.int32, (tq, tk), 0))
    bucket = _t5_bucket(rel)
    bias = jnp.zeros((tq, tk), F32)
    for b in range(N_BUCKETS):
        bias = jnp.where(bucket == b, tab_ref[b, h], bias)
    o_ref[0, 0] = bias * LOG2E


def _bias_tiles(rel_bias, tq, tk):
    u_lo, u_hi = _bias_span(tq, tk)
    n_off = u_hi - u_lo + 1
    n_heads = rel_bias.shape[1]
    return pl.pallas_call(
        functools.partial(_bias_body, tq=tq, tk=tk, u_lo=u_lo),
        grid=(n_heads, n_off),
        in_specs=[pl.BlockSpec(memory_space=pltpu.SMEM)],
        out_specs=pl.BlockSpec((1, 1, tq, tk), lambda h, u: (h, u, 0, 0)),
        out_shape=jax.ShapeDtypeStruct((n_heads, n_off, tq, tk), F32),
        compiler_params=_params(2),
        name="t5_bias",
    )(rel_bias.astype(F32))


def _diff_body(lam_ref, q_ref, k_ref, v_ref, bias_ref, g_ref, o_ref,
               lhs_scr, m_scr, l_scr, acc_scr, *, tq, tk, n_kv, u_lo, u_hi, out_scale):
    qi = pl.program_id(2)
    _stack_halves([q_ref[...]], lhs_scr, tq)
    _init_softmax(m_scr, l_scr, acc_scr)

    def step(t, carry):
        rows = pl.ds(pl.multiple_of(t * tk, tk), tk)
        s = _qk(lhs_scr[...], k_ref[rows, :])
        u = jnp.clip(t * (tk // tq) - qi, u_lo, u_hi) - u_lo
        s = (s.reshape(2, tq, tk) + bias_ref[0, u][None]).reshape(2 * tq, tk)
        _online_softmax_step(s, v_ref[rows, :], m_scr, l_scr, acc_scr)
        return carry

    lax.fori_loop(0, n_kv, step, 0, unroll=min(KV_UNROLL, n_kv))
    o = acc_scr[...] / l_scr[...]
    o = o[:tq] - lam_ref[0] * o[tq:]
    o_ref[...] = (_rms(o, g_ref[...], DIFF_SUBLN_EPS) * out_scale).astype(o_ref.dtype)


def _diff(lam, q, k, v, bias, subln_g, nseq, seq_len, row_off, lam_init, tq, tk):
    n_heads = q.shape[1] // LANES
    nq = seq_len // tq
    qoff = row_off // tq
    soff = row_off // seq_len
    u_lo, u_hi = _bias_span(tq, tk)
    n_off = u_hi - u_lo + 1
    kv_spec = pl.BlockSpec((seq_len, LANES), lambda b, h, i: (soff + b, h))
    return pl.pallas_call(
        functools.partial(_diff_body, tq=tq, tk=tk, n_kv=seq_len // tk, u_lo=u_lo, u_hi=u_hi,
                          out_scale=1.0 - lam_init),
        grid=(nseq, n_heads, nq),
        in_specs=[pl.BlockSpec(memory_space=pltpu.SMEM),
                  pl.BlockSpec((tq, LANES), lambda b, h, i: (qoff + b * nq + i, h)),
                  kv_spec, kv_spec,
                  pl.BlockSpec((1, n_off, tq, tk), lambda b, h, i: (h, 0, 0, 0),
                               pipeline_mode=pl.Buffered(1)),
                  pl.BlockSpec((1, LANES), lambda b, h, i: (0, 0))],
        out_specs=pl.BlockSpec((tq, LANES), lambda b, h, i: (b * nq + i, h)),
        out_shape=jax.ShapeDtypeStruct((nseq * seq_len, q.shape[1]), BF16),
        scratch_shapes=[pltpu.VMEM((2 * tq, LANES), BF16)] + [pltpu.VMEM((2 * tq, LANES), F32)] * 3,
        compiler_params=_params(3),
        name="diff_attn",
    )(lam, q, k, v, bias, subln_g.reshape(1, LANES))


def _merge_body(x_ref, yh_ref, yg_ref, yd_ref, gate_ref, wb_ref, wo_ref, o_ref):
    d = x_ref.shape[1]
    merged = (gate_ref[:, 0:d].astype(F32) * _dot(yh_ref[...], wb_ref[0])
              + gate_ref[:, d:2 * d].astype(F32) * _dot(yg_ref[...], wb_ref[1])
              + gate_ref[:, 2 * d:3 * d].astype(F32) * _dot(yd_ref[...], wb_ref[2]))
    o_ref[...] = x_ref[...] + _dot(merged.astype(BF16), wo_ref[...])


def _merge(x, y_hy, y_gqa, y_diff, gates, wb, wo):
    t, d = x.shape
    db = y_hy.shape[1]
    tm = min(MERGE_TM, t)
    row = lambda i: (i, 0)
    return pl.pallas_call(
        _merge_body,
        grid=(t // tm,),
        in_specs=[pl.BlockSpec((tm, d), row),
                  pl.BlockSpec((tm, db), row), pl.BlockSpec((tm, db), row), pl.BlockSpec((tm, db), row),
                  pl.BlockSpec((tm, 3 * d), row),
                  pl.BlockSpec(wb.shape, lambda i: (0, 0, 0)),
                  pl.BlockSpec(wo.shape, lambda i: (0, 0))],
        out_specs=pl.BlockSpec((tm, d), row),
        out_shape=jax.ShapeDtypeStruct((t, d), F32),
        compiler_params=_params(1),
        name="merge",
    )(x, y_hy, y_gqa, y_diff, gates, wb, wo)


def _rope_tables(max_len):
    pos = np.arange(max_len)
    half = HEAD_DIM // 2
    inv = ROPE_THETA ** (-jnp.arange(0, half, 2, dtype=F32) / half)
    row = jnp.asarray(pos // GRID_W, F32)
    col = jnp.asarray(pos % GRID_W, F32)
    ang = jnp.concatenate([row[:, None] * inv, col[:, None] * inv], axis=-1)
    cos, sin = jnp.cos(ang), jnp.sin(ang)
    cos_h = jnp.concatenate([cos, cos], axis=1)
    sin_h = jnp.concatenate([-sin, sin], axis=1)
    return jnp.tile(cos_h, (1, LANES // HEAD_DIM)), jnp.tile(sin_h, (1, LANES // HEAD_DIM))


def _column_layout(d_model, d_hy):
    d_gqa = N_Q_HEADS * HEAD_DIM
    d_kv = N_KV_HEADS * HEAD_DIM
    d_diff = N_DIFF_HEADS * 2 * HEAD_DIM
    widths = [3 * d_hy, d_gqa, d_kv, d_kv, d_diff, d_diff, d_diff, 3 * d_model]
    offs = [0]
    for w in widths:
        offs.append(offs[-1] + w)
    deint = np.concatenate([np.arange(0, HEAD_DIM, 2), np.arange(1, HEAD_DIM, 2)])
    q_heads = [kv * GQA_GROUP + g for g in range(GQA_GROUP) for kv in range(N_KV_HEADS)]
    q_perm = np.concatenate([h * HEAD_DIM + deint for h in q_heads])
    k_perm = np.concatenate([h * HEAD_DIM + deint for h in range(N_KV_HEADS)])
    perm = np.arange(offs[-1])
    perm[offs[1]:offs[2]] = offs[1] + q_perm
    perm[offs[2]:offs[3]] = offs[2] + k_perm
    out_rows = np.concatenate([h * HEAD_DIM + np.arange(HEAD_DIM) for h in q_heads])
    return offs, perm, deint, q_heads, out_rows


def kernel(x_prompt, x_sample, ffn1_norm, ffn1_w_in, ffn1_w_out, mix_norm, w_in, hy_conv_w, hy_conv_b, hy_filt_w1, hy_filt_b1, hy_filt_w2, hy_filt_b2, hy_filt_w3, hy_filt_freq, hy_skip, gqa_q_norm, gqa_k_norm, diff_lambda, diff_subln, rel_bias, w_branch, w_out, ffn2_norm, ffn2_w_in, ffn2_w_out, final_norm):
    b1, l1, d = x_prompt.shape
    b2, l2, _ = x_sample.shape
    depth = w_in.shape[0]
    d_hy = hy_skip.shape[1]
    bl1 = b1 * l1
    groups = ((bl1, l1), (b2 * l2, l2))
    assert bl1 % l2 == 0 and l1 % GRID_W == 0 and l2 % GRID_W == 0

    x = jnp.concatenate([x_prompt.reshape(bl1, d), x_sample.reshape(b2 * l2, d)], axis=0)

    offs, perm, deint, q_heads, out_rows = _column_layout(d, d_hy)
    cos_t, sin_t = _rope_tables(max(l1, l2))
    bd = jnp.asarray(np.kron(np.eye(LANES // HEAD_DIM), np.full((HEAD_DIM, HEAD_DIM), 1.0 / HEAD_DIM)), BF16)
    tabs = {l: _dft_tables(l) for l in sorted({l1, l2})}
    seq_groups = [(b1, l1, 0), (b2, l2, bl1)]
    diff_tiles = {l: (min(DIFF_TQ, l), min(DIFF_TK, l)) for l in (l1, l2)}
    bias = {tt: _bias_tiles(rel_bias, *tt) for tt in sorted(set(diff_tiles.values()))}

    for l in range(depth):
        x = _ffn(x, ffn1_norm[l], ffn1_w_in[l].astype(BF16), ffn1_w_out[l].astype(BF16))

        w_l = w_in[l][:, perm].astype(BF16)
        qn = jnp.tile(gqa_q_norm[l][deint], N_Q_HEADS).reshape(1, -1)
        kn = jnp.tile(gqa_k_norm[l][deint], N_KV_HEADS).reshape(1, -1)
        hy, gq, gk, gv, dq, dk, dv, gates = _inproj(x, mix_norm[l], w_l, cos_t, sin_t, bd, qn, kn,
                                                     offs, groups)

        z, x0 = _shortconv(hy, hy_conv_w[l], hy_conv_b[l], groups)
        lp = diff_lambda[l].astype(F32)
        lam_init = 0.8 - 0.6 * math.exp(-0.3 * l)
        lam = (jnp.exp(jnp.sum(lp[0] * lp[1])) - jnp.exp(jnp.sum(lp[2] * lp[3])) + lam_init).reshape(1)

        y_hy, y_gqa, y_diff = [], [], []
        for nseq, sl, off in seq_groups:
            rows = slice(off, off + nseq * sl)
            kf = _hyena_kf(sl, tabs[sl], hy_filt_w1[l], hy_filt_b1[l], hy_filt_w2[l], hy_filt_b2[l],
                           hy_filt_w3[l], hy_filt_freq[l])
            y_hy.append(_hyena_longconv(z[rows], x0[rows], kf, hy_skip[l], nseq, sl, tabs[sl]))
            y_gqa.append(_gqa(gq, gk, gv, nseq, sl, off))
            tq, tk = diff_tiles[sl]
            y_diff.append(_diff(lam, dq, dk, dv, bias[(tq, tk)], diff_subln[l], nseq, sl, off,
                                lam_init, tq, tk))
        y_hy = jnp.concatenate(y_hy, axis=0)
        y_gqa = jnp.concatenate(y_gqa, axis=0)
        y_diff = jnp.concatenate(y_diff, axis=0)

        wb = w_branch[l].at[1].set(w_branch[l][1][out_rows]).astype(BF16)
        x = _merge(x, y_hy, y_gqa, y_diff, gates, wb, w_out[l].astype(BF16))
        x = _ffn(x, ffn2_norm[l], ffn2_w_in[l].astype(BF16), ffn2_w_out[l].astype(BF16),
                 final_g=final_norm if l == depth - 1 else None)

    return x[:bl1].reshape(b1, l1, d), x[bl1:].reshape(b2, l2, d)
```

```python
import functools
import math

import numpy as np
import jax
import jax.numpy as jnp
from jax import lax
from jax.experimental import pallas as pl
from jax.experimental.pallas import tpu as pltpu

F32 = jnp.float32
BF16 = jnp.bfloat16

NORM_EPS = 1e-6
GRID_W = 64
HEAD_DIM = 64
N_Q_HEADS = 8
N_KV_HEADS = 2
GQA_GROUP = N_Q_HEADS // N_KV_HEADS
ROPE_THETA = 10000.0
N_DIFF_HEADS = 4
DIFF_SUBLN_EPS = 1e-5
N_BUCKETS = 32
MAX_DISTANCE = 128
POS_BANDS = 16
DECAY_TARGET = 1e-2
FAST_DECAY_PCT = 0.3
SLOW_DECAY_PCT = 1.5

LANES = 128
VMEM_LIMIT_BYTES = 56 * 1024 * 1024

FFN_TM = 512
PROJ_TM = 256
MERGE_TM = 512
CONV_TM = 512
GQA_TQ = 128
GQA_TK = 1024
DIFF_TQ = 512
DIFF_TK = 1024
FFT_CC = 8
FILT_TM = 512

HIGHEST = lax.Precision.HIGHEST
NEG_BIG = -1e30
LOG2E = math.log2(math.e)


def _params(n_axes):
    return pltpu.CompilerParams(dimension_semantics=("arbitrary",) * n_axes,
                                vmem_limit_bytes=VMEM_LIMIT_BYTES)


def _dot(a, b):
    return jnp.dot(a, b, preferred_element_type=F32)


def _dot_hp(a, b):
    return jnp.dot(a, b, precision=HIGHEST, preferred_element_type=F32)


def _split_bf16(x):
    hi = x.astype(BF16)
    return hi, (x - hi.astype(F32)).astype(BF16)


def _dot3(a, b):
    ah, al = _split_bf16(a)
    bh, bl = _split_bf16(b)
    return _dot(ah, bh) + (_dot(ah, bl) + _dot(al, bh))


def _rms(x, g, eps):
    ms = jnp.mean(x * x, axis=-1, keepdims=True)
    return x * lax.rsqrt(ms + eps) * g


def _ffn_body(*refs, n_ff, final):
    if final:
        x_ref, g_ref, wg_ref, wu_ref, wo_ref, gf_ref, o_ref, h_scr, acc_scr = refs
    else:
        x_ref, g_ref, wg_ref, wu_ref, wo_ref, o_ref, h_scr, acc_scr = refs
    j = pl.program_id(1)

    @pl.when(j == 0)
    def _():
        h_scr[...] = _rms(x_ref[...], g_ref[...], NORM_EPS).astype(BF16)
        acc_scr[...] = jnp.zeros_like(acc_scr)

    h = h_scr[...]
    gate = _dot(h, wg_ref[...])
    up = _dot(h, wu_ref[...])
    a = (gate * jax.nn.sigmoid(gate) * up).astype(BF16)
    acc_scr[...] += _dot(a, wo_ref[...])

    @pl.when(j == n_ff - 1)
    def _():
        y = x_ref[...] + 0.5 * acc_scr[...]
        if final:
            y = _rms(y, gf_ref[...], NORM_EPS)
        o_ref[...] = y


def _ffn_tile(d_ff):
    for n in (2, 4, 1):
        if d_ff % n == 0 and (d_ff // n) % LANES == 0:
            return d_ff // n
    return d_ff


def _ffn(x, norm_g, w_in, w_out, final_g=None):
    t, d = x.shape
    d_ff = w_out.shape[0]
    tf = _ffn_tile(d_ff)
    n_ff = d_ff // tf
    tm = min(FFN_TM, t)
    final = final_g is not None
    in_specs = [
        pl.BlockSpec((tm, d), lambda i, j: (i, 0)),
        pl.BlockSpec((1, d), lambda i, j: (0, 0)),
        pl.BlockSpec((d, tf), lambda i, j: (0, j)),
        pl.BlockSpec((d, tf), lambda i, j: (0, j + n_ff)),
        pl.BlockSpec((tf, d), lambda i, j: (j, 0)),
    ]
    args = [x, norm_g.reshape(1, d), w_in, w_in, w_out]
    if final:
        in_specs.append(pl.BlockSpec((1, d), lambda i, j: (0, 0)))
        args.append(final_g.reshape(1, d))
    return pl.pallas_call(
        functools.partial(_ffn_body, n_ff=n_ff, final=final),
        grid=(t // tm, n_ff),
        in_specs=in_specs,
        out_specs=pl.BlockSpec((tm, d), lambda i, j: (i, 0)),
        out_shape=jax.ShapeDtypeStruct((t, d), F32),
        scratch_shapes=[pltpu.VMEM((tm, d), BF16), pltpu.VMEM((tm, d), F32)],
        compiler_params=_params(2),
        name="ffn",
    )(*args)


def _head_rms(x, bd):
    x2 = x * x
    hi = x2.astype(BF16)
    lo = (x2 - hi.astype(F32)).astype(BF16)
    ms = _dot(hi, bd) + _dot(lo, bd)
    return x * lax.rsqrt(ms + NORM_EPS)


def _rope(x, cos, sin_signed):
    lane = lax.broadcasted_iota(jnp.int32, x.shape, 1)
    partner = jnp.where((lane % HEAD_DIM) < HEAD_DIM // 2,
                        pltpu.roll(x, LANES - HEAD_DIM // 2, 1),
                        pltpu.roll(x, HEAD_DIM // 2, 1))
    return x * cos + partner * sin_signed


def _inproj_body(x_ref, g_ref, w_ref, cos_ref, sin_ref, bd_ref, qn_ref, kn_ref,
                 hy_ref, gq_ref, gk_ref, gv_ref, dq_ref, dk_ref, dv_ref, gate_ref, *, offs):
    o_hy, o_gq, o_gk, o_gv, o_dq, o_dk, o_dv, o_gate, o_end = offs
    h = _rms(x_ref[...], g_ref[...], NORM_EPS).astype(BF16)

    def seg(a, b):
        return _dot(h, w_ref[:, a:b])

    hy_ref[...] = seg(o_hy, o_gq)
    cos = cos_ref[...]
    sin = sin_ref[...]
    bd = bd_ref[...]
    scale = HEAD_DIM ** -0.5 * LOG2E
    for c in range((o_gk - o_gq) // LANES):
        a = o_gq + c * LANES
        y = _head_rms(seg(a, a + LANES), bd) * qn_ref[:, c * LANES:(c + 1) * LANES]
        gq_ref[:, c * LANES:(c + 1) * LANES] = (_rope(y, cos, sin) * scale).astype(BF16)
    for c in range((o_gv - o_gk) // LANES):
        a = o_gk + c * LANES
        y = _head_rms(seg(a, a + LANES), bd) * kn_ref[:, c * LANES:(c + 1) * LANES]
        gk_ref[:, c * LANES:(c + 1) * LANES] = _rope(y, cos, sin).astype(BF16)
    gv_ref[...] = seg(o_gv, o_dq).astype(BF16)
    dq_ref[...] = (seg(o_dq, o_dk) * scale).astype(BF16)
    dk_ref[...] = seg(o_dk, o_dv).astype(BF16)
    dv_ref[...] = seg(o_dv, o_gate).astype(BF16)
    gate_ref[...] = jax.nn.sigmoid(seg(o_gate, o_end)).astype(BF16)


def _inproj(x, norm_g, w, cos_t, sin_t, bd, qn, kn, offs, groups):
    t, d = x.shape
    tm = min(PROJ_TM, t)
    widths = [offs[i + 1] - offs[i] for i in range(8)]
    (bl, l1), (_, l2) = groups
    nbp = bl // tm

    def pos_map(i):
        return (jnp.where(i < nbp, i % (l1 // tm), (i - nbp) % (l2 // tm)), 0)

    full = lambda i: (0, 0)
    row = lambda i: (i, 0)
    out_dtypes = [F32] + [BF16] * 7
    return pl.pallas_call(
        functools.partial(_inproj_body, offs=tuple(offs)),
        grid=(t // tm,),
        in_specs=[
            pl.BlockSpec((tm, d), row),
            pl.BlockSpec((1, d), full),
            pl.BlockSpec(w.shape, full),
            pl.BlockSpec((tm, LANES), pos_map),
            pl.BlockSpec((tm, LANES), pos_map),
            pl.BlockSpec(bd.shape, full),
            pl.BlockSpec(qn.shape, full),
            pl.BlockSpec(kn.shape, full),
        ],
        out_specs=[pl.BlockSpec((tm, wd), row) for wd in widths],
        out_shape=[jax.ShapeDtypeStruct((t, wd), dt) for wd, dt in zip(widths, out_dtypes)],
        compiler_params=_params(1),
        name="inproj",
    )(x, norm_g.reshape(1, d), w, cos_t, sin_t, bd, qn, kn)


def _shortconv_body(u_ref, prev_ref, next_ref, w_ref, b_ref, z_ref, x0_ref, *, tm, groups, d_hy):
    i = pl.program_id(0)
    row0 = i * tm
    (bl, l1), (_, l2) = groups
    pos = jnp.where(row0 < bl, row0 % l1, (row0 - bl) % l2)
    seq_len = jnp.where(row0 < bl, l1, l2)
    u = u_ref[...]
    rows = lax.broadcasted_iota(jnp.int32, u.shape, 0)
    before = jnp.where(pos == 0, 0.0, prev_ref[7:8, :])
    after = jnp.where(pos + tm == seq_len, 0.0, next_ref[0:1, :])
    up = jnp.where(rows == 0, before, pltpu.roll(u, 1, 0))
    dn = jnp.where(rows == tm - 1, after, pltpu.roll(u, tm - 1, 0))
    y = up * w_ref[0:1, :] + u * w_ref[1:2, :] + dn * w_ref[2:3, :] + b_ref[...]
    x0_ref[...] = y[:, :d_hy]
    z_ref[...] = y[:, 2 * d_hy:] * y[:, d_hy:2 * d_hy]


def _shortconv(u, w, b, groups):
    t, c3 = u.shape
    d_hy = c3 // 3
    tm = min(CONV_TM, t)
    nb8 = t // 8
    return pl.pallas_call(
        functools.partial(_shortconv_body, tm=tm, groups=groups, d_hy=d_hy),
        grid=(t // tm,),
        in_specs=[
            pl.BlockSpec((tm, c3), lambda i: (i, 0)),
            pl.BlockSpec((8, c3), lambda i: (jnp.maximum(i * (tm // 8) - 1, 0), 0)),
            pl.BlockSpec((8, c3), lambda i: (jnp.minimum((i + 1) * (tm // 8), nb8 - 1), 0)),
            pl.BlockSpec((3, c3), lambda i: (0, 0)),
            pl.BlockSpec((1, c3), lambda i: (0, 0)),
        ],
        out_specs=[pl.BlockSpec((tm, d_hy), lambda i: (i, 0))] * 2,
        out_shape=[jax.ShapeDtypeStruct((t, d_hy), F32)] * 2,
        compiler_params=_params(1),
        name="shortconv",
    )(u, u, u, w, b.reshape(1, c3))


def _filter_body(f_ref, w1_ref, b1_ref, w2_ref, b2_ref, w3_ref, fr_ref, dl_ref, o_ref, *, n_feat):
    f = f_ref[...]
    fr = fr_ref[...]
    h = jnp.sin(fr * (_dot_hp(f, w1_ref[...]) + b1_ref[...]))
    h = jnp.sin(fr * (_dot_hp(h, w2_ref[...]) + b2_ref[...]))
    h = _dot_hp(h, w3_ref[...])
    tpos = f[:, 0:1]
    valid = f[:, n_feat:n_feat + 1]
    o_ref[...] = h * jnp.exp(-tpos * dl_ref[...]) * valid


def _hyena_filter(feats, w1p, b1, w2, b2, w3, freq, deltas_abs, seq_len, n_feat):
    n2l, fw = feats.shape
    d_hy = w3.shape[1] // 2
    hid = w2.shape[0]
    tm = min(FILT_TM, seq_len)
    nbl = seq_len // tm
    full = lambda i: (0, 0)
    return pl.pallas_call(
        functools.partial(_filter_body, n_feat=n_feat),
        grid=(n2l // tm,),
        in_specs=[
            pl.BlockSpec((tm, fw), lambda i: (i, 0)),
            pl.BlockSpec(w1p.shape, full),
            pl.BlockSpec((1, hid), full),
            pl.BlockSpec(w2.shape, full),
            pl.BlockSpec((1, hid), full),
            pl.BlockSpec((hid, d_hy), lambda i: (0, i // nbl)),
            pl.BlockSpec((1, hid), full),
            pl.BlockSpec((1, d_hy), full),
        ],
        out_specs=pl.BlockSpec((tm, d_hy), lambda i: (i, 0)),
        out_shape=jax.ShapeDtypeStruct((n2l, d_hy), F32),
        compiler_params=_params(1),
        name="hyena_filter",
    )(feats, w1p, b1.reshape(1, hid), w2, b2.reshape(1, hid), w3, freq.reshape(1, hid),
      deltas_abs.reshape(1, d_hy))


def _fwd_spectrum(f1, x, twr, twi, w2f, cc):
    n1 = twr.shape[0]
    a = _dot3(f1, x)
    rows = []
    for c in range(cc):
        ar = a[:n1, c * LANES:(c + 1) * LANES]
        ai = a[n1:, c * LANES:(c + 1) * LANES]
        rows.append(jnp.concatenate([ar * twr - ai * twi, ar * twi + ai * twr], axis=1))
    xs = _dot3(jnp.concatenate(rows, axis=0), w2f)
    return xs[:, :LANES], xs[:, LANES:]


def _kf_body(f1_ref, x_ref, twr_ref, twi_ref, w2f_ref, o_ref, *, cc, scale):
    n1 = twr_ref.shape[0]
    xr, xi = _fwd_spectrum(f1_ref[...], x_ref[...], twr_ref[...], twi_ref[...], w2f_ref[...], cc)
    for c in range(cc):
        o_ref[0, :, c * LANES:(c + 1) * LANES] = xr[c * n1:(c + 1) * n1] * scale
        o_ref[1, :, c * LANES:(c + 1) * LANES] = xi[c * n1:(c + 1) * n1] * scale


def _conv_body(f1_ref, finv_ref, z_ref, x0_ref, kf_ref, skip_ref, twr_ref, twi_ref, w2f_ref, w2i_ref,
               o_ref, *, cc):
    n1 = twr_ref.shape[0]
    twr = twr_ref[...]
    twi = twi_ref[...]
    z = z_ref[0]
    xr, xi = _fwd_spectrum(f1_ref[...], z, twr, twi, w2f_ref[...], cc)
    kr = jnp.concatenate([kf_ref[0, :, c * LANES:(c + 1) * LANES] for c in range(cc)], axis=0)
    ki = jnp.concatenate([kf_ref[1, :, c * LANES:(c + 1) * LANES] for c in range(cc)], axis=0)
    y = jnp.concatenate([xr * kr - xi * ki, xr * ki + xi * kr], axis=1)
    cs = _dot3(y, w2i_ref[...])
    d_re, d_im = [], []
    for c in range(cc):
        cr = cs[c * n1:(c + 1) * n1, :LANES]
        ci = cs[c * n1:(c + 1) * n1, LANES:]
        d_re.append(cr * twr + ci * twi)
        d_im.append(ci * twr - cr * twi)
    d = jnp.concatenate([jnp.concatenate(d_re, axis=1), jnp.concatenate(d_im, axis=1)], axis=0)
    conv = _dot3(finv_ref[...], d)
    o_ref[0] = (x0_ref[0] * (conv + z * skip_ref[...])).astype(o_ref.dtype)


def _dft_tables(seq_len):
    n = 2 * seq_len
    n1 = n // LANES
    n1h = n1 // 2
    k1 = np.arange(n1)[:, None]
    ang1 = 2.0 * np.pi * k1 * np.arange(n1)[None, :] / n1
    f1 = np.concatenate([np.cos(ang1), -np.sin(ang1)], axis=0)
    finv = np.concatenate([np.cos(ang1[:n1h]), -np.sin(ang1[:n1h])], axis=1)
    ang2 = 2.0 * np.pi * np.arange(LANES)[:, None] * np.arange(LANES)[None, :] / LANES
    f2r, f2i = np.cos(ang2), -np.sin(ang2)
    w2f = np.block([[f2r, f2i], [-f2i, f2r]])
    w2i = np.block([[f2r, -f2i], [f2i, f2r]])
    angt = 2.0 * np.pi * k1 * np.arange(LANES)[None, :] / n
    f = lambda a: jnp.asarray(a, F32)
    return dict(n=n, n1=n1, n1h=n1h, f1_full=f(f1), f1_half=f(f1[:, :n1h]), finv=f(finv),
                w2f=f(w2f), w2i=f(w2i), twr=f(np.cos(angt)), twi=f(-np.sin(angt)))


def _to_blocked(x, nseq):
    t, c = x.shape
    r = t // (nseq * LANES)
    return x.reshape(nseq, r, LANES, c).transpose(0, 1, 3, 2).reshape(nseq, r, c * LANES)


def _from_blocked(y, c):
    nseq, r, _ = y.shape
    return y.reshape(nseq, r, c, LANES).transpose(0, 1, 3, 2).reshape(nseq * r * LANES, c)


def _filter_features(seq_len, fw):
    idx = np.arange(2 * seq_len)
    lag = np.where(idx < seq_len, idx, 2 * seq_len - idx).astype(np.int64)
    lag = np.minimum(lag, seq_len - 1)
    t = jnp.linspace(0.0, 1.0, seq_len, dtype=F32)[:, None]
    band = jnp.linspace(1e-4, POS_BANDS - 1, POS_BANDS, dtype=F32)
    ang = (2.0 * math.pi / seq_len) * jnp.arange(seq_len, dtype=F32)[:, None] * band[None, :]
    feats = jnp.concatenate([t, jnp.cos(ang), -jnp.sin(ang)], axis=-1)
    n_feat = feats.shape[1]
    valid = jnp.asarray((idx != seq_len).astype(np.float32))[:, None]
    rows = jnp.concatenate([feats[lag], valid], axis=1)
    return jnp.pad(rows, ((0, 0), (0, fw - n_feat - 1))), n_feat


def _hyena_kf(seq_len, tabs, w1, b1, w2, b2, w3, freq):
    d_hy = w3.shape[1] // 2
    feats, n_feat = _filter_features(seq_len, LANES)
    w1p = jnp.pad(w1, ((0, LANES - w1.shape[0]), (0, 0)))
    max_decay = math.log(DECAY_TARGET) / FAST_DECAY_PCT
    min_decay = math.log(DECAY_TARGET) / SLOW_DECAY_PCT
    deltas = jnp.abs(jnp.linspace(min_decay, max_decay, d_hy, dtype=F32))
    kern = _hyena_filter(feats, w1p, b1, w2, b2, w3, freq, deltas, seq_len, n_feat)
    n, n1 = tabs["n"], tabs["n1"]
    cc = FFT_CC
    w = cc * LANES
    full = lambda j: (0, 0)
    return pl.pallas_call(
        functools.partial(_kf_body, cc=cc, scale=1.0 / n),
        grid=(d_hy // cc,),
        in_specs=[pl.BlockSpec((2 * n1, n1), full),
                  pl.BlockSpec((n1, w), lambda j: (0, j)),
                  pl.BlockSpec((n1, LANES), full), pl.BlockSpec((n1, LANES), full),
                  pl.BlockSpec((2 * LANES, 2 * LANES), full)],
        out_specs=pl.BlockSpec((2, n1, w), lambda j: (0, 0, j)),
        out_shape=jax.ShapeDtypeStruct((2, n1, d_hy * LANES), F32),
        compiler_params=_params(1),
        name="hyena_kf",
    )(tabs["f1_full"], _to_blocked(kern, 1)[0], tabs["twr"], tabs["twi"], tabs["w2f"])


def _hyena_longconv(z, x0, kf, skip, nseq, seq_len, tabs):
    c = z.shape[1]
    n1, n1h = tabs["n1"], tabs["n1h"]
    cc = FFT_CC
    w = cc * LANES
    full = lambda b, j: (0, 0)
    rowblk = pl.BlockSpec((1, n1h, w), lambda b, j: (b, 0, j))
    y = pl.pallas_call(
        functools.partial(_conv_body, cc=cc),
        grid=(nseq, c // cc),
        in_specs=[pl.BlockSpec((2 * n1, n1h), full),
                  pl.BlockSpec((n1h, 2 * n1), full),
                  rowblk, rowblk,
                  pl.BlockSpec((2, n1, w), lambda b, j: (0, 0, j)),
                  pl.BlockSpec((1, w), lambda b, j: (0, j)),
                  pl.BlockSpec((n1, LANES), full), pl.BlockSpec((n1, LANES), full),
                  pl.BlockSpec((2 * LANES, 2 * LANES), full), pl.BlockSpec((2 * LANES, 2 * LANES), full)],
        out_specs=rowblk,
        out_shape=jax.ShapeDtypeStruct((nseq, n1h, c * LANES), BF16),
        compiler_params=_params(2),
        name="hyena_conv",
    )(tabs["f1_half"], tabs["finv"], _to_blocked(z, nseq), _to_blocked(x0, nseq), kf,
      jnp.repeat(skip.astype(F32), LANES).reshape(1, c * LANES),
      tabs["twr"], tabs["twi"], tabs["w2f"], tabs["w2i"])
    return _from_blocked(y, c)


def _stack_halves(q_cols, lhs_scr, tq):
    lane = lax.broadcasted_iota(jnp.int32, (tq, LANES), 1)
    low = lane < HEAD_DIM
    zero = jnp.zeros((tq, LANES), lhs_scr.dtype)
    for c, q in enumerate(q_cols):
        lhs_scr[(2 * c) * tq:(2 * c + 1) * tq, :] = jnp.where(low, q, zero)
        lhs_scr[(2 * c + 1) * tq:(2 * c + 2) * tq, :] = jnp.where(low, zero, q)


def _consume(s, v, m_scr, acc_scr):
    tk = s.shape[1]
    m_prev = m_scr[...]
    m_new = jnp.maximum(m_prev, jnp.max(s, axis=1, keepdims=True))
    p = jnp.exp2(s - jnp.concatenate([m_new] * (tk // LANES), axis=1)).astype(BF16)
    alpha = jnp.exp2(m_prev - m_new)
    v_ext = jnp.concatenate([v, jnp.ones_like(v)], axis=1)
    acc_scr[...] = jnp.concatenate([alpha, alpha], axis=1) * acc_scr[...] + _dot(p, v_ext)
    m_scr[...] = m_new


def _init_softmax(m_scr, acc_scr):
    m_scr[...] = jnp.full_like(m_scr, NEG_BIG)
    acc_scr[...] = jnp.zeros_like(acc_scr)


def _qk(lhs, k):
    return lax.dot_general(lhs, k, (((1,), (1,)), ((), ())), preferred_element_type=F32)


def _key_rows(t, tk):
    return pl.ds(t * tk if isinstance(t, int) else pl.multiple_of(t * tk, tk), tk)


def _pipelined_kv_loop(n_kv, score_fn, consume_fn, s0_scr, s1_scr):
    s0_scr[...] = score_fn(0)
    if n_kv == 1:
        consume_fn(s0_scr[...], 0)
        return
    assert n_kv % 2 == 0

    def pair(i, carry):
        t = 2 * i
        s1_scr[...] = score_fn(t + 1)
        consume_fn(s0_scr[...], t)
        s0_scr[...] = score_fn(t + 2)
        consume_fn(s1_scr[...], t + 1)
        return carry

    lax.fori_loop(0, n_kv // 2 - 1, pair, 0)
    s1_scr[...] = score_fn(n_kv - 1)
    consume_fn(s0_scr[...], n_kv - 2)
    consume_fn(s1_scr[...], n_kv - 1)


def _attn_scratch(m, tk):
    return [pltpu.VMEM((m, LANES), BF16), pltpu.VMEM((m, LANES), F32), pltpu.VMEM((m, 2 * LANES), F32),
            pltpu.VMEM((m, tk), F32), pltpu.VMEM((m, tk), F32)]


def _gqa_body(q_ref, k_ref, v_ref, o_ref, lhs_scr, m_scr, acc_scr, s0_scr, s1_scr, *, tq, tk, n_kv):
    n_col = q_ref.shape[1] // LANES
    _stack_halves([q_ref[:, c * LANES:(c + 1) * LANES] for c in range(n_col)], lhs_scr, tq)
    _init_softmax(m_scr, acc_scr)

    def score(t):
        return _qk(lhs_scr[...], k_ref[_key_rows(t, tk), :])

    def consume(s, t):
        _consume(s, v_ref[_key_rows(t, tk), :], m_scr, acc_scr)

    _pipelined_kv_loop(n_kv, score, consume, s0_scr, s1_scr)
    o = acc_scr[:, :LANES] / acc_scr[:, LANES:]
    low = lax.broadcasted_iota(jnp.int32, (tq, LANES), 1) < HEAD_DIM
    for c in range(n_col):
        o_ref[:, c * LANES:(c + 1) * LANES] = jnp.where(
            low, o[(2 * c) * tq:(2 * c + 1) * tq], o[(2 * c + 1) * tq:(2 * c + 2) * tq]
        ).astype(o_ref.dtype)


def _gqa(q, k, v, nseq, seq_len, row_off):
    dq = q.shape[1]
    tq = min(GQA_TQ, seq_len)
    tk = min(GQA_TK, seq_len)
    nq = seq_len // tq
    m = 2 * (dq // LANES) * tq
    qoff = row_off // tq
    soff = row_off // seq_len
    return pl.pallas_call(
        functools.partial(_gqa_body, tq=tq, tk=tk, n_kv=seq_len // tk),
        grid=(nseq, nq),
        in_specs=[pl.BlockSpec((tq, dq), lambda b, i: (qoff + b * nq + i, 0)),
                  pl.BlockSpec((seq_len, LANES), lambda b, i: (soff + b, 0)),
                  pl.BlockSpec((seq_len, LANES), lambda b, i: (soff + b, 0))],
        out_specs=pl.BlockSpec((tq, dq), lambda b, i: (b * nq + i, 0)),
        out_shape=jax.ShapeDtypeStruct((nseq * seq_len, dq), BF16),
        scratch_shapes=_attn_scratch(m, tk),
        compiler_params=_params(2),
        name="gqa_attn",
    )(q, k, v)


def _bias_span(tq, tk):
    u_lo = -((MAX_DISTANCE + tk - 1 + tq - 1) // tq)
    u_hi = (MAX_DISTANCE + tq - 1 + tq - 1) // tq
    return u_lo, u_hi


def _t5_bucket(rel):
    nb = N_BUCKETS // 2
    max_exact = nb // 2
    ret = jnp.where(rel > 0, nb, 0)
    n = jnp.abs(rel)
    nf = jnp.maximum(n, 1).astype(F32)
    large = max_exact + (jnp.log(nf / max_exact) / math.log(MAX_DISTANCE / max_exact)
                         * (nb - max_exact)).astype(jnp.int32)
    large = jnp.minimum(large, nb - 1)
    return ret + jnp.where(n < max_exact, n, large)


def _bias_body(tab_ref, o_ref, *, tq, tk, u_lo):
    h = pl.program_id(0)
    u = pl.program_id(1) + u_lo
    rel = (u * tq + lax.broadcasted_iota(jnp.int32, (tq, tk), 1)
           - lax.broadcasted_iota(jnp.int32, (tq, tk), 0))
    bucket = _t5_bucket(rel)
    bias = jnp.zeros((tq, tk), F32)
    for b in range(N_BUCKETS):
        bias = jnp.where(bucket == b, tab_ref[b, h], bias)
    o_ref[0, 0] = bias * LOG2E


def _bias_tiles(rel_bias, tq, tk):
    u_lo, u_hi = _bias_span(tq, tk)
    n_off = u_hi - u_lo + 1
    n_heads = rel_bias.shape[1]
    return pl.pallas_call(
        functools.partial(_bias_body, tq=tq, tk=tk, u_lo=u_lo),
        grid=(n_heads, n_off),
        in_specs=[pl.BlockSpec(memory_space=pltpu.SMEM)],
        out_specs=pl.BlockSpec((1, 1, tq, tk), lambda h, u: (h, u, 0, 0)),
        out_shape=jax.ShapeDtypeStruct((n_heads, n_off, tq, tk), F32),
        compiler_params=_params(2),
        name="t5_bias",
    )(rel_bias.astype(F32))


def _diff_body(lam_ref, q_ref, k_ref, v_ref, bias_ref, g_ref, o_ref,
               lhs_scr, m_scr, acc_scr, s0_scr, s1_scr, *, tq, tk, n_kv, u_lo, u_hi, out_scale):
    qi = pl.program_id(2)
    _stack_halves([q_ref[...]], lhs_scr, tq)
    _init_softmax(m_scr, acc_scr)

    def score(t):
        s = _qk(lhs_scr[...], k_ref[_key_rows(t, tk), :])
        u = jnp.clip(t * (tk // tq) - qi, u_lo, u_hi) - u_lo
        return (s.reshape(2, tq, tk) + bias_ref[0, u][None]).reshape(2 * tq, tk)

    def consume(s, t):
        _consume(s, v_ref[_key_rows(t, tk), :], m_scr, acc_scr)

    _pipelined_kv_loop(n_kv, score, consume, s0_scr, s1_scr)
    o = acc_scr[:, :LANES] / acc_scr[:, LANES:]
    o = o[:tq] - lam_ref[0] * o[tq:]
    o_ref[...] = (_rms(o, g_ref[...], DIFF_SUBLN_EPS) * out_scale).astype(o_ref.dtype)


def _diff(lam, q, k, v, bias, subln_g, nseq, seq_len, row_off, lam_init, tq, tk):
    n_heads = q.shape[1] // LANES
    nq = seq_len // tq
    qoff = row_off // tq
    soff = row_off // seq_len
    u_lo, u_hi = _bias_span(tq, tk)
    n_off = u_hi - u_lo + 1
    kv_spec = pl.BlockSpec((seq_len, LANES), lambda b, h, i: (soff + b, h))
    return pl.pallas_call(
        functools.partial(_diff_body, tq=tq, tk=tk, n_kv=seq_len // tk, u_lo=u_lo, u_hi=u_hi,
                          out_scale=1.0 - lam_init),
        grid=(nseq, n_heads, nq),
        in_specs=[pl.BlockSpec(memory_space=pltpu.SMEM),
                  pl.BlockSpec((tq, LANES), lambda b, h, i: (qoff + b * nq + i, h)),
                  kv_spec, kv_spec,
                  pl.BlockSpec((1, n_off, tq, tk), lambda b, h, i: (h, 0, 0, 0),
                               pipeline_mode=pl.Buffered(1)),
                  pl.BlockSpec((1, LANES), lambda b, h, i: (0, 0))],
        out_specs=pl.BlockSpec((tq, LANES), lambda b, h, i: (b * nq + i, h)),
        out_shape=jax.ShapeDtypeStruct((nseq * seq_len, q.shape[1]), BF16),
        scratch_shapes=_attn_scratch(2 * tq, tk),
        compiler_params=_params(3),
        name="diff_attn",
    )(lam, q, k, v, bias, subln_g.reshape(1, LANES))


def _merge_body(x_ref, yh_ref, yg_ref, yd_ref, gate_ref, wb_ref, wo_ref, o_ref):
    d = x_ref.shape[1]
    merged = (gate_ref[:, 0:d].astype(F32) * _dot(yh_ref[...], wb_ref[0])
              + gate_ref[:, d:2 * d].astype(F32) * _dot(yg_ref[...], wb_ref[1])
              + gate_ref[:, 2 * d:3 * d].astype(F32) * _dot(yd_ref[...], wb_ref[2]))
    o_ref[...] = x_ref[...] + _dot(merged.astype(BF16), wo_ref[...])


def _merge(x, y_hy, y_gqa, y_diff, gates, wb, wo):
    t, d = x.shape
    db = y_hy.shape[1]
    tm = min(MERGE_TM, t)
    row = lambda i: (i, 0)
    return pl.pallas_call(
        _merge_body,
        grid=(t // tm,),
        in_specs=[pl.BlockSpec((tm, d), row),
                  pl.BlockSpec((tm, db), row), pl.BlockSpec((tm, db), row), pl.BlockSpec((tm, db), row),
                  pl.BlockSpec((tm, 3 * d), row),
                  pl.BlockSpec(wb.shape, lambda i: (0, 0, 0)),
                  pl.BlockSpec(wo.shape, lambda i: (0, 0))],
        out_specs=pl.BlockSpec((tm, d), row),
        out_shape=jax.ShapeDtypeStruct((t, d), F32),
        compiler_params=_params(1),
        name="merge",
    )(x, y_hy, y_gqa, y_diff, gates, wb, wo)


def _rope_tables(max_len):
    pos = np.arange(max_len)
    half = HEAD_DIM // 2
    inv = ROPE_THETA ** (-jnp.arange(0, half, 2, dtype=F32) / half)
    row = jnp.asarray(pos // GRID_W, F32)
    col = jnp.asarray(pos % GRID_W, F32)
    ang = jnp.concatenate([row[:, None] * inv, col[:, None] * inv], axis=-1)
    cos, sin = jnp.cos(ang), jnp.sin(ang)
    cos_h = jnp.concatenate([cos, cos], axis=1)
    sin_h = jnp.concatenate([-sin, sin], axis=1)
    return jnp.tile(cos_h, (1, LANES // HEAD_DIM)), jnp.tile(sin_h, (1, LANES // HEAD_DIM))


def _column_layout(d_model, d_hy):
    d_gqa = N_Q_HEADS * HEAD_DIM
    d_kv = N_KV_HEADS * HEAD_DIM
    d_diff = N_DIFF_HEADS * 2 * HEAD_DIM
    widths = [3 * d_hy, d_gqa, d_kv, d_kv, d_diff, d_diff, d_diff, 3 * d_model]
    offs = [0]
    for w in widths:
        offs.append(offs[-1] + w)
    deint = np.concatenate([np.arange(0, HEAD_DIM, 2), np.arange(1, HEAD_DIM, 2)])
    q_heads = [kv * GQA_GROUP + g for g in range(GQA_GROUP) for kv in range(N_KV_HEADS)]
    q_perm = np.concatenate([h * HEAD_DIM + deint for h in q_heads])
    k_perm = np.concatenate([h * HEAD_DIM + deint for h in range(N_KV_HEADS)])
    perm = np.arange(offs[-1])
    perm[offs[1]:offs[2]] = offs[1] + q_perm
    perm[offs[2]:offs[3]] = offs[2] + k_perm
    out_rows = np.concatenate([h * HEAD_DIM + np.arange(HEAD_DIM) for h in q_heads])
    return offs, perm, deint, q_heads, out_rows


def kernel(x_prompt, x_sample, ffn1_norm, ffn1_w_in, ffn1_w_out, mix_norm, w_in, hy_conv_w, hy_conv_b, hy_filt_w1, hy_filt_b1, hy_filt_w2, hy_filt_b2, hy_filt_w3, hy_filt_freq, hy_skip, gqa_q_norm, gqa_k_norm, diff_lambda, diff_subln, rel_bias, w_branch, w_out, ffn2_norm, ffn2_w_in, ffn2_w_out, final_norm):
    b1, l1, d = x_prompt.shape
    b2, l2, _ = x_sample.shape
    depth = w_in.shape[0]
    d_hy = hy_skip.shape[1]
    bl1 = b1 * l1
    groups = ((bl1, l1), (b2 * l2, l2))
    assert bl1 % l2 == 0 and l1 % GRID_W == 0 and l2 % GRID_W == 0

    x = jnp.concatenate([x_prompt.reshape(bl1, d), x_sample.reshape(b2 * l2, d)], axis=0)

    offs, perm, deint, q_heads, out_rows = _column_layout(d, d_hy)
    cos_t, sin_t = _rope_tables(max(l1, l2))
    bd = jnp.asarray(np.kron(np.eye(LANES // HEAD_DIM), np.full((HEAD_DIM, HEAD_DIM), 1.0 / HEAD_DIM)), BF16)
    tabs = {l: _dft_tables(l) for l in sorted({l1, l2})}
    seq_groups = [(b1, l1, 0), (b2, l2, bl1)]
    diff_tiles = {l: (min(DIFF_TQ, l), min(DIFF_TK, l)) for l in (l1, l2)}
    bias = {tt: _bias_tiles(rel_bias, *tt) for tt in sorted(set(diff_tiles.values()))}

    for l in range(depth):
        x = _ffn(x, ffn1_norm[l], ffn1_w_in[l].astype(BF16), ffn1_w_out[l].astype(BF16))

        w_l = w_in[l][:, perm].astype(BF16)
        qn = jnp.tile(gqa_q_norm[l][deint], N_Q_HEADS).reshape(1, -1)
        kn = jnp.tile(gqa_k_norm[l][deint], N_KV_HEADS).reshape(1, -1)
        hy, gq, gk, gv, dq, dk, dv, gates = _inproj(x, mix_norm[l], w_l, cos_t, sin_t, bd, qn, kn,
                                                     offs, groups)

        z, x0 = _shortconv(hy, hy_conv_w[l], hy_conv_b[l], groups)
        lp = diff_lambda[l].astype(F32)
        lam_init = 0.8 - 0.6 * math.exp(-0.3 * l)
        lam = (jnp.exp(jnp.sum(lp[0] * lp[1])) - jnp.exp(jnp.sum(lp[2] * lp[3])) + lam_init).reshape(1)

        y_hy, y_gqa, y_diff = [], [], []
        for nseq, sl, off in seq_groups:
            rows = slice(off, off + nseq * sl)
            kf = _hyena_kf(sl, tabs[sl], hy_filt_w1[l], hy_filt_b1[l], hy_filt_w2[l], hy_filt_b2[l],
                           hy_filt_w3[l], hy_filt_freq[l])
            y_hy.append(_hyena_longconv(z[rows], x0[rows], kf, hy_skip[l], nseq, sl, tabs[sl]))
            y_gqa.append(_gqa(gq, gk, gv, nseq, sl, off))
            tq, tk = diff_tiles[sl]
            y_diff.append(_diff(lam, dq, dk, dv, bias[(tq, tk)], diff_subln[l], nseq, sl, off,
                                lam_init, tq, tk))
        y_hy = jnp.concatenate(y_hy, axis=0)
        y_gqa = jnp.concatenate(y_gqa, axis=0)
        y_diff = jnp.concatenate(y_diff, axis=0)

        wb = w_branch[l].at[1].set(w_branch[l][1][out_rows]).astype(BF16)
        x = _merge(x, y_hy, y_gqa, y_diff, gates, wb, w_out[l].astype(BF16))
        x = _ffn(x, ffn2_norm[l], ffn2_w_in[l].astype(BF16), ffn2_w_out[l].astype(BF16),
                 final_g=final_norm if l == depth - 1 else None)

    return x[:bl1].reshape(b1, l1, d), x[bl1:].reshape(b2, l2, d)
```

```python
import functools
import math

import numpy as np
import jax
import jax.numpy as jnp
from jax import lax
from jax.experimental import pallas as pl
from jax.experimental.pallas import tpu as pltpu

F32 = jnp.float32
BF16 = jnp.bfloat16

NORM_EPS = 1e-6
GRID_W = 64
HEAD_DIM = 64
N_Q_HEADS = 8
N_KV_HEADS = 2
GQA_GROUP = N_Q_HEADS // N_KV_HEADS
ROPE_THETA = 10000.0
N_DIFF_HEADS = 4
DIFF_SUBLN_EPS = 1e-5
N_BUCKETS = 32
MAX_DISTANCE = 128
POS_BANDS = 16
DECAY_TARGET = 1e-2
FAST_DECAY_PCT = 0.3
SLOW_DECAY_PCT = 1.5

LANES = 128
MXU_WIDTH = 256
VMEM_LIMIT_BYTES = 56 * 1024 * 1024

FFN_TM = 512
PROJ_TM = 512
MERGE_TM = 512
CONV_TM = 512
GQA_TQ = 128
GQA_TK = 1024
DIFF_TQ = 512
DIFF_TK = 1024
FFT_CC = 8
FILT_TM = 512

NEG_BIG = -1e30
LOG2E = math.log2(math.e)
KV_TILES_PER_ITER = 8


def _params(n_axes):
    return pltpu.CompilerParams(dimension_semantics=("arbitrary",) * n_axes,
                                vmem_limit_bytes=VMEM_LIMIT_BYTES)


def _dot(a, b):
    return jnp.dot(a, b, preferred_element_type=F32)


def _split_bf16(x):
    hi = x.astype(BF16)
    return hi, (x - hi.astype(F32)).astype(BF16)


def _dot3(a, b):
    ah, al = _split_bf16(a)
    bh, bl = _split_bf16(b)
    return _dot(ah, bh) + (_dot(ah, bl) + _dot(al, bh))


def _rms(x, g, eps):
    ms = jnp.mean(x * x, axis=-1, keepdims=True)
    return x * lax.rsqrt(ms + eps) * g


def _ffn_body(*refs, chunks, final):
    if final:
        x_ref, g_ref, w_in_ref, w_out_ref, gf_ref, o_ref = refs
    else:
        x_ref, g_ref, w_in_ref, w_out_ref, o_ref = refs
    d_ff = w_out_ref.shape[0]
    x = x_ref[...]
    h = _rms(x, g_ref[...], NORM_EPS).astype(BF16)
    acc = None
    for a, b in chunks:
        gate = _dot(h, w_in_ref[:, a:b])
        up = _dot(h, w_in_ref[:, d_ff + a:d_ff + b])
        part = _dot((gate * jax.nn.sigmoid(gate) * up).astype(BF16), w_out_ref[a:b, :])
        acc = part if acc is None else acc + part
    y = x + 0.5 * acc
    if final:
        y = _rms(y, gf_ref[...], NORM_EPS)
    o_ref[...] = y


def _ffn_chunks(d_ff):
    n_blk = d_ff // MXU_WIDTH
    if d_ff % MXU_WIDTH or n_blk < 2:
        return ((0, d_ff),)
    cut = (n_blk + 1) // 2 * MXU_WIDTH
    return ((0, cut), (cut, d_ff))


def _ffn(x, norm_g, w_in, w_out, final_g=None, row_off=0, n_rows=None):
    t, d = x.shape
    n_rows = t if n_rows is None else n_rows
    d_ff = w_out.shape[0]
    tm = min(FFN_TM, n_rows)
    off = row_off // tm
    final = final_g is not None
    const = lambda i: (0, 0)
    in_specs = [
        pl.BlockSpec((tm, d), lambda i: (off + i, 0)),
        pl.BlockSpec((1, d), const),
        pl.BlockSpec(w_in.shape, const, pipeline_mode=pl.Buffered(1)),
        pl.BlockSpec(w_out.shape, const, pipeline_mode=pl.Buffered(1)),
    ]
    args = [x, norm_g.reshape(1, d), w_in, w_out]
    if final:
        in_specs.append(pl.BlockSpec((1, d), const))
        args.append(final_g.reshape(1, d))
    return pl.pallas_call(
        functools.partial(_ffn_body, chunks=_ffn_chunks(d_ff), final=final),
        grid=(n_rows // tm,),
        in_specs=in_specs,
        out_specs=pl.BlockSpec((tm, d), lambda i: (i, 0)),
        out_shape=jax.ShapeDtypeStruct((n_rows, d), F32),
        compiler_params=_params(1),
        name="ffn",
    )(*args)


def _head_rms(x, bd):
    x2 = x * x
    hi = x2.astype(BF16)
    lo = (x2 - hi.astype(F32)).astype(BF16)
    ms = _dot(hi, bd) + _dot(lo, bd)
    return x * lax.rsqrt(ms + NORM_EPS)


def _rope(x, cos, sin_signed):
    lane = lax.broadcasted_iota(jnp.int32, x.shape, 1)
    partner = jnp.where((lane % HEAD_DIM) < HEAD_DIM // 2,
                        pltpu.roll(x, LANES - HEAD_DIM // 2, 1),
                        pltpu.roll(x, HEAD_DIM // 2, 1))
    return x * cos + partner * sin_signed


def _inproj_body(x_ref, g_ref, w_ref, wqkv_ref, cos_ref, sin_ref, bd_ref, qn_ref, kn_ref,
                 hy_ref, gq_ref, gk_ref, gv_ref, dq_ref, dk_ref, dv_ref, gate_ref, *, offs):
    o_hy, o_gq, o_gk, o_gv, o_dq, o_dk, o_dv, o_gate, o_end = offs
    h = _rms(x_ref[...], g_ref[...], NORM_EPS).astype(BF16)

    def seg(a, b):
        return _dot(h, w_ref[:, a:b])

    hy_ref[...] = seg(o_hy, o_gq)
    cos = cos_ref[...]
    sin = sin_ref[...]
    bd = bd_ref[...]
    scale = HEAD_DIM ** -0.5 * LOG2E
    qkv = _dot(h, wqkv_ref[...])
    wq, wk = o_gk - o_gq, o_gv - o_gk
    for c in range(wq // LANES):
        y = _head_rms(qkv[:, c * LANES:(c + 1) * LANES], bd) * qn_ref[:, c * LANES:(c + 1) * LANES]
        gq_ref[:, c * LANES:(c + 1) * LANES] = (_rope(y, cos, sin) * scale).astype(BF16)
    for c in range(wk // LANES):
        y = _head_rms(qkv[:, wq + c * LANES:wq + (c + 1) * LANES], bd) * kn_ref[:, c * LANES:(c + 1) * LANES]
        gk_ref[:, c * LANES:(c + 1) * LANES] = _rope(y, cos, sin).astype(BF16)
    gv_ref[...] = qkv[:, wq + wk:].astype(BF16)
    dq_ref[...] = (seg(o_dq, o_dk) * scale).astype(BF16)
    dk_ref[...] = seg(o_dk, o_dv).astype(BF16)
    dv_ref[...] = seg(o_dv, o_gate).astype(BF16)
    d = x_ref.shape[1]
    for c in range((o_end - o_gate) // d):
        gate_ref[:, c * d:(c + 1) * d] = jax.nn.sigmoid(seg(o_gate + c * d, o_gate + (c + 1) * d)).astype(BF16)


def _inproj(x, norm_g, w, wqkv, cos_t, sin_t, bd, qn, kn, offs, groups):
    t, d = x.shape
    tm = min(PROJ_TM, t)
    widths = [offs[i + 1] - offs[i] for i in range(8)]
    (bl, l1), (_, l2) = groups
    nbp = bl // tm

    def pos_map(i):
        return (jnp.where(i < nbp, i % (l1 // tm), (i - nbp) % (l2 // tm)), 0)

    full = lambda i: (0, 0)
    row = lambda i: (i, 0)
    out_dtypes = [F32] + [BF16] * 7
    return pl.pallas_call(
        functools.partial(_inproj_body, offs=tuple(offs)),
        grid=(t // tm,),
        in_specs=[
            pl.BlockSpec((tm, d), row),
            pl.BlockSpec((1, d), full),
            pl.BlockSpec(w.shape, full, pipeline_mode=pl.Buffered(1)),
            pl.BlockSpec(wqkv.shape, full, pipeline_mode=pl.Buffered(1)),
            pl.BlockSpec((tm, LANES), pos_map),
            pl.BlockSpec((tm, LANES), pos_map),
            pl.BlockSpec(bd.shape, full),
            pl.BlockSpec(qn.shape, full),
            pl.BlockSpec(kn.shape, full),
        ],
        out_specs=[pl.BlockSpec((tm, wd), row) for wd in widths],
        out_shape=[jax.ShapeDtypeStruct((t, wd), dt) for wd, dt in zip(widths, out_dtypes)],
        compiler_params=_params(1),
        name="inproj",
    )(x, norm_g.reshape(1, d), w, wqkv, cos_t, sin_t, bd, qn, kn)


def _shortconv_body(u_ref, prev_ref, next_ref, w_ref, b_ref, z_ref, x0_ref, *, tm, groups, d_hy):
    i = pl.program_id(0)
    row0 = i * tm
    (bl, l1), (_, l2) = groups
    pos = jnp.where(row0 < bl, row0 % l1, (row0 - bl) % l2)
    seq_len = jnp.where(row0 < bl, l1, l2)
    u = u_ref[...]
    rows = lax.broadcasted_iota(jnp.int32, u.shape, 0)
    before = jnp.where(pos == 0, 0.0, prev_ref[7:8, :])
    after = jnp.where(pos + tm == seq_len, 0.0, next_ref[0:1, :])
    up = jnp.where(rows == 0, before, pltpu.roll(u, 1, 0))
    dn = jnp.where(rows == tm - 1, after, pltpu.roll(u, tm - 1, 0))
    y = up * w_ref[0:1, :] + u * w_ref[1:2, :] + dn * w_ref[2:3, :] + b_ref[...]
    x0_ref[...] = y[:, :d_hy]
    z_ref[...] = y[:, 2 * d_hy:] * y[:, d_hy:2 * d_hy]


def _shortconv(u, w, b, groups):
    t, c3 = u.shape
    d_hy = c3 // 3
    tm = min(CONV_TM, t)
    nb8 = t // 8
    return pl.pallas_call(
        functools.partial(_shortconv_body, tm=tm, groups=groups, d_hy=d_hy),
        grid=(t // tm,),
        in_specs=[
            pl.BlockSpec((tm, c3), lambda i: (i, 0)),
            pl.BlockSpec((8, c3), lambda i: (jnp.maximum(i * (tm // 8) - 1, 0), 0)),
            pl.BlockSpec((8, c3), lambda i: (jnp.minimum((i + 1) * (tm // 8), nb8 - 1), 0)),
            pl.BlockSpec((3, c3), lambda i: (0, 0)),
            pl.BlockSpec((1, c3), lambda i: (0, 0)),
        ],
        out_specs=[pl.BlockSpec((tm, d_hy), lambda i: (i, 0))] * 2,
        out_shape=[jax.ShapeDtypeStruct((t, d_hy), F32)] * 2,
        compiler_params=_params(1),
        name="shortconv",
    )(u, u, u, w, b.reshape(1, c3))


def _filter_body(f_ref, w1_ref, b1_ref, w2_ref, b2_ref, w3_ref, fr_ref, dl_ref, o_ref, *, n_feat):
    f = f_ref[...]
    fr = fr_ref[...]
    h = jnp.sin(fr * (_dot3(f, w1_ref[...]) + b1_ref[...]))
    h = jnp.sin(fr * (_dot3(h, w2_ref[...]) + b2_ref[...]))
    h = _dot3(h, w3_ref[...])
    tpos = f[:, 0:1]
    valid = f[:, n_feat:n_feat + 1]
    o_ref[...] = h * jnp.exp(-tpos * dl_ref[...]) * valid


def _hyena_filter(feats, w1p, b1, w2, b2, w3, freq, deltas_abs, seq_len, n_feat):
    n2l, fw = feats.shape
    d_hy = w3.shape[1] // 2
    hid = w2.shape[0]
    tm = min(FILT_TM, seq_len)
    nbl = seq_len // tm
    full = lambda i: (0, 0)
    return pl.pallas_call(
        functools.partial(_filter_body, n_feat=n_feat),
        grid=(n2l // tm,),
        in_specs=[
            pl.BlockSpec((tm, fw), lambda i: (i, 0)),
            pl.BlockSpec(w1p.shape, full),
            pl.BlockSpec((1, hid), full),
            pl.BlockSpec(w2.shape, full),
            pl.BlockSpec((1, hid), full),
            pl.BlockSpec((hid, d_hy), lambda i: (0, i // nbl)),
            pl.BlockSpec((1, hid), full),
            pl.BlockSpec((1, d_hy), full),
        ],
        out_specs=pl.BlockSpec((tm, d_hy), lambda i: (i, 0)),
        out_shape=jax.ShapeDtypeStruct((n2l, d_hy), F32),
        compiler_params=_params(1),
        name="hyena_filter",
    )(feats, w1p, b1.reshape(1, hid), w2, b2.reshape(1, hid), w3, freq.reshape(1, hid),
      deltas_abs.reshape(1, d_hy))


def _fwd_spectrum(f1, x, twr, twi, w2f, cc):
    n1 = twr.shape[0]
    a = _dot3(f1, x)
    rows = []
    for c in range(cc):
        ar = a[:n1, c * LANES:(c + 1) * LANES]
        ai = a[n1:, c * LANES:(c + 1) * LANES]
        rows.append(jnp.concatenate([ar * twr - ai * twi, ar * twi + ai * twr], axis=1))
    xs = _dot3(jnp.concatenate(rows, axis=0), w2f)
    return xs[:, :LANES], xs[:, LANES:]


def _kf_body(f1_ref, x_ref, twr_ref, twi_ref, w2f_ref, o_ref, *, cc, scale):
    n1 = twr_ref.shape[0]
    xr, xi = _fwd_spectrum(f1_ref[...], x_ref[...], twr_ref[...], twi_ref[...], w2f_ref[...], cc)
    for c in range(cc):
        o_ref[0, :, c * LANES:(c + 1) * LANES] = xr[c * n1:(c + 1) * n1] * scale
        o_ref[1, :, c * LANES:(c + 1) * LANES] = xi[c * n1:(c + 1) * n1] * scale


def _conv_body(f1_ref, finv_ref, z_ref, x0_ref, kf_ref, skip_ref, twr_ref, twi_ref, w2f_ref, w2i_ref,
               o_ref, *, cc):
    n1 = twr_ref.shape[0]
    twr = twr_ref[...]
    twi = twi_ref[...]
    z = z_ref[0]
    xr, xi = _fwd_spectrum(f1_ref[...], z, twr, twi, w2f_ref[...], cc)
    kr = jnp.concatenate([kf_ref[0, :, c * LANES:(c + 1) * LANES] for c in range(cc)], axis=0)
    ki = jnp.concatenate([kf_ref[1, :, c * LANES:(c + 1) * LANES] for c in range(cc)], axis=0)
    y = jnp.concatenate([xr * kr - xi * ki, xr * ki + xi * kr], axis=1)
    cs = _dot3(y, w2i_ref[...])
    d_re, d_im = [], []
    for c in range(cc):
        cr = cs[c * n1:(c + 1) * n1, :LANES]
        ci = cs[c * n1:(c + 1) * n1, LANES:]
        d_re.append(cr * twr + ci * twi)
        d_im.append(ci * twr - cr * twi)
    d = jnp.concatenate([jnp.concatenate(d_re, axis=1), jnp.concatenate(d_im, axis=1)], axis=0)
    conv = _dot3(finv_ref[...], d)
    o_ref[0] = (x0_ref[0] * (conv + z * skip_ref[...])).astype(o_ref.dtype)


def _dft_tables(seq_len):
    n = 2 * seq_len
    n1 = n // LANES
    n1h = n1 // 2
    k1 = np.arange(n1)[:, None]
    ang1 = 2.0 * np.pi * k1 * np.arange(n1)[None, :] / n1
    f1 = np.concatenate([np.cos(ang1), -np.sin(ang1)], axis=0)
    finv = np.concatenate([np.cos(ang1[:n1h]), -np.sin(ang1[:n1h])], axis=1)
    ang2 = 2.0 * np.pi * np.arange(LANES)[:, None] * np.arange(LANES)[None, :] / LANES
    f2r, f2i = np.cos(ang2), -np.sin(ang2)
    w2f = np.block([[f2r, f2i], [-f2i, f2r]])
    w2i = np.block([[f2r, -f2i], [f2i, f2r]])
    angt = 2.0 * np.pi * k1 * np.arange(LANES)[None, :] / n
    f = lambda a: jnp.asarray(a, F32)
    return dict(n=n, n1=n1, n1h=n1h, f1_full=f(f1), f1_half=f(f1[:, :n1h]), finv=f(finv),
                w2f=f(w2f), w2i=f(w2i), twr=f(np.cos(angt)), twi=f(-np.sin(angt)))


def _to_blocked(x, nseq):
    t, c = x.shape
    r = t // (nseq * LANES)
    return x.reshape(nseq, r, LANES, c).transpose(0, 1, 3, 2).reshape(nseq, r, c * LANES)


def _from_blocked(y, c):
    nseq, r, _ = y.shape
    return y.reshape(nseq, r, c, LANES).transpose(0, 1, 3, 2).reshape(nseq * r * LANES, c)


def _filter_features(seq_len, fw):
    idx = np.arange(2 * seq_len)
    lag = np.where(idx < seq_len, idx, 2 * seq_len - idx).astype(np.int64)
    lag = np.minimum(lag, seq_len - 1)
    t = jnp.linspace(0.0, 1.0, seq_len, dtype=F32)[:, None]
    band = jnp.linspace(1e-4, POS_BANDS - 1, POS_BANDS, dtype=F32)
    ang = (2.0 * math.pi / seq_len) * jnp.arange(seq_len, dtype=F32)[:, None] * band[None, :]
    feats = jnp.concatenate([t, jnp.cos(ang), -jnp.sin(ang)], axis=-1)
    n_feat = feats.shape[1]
    valid = jnp.asarray((idx != seq_len).astype(np.float32))[:, None]
    rows = jnp.concatenate([feats[lag], valid], axis=1)
    return jnp.pad(rows, ((0, 0), (0, fw - n_feat - 1))), n_feat


def _hyena_kf(seq_len, tabs, w1, b1, w2, b2, w3, freq):
    d_hy = w3.shape[1] // 2
    feats, n_feat = _filter_features(seq_len, LANES)
    w1p = jnp.pad(w1, ((0, LANES - w1.shape[0]), (0, 0)))
    max_decay = math.log(DECAY_TARGET) / FAST_DECAY_PCT
    min_decay = math.log(DECAY_TARGET) / SLOW_DECAY_PCT
    deltas = jnp.abs(jnp.linspace(min_decay, max_decay, d_hy, dtype=F32))
    kern = _hyena_filter(feats, w1p, b1, w2, b2, w3, freq, deltas, seq_len, n_feat)
    n, n1 = tabs["n"], tabs["n1"]
    cc = FFT_CC
    w = cc * LANES
    full = lambda j: (0, 0)
    return pl.pallas_call(
        functools.partial(_kf_body, cc=cc, scale=1.0 / n),
        grid=(d_hy // cc,),
        in_specs=[pl.BlockSpec((2 * n1, n1), full),
                  pl.BlockSpec((n1, w), lambda j: (0, j)),
                  pl.BlockSpec((n1, LANES), full), pl.BlockSpec((n1, LANES), full),
                  pl.BlockSpec((2 * LANES, 2 * LANES), full)],
        out_specs=pl.BlockSpec((2, n1, w), lambda j: (0, 0, j)),
        out_shape=jax.ShapeDtypeStruct((2, n1, d_hy * LANES), F32),
        compiler_params=_params(1),
        name="hyena_kf",
    )(tabs["f1_full"], _to_blocked(kern, 1)[0], tabs["twr"], tabs["twi"], tabs["w2f"])


def _hyena_longconv(z, x0, kf, skip, nseq, seq_len, tabs):
    c = z.shape[1]
    n1, n1h = tabs["n1"], tabs["n1h"]
    cc = FFT_CC
    w = cc * LANES
    full = lambda b, j: (0, 0)
    rowblk = pl.BlockSpec((1, n1h, w), lambda b, j: (b, 0, j))
    y = pl.pallas_call(
        functools.partial(_conv_body, cc=cc),
        grid=(nseq, c // cc),
        in_specs=[pl.BlockSpec((2 * n1, n1h), full),
                  pl.BlockSpec((n1h, 2 * n1), full),
                  rowblk, rowblk,
                  pl.BlockSpec((2, n1, w), lambda b, j: (0, 0, j)),
                  pl.BlockSpec((1, w), lambda b, j: (0, j)),
                  pl.BlockSpec((n1, LANES), full), pl.BlockSpec((n1, LANES), full),
                  pl.BlockSpec((2 * LANES, 2 * LANES), full), pl.BlockSpec((2 * LANES, 2 * LANES), full)],
        out_specs=rowblk,
        out_shape=jax.ShapeDtypeStruct((nseq, n1h, c * LANES), BF16),
        compiler_params=_params(2),
        name="hyena_conv",
    )(tabs["f1_half"], tabs["finv"], _to_blocked(z, nseq), _to_blocked(x0, nseq), kf,
      jnp.repeat(skip.astype(F32), LANES).reshape(1, c * LANES),
      tabs["twr"], tabs["twi"], tabs["w2f"], tabs["w2i"])
    return _from_blocked(y, c)


def _stack_halves(q_cols, lhs_ref, tq):
    lane = lax.broadcasted_iota(jnp.int32, (tq, LANES), 1)
    low = lane < HEAD_DIM
    zero = jnp.zeros((tq, LANES), lhs_ref.dtype)
    for c, q in enumerate(q_cols):
        lhs_ref[(2 * c) * tq:(2 * c + 1) * tq, :] = jnp.where(low, q, zero)
        lhs_ref[(2 * c + 1) * tq:(2 * c + 2) * tq, :] = jnp.where(low, zero, q)


def _consume(s, v, m_scr, acc_scr, const=None):
    tk = s.shape[1]
    m_prev = m_scr[...]
    row_max = jnp.max(s, axis=1, keepdims=True)
    if const is not None:
        row_max = row_max + const
    m_new = jnp.maximum(m_prev, row_max)
    shift = m_new if const is None else m_new - const
    p = jnp.exp2(s - jnp.concatenate([shift] * (tk // LANES), axis=1)).astype(BF16)
    alpha = jnp.exp2(m_prev - m_new)
    v_ext = jnp.concatenate([v, jnp.ones_like(v)], axis=1)
    acc_scr[...] = jnp.concatenate([alpha, alpha], axis=1) * acc_scr[...] + _dot(p, v_ext)
    m_scr[...] = m_new


def _qk(lhs, k):
    return lax.dot_general(lhs, k, (((1,), (1,)), ((), ())), preferred_element_type=F32)


def _key_rows(t, tk):
    return pl.ds(t * tk if isinstance(t, int) else pl.multiple_of(t * tk, tk), tk)


def _pipelined_attention(qi, n_kv, n_special, stack_fn, score_fn, consume_fn, m_scr, acc_scr, s_scr):
    @pl.when(qi == 0)
    def _():
        stack_fn(False)
        s_scr[0][...] = score_fn(0, n_special == n_kv, False)

    m_scr[...] = jnp.full_like(m_scr, NEG_BIG)
    acc_scr[...] = jnp.zeros_like(acc_scr)
    unroll = min(KV_TILES_PER_ITER, n_kv)
    assert n_kv % unroll == 0 and n_special <= unroll

    def run(p0, last):
        for j in range(unroll):
            special = last and j >= unroll - n_special
            if last and j == unroll - 1:
                stack_fn(True)
                s_scr[(j + 1) % 2][...] = score_fn(0, n_special == n_kv, True)
            else:
                s_scr[(j + 1) % 2][...] = score_fn(p0 + j + 1, last and j + 1 >= unroll - n_special, False)
            consume_fn(s_scr[j % 2][...], p0 + j, special)

    def body(i, carry):
        run(i * unroll, False)
        return carry

    lax.fori_loop(0, n_kv // unroll - 1, body, 0)
    run(n_kv - unroll, True)
    if unroll % 2 == 1:
        s_scr[0][...] = s_scr[1][...]


def _attn_scratch(m, tk):
    return [pltpu.VMEM((2, m, LANES), BF16), pltpu.VMEM((m, LANES), F32), pltpu.VMEM((m, 2 * LANES), F32),
            pltpu.VMEM((m, tk), F32), pltpu.VMEM((m, tk), F32)]


def _gqa_body(q_ref, qn_ref, k_ref, v_ref, o_ref, lhs_scr, m_scr, acc_scr, s0_scr, s1_scr, *, tq, tk, n_kv):
    qi = pl.program_id(1)
    cur = qi % 2
    n_col = q_ref.shape[1] // LANES

    def stack(nxt):
        ref = qn_ref if nxt else q_ref
        slot = 1 - cur if nxt else cur
        _stack_halves([ref[:, c * LANES:(c + 1) * LANES] for c in range(n_col)], lhs_scr.at[slot], tq)

    def score(t, special, nxt):
        return _qk(lhs_scr[1 - cur if nxt else cur], k_ref[_key_rows(t, tk), :])

    def consume(s, t, special):
        _consume(s, v_ref[_key_rows(t, tk), :], m_scr, acc_scr)

    _pipelined_attention(qi, n_kv, 0, stack, score, consume, m_scr, acc_scr, (s0_scr, s1_scr))
    o = acc_scr[:, :LANES] / acc_scr[:, LANES:]
    low = lax.broadcasted_iota(jnp.int32, (tq, LANES), 1) < HEAD_DIM
    for c in range(n_col):
        o_ref[:, c * LANES:(c + 1) * LANES] = jnp.where(
            low, o[(2 * c) * tq:(2 * c + 1) * tq], o[(2 * c + 1) * tq:(2 * c + 2) * tq]
        ).astype(o_ref.dtype)


def _gqa(q, k, v, nseq, seq_len, row_off):
    dq = q.shape[1]
    tq = min(GQA_TQ, seq_len)
    tk = min(GQA_TK, seq_len)
    nq = seq_len // tq
    assert nq == 1 or nq % 2 == 0
    m = 2 * (dq // LANES) * tq
    qoff = row_off // tq
    soff = row_off // seq_len
    return pl.pallas_call(
        functools.partial(_gqa_body, tq=tq, tk=tk, n_kv=seq_len // tk),
        grid=(nseq, nq),
        in_specs=[pl.BlockSpec((tq, dq), lambda b, i: (qoff + b * nq + i, 0)),
                  pl.BlockSpec((tq, dq), lambda b, i: (qoff + b * nq + jnp.minimum(i + 1, nq - 1), 0)),
                  pl.BlockSpec((seq_len, LANES), lambda b, i: (soff + b, 0)),
                  pl.BlockSpec((seq_len, LANES), lambda b, i: (soff + b, 0))],
        out_specs=pl.BlockSpec((tq, dq), lambda b, i: (b * nq + i, 0)),
        out_shape=jax.ShapeDtypeStruct((nseq * seq_len, dq), BF16),
        scratch_shapes=_attn_scratch(m, tk),
        compiler_params=_params(2),
        name="gqa_attn",
    )(q, q, k, v)


def _bias_span(tq, tk):
    u_lo = -((MAX_DISTANCE + tk - 1 + tq - 1) // tq)
    u_hi = (MAX_DISTANCE + tq - 1 + tq - 1) // tq
    return u_lo, u_hi


def _t5_bucket(rel):
    nb = N_BUCKETS // 2
    max_exact = nb // 2
    ret = jnp.where(rel > 0, nb, 0)
    n = jnp.abs(rel)
    nf = jnp.maximum(n, 1).astype(F32)
    large = max_exact + (jnp.log(nf / max_exact) / math.log(MAX_DISTANCE / max_exact)
                         * (nb - max_exact)).astype(jnp.int32)
    large = jnp.minimum(large, nb - 1)
    return ret + jnp.where(n < max_exact, n, large)


def _bias_body(tab_ref, o_ref, *, tq, tk, u_lo):
    h = pl.program_id(0)
    u = pl.program_id(1) + u_lo
    rel = (u * tq + lax.broadcasted_iota(jnp.int32, (tq, tk), 1)
           - lax.broadcasted_iota(jnp.int32, (tq, tk), 0))
    bucket = _t5_bucket(rel)
    bias = jnp.zeros((tq, tk), F32)
    for b in range(N_BUCKETS):
        bias = jnp.where(bucket == b, tab_ref[b, h], bias)
    o_ref[0, 0] = bias * LOG2E


def _bias_tiles(rel_bias, tq, tk):
    u_lo, u_hi = _bias_span(tq, tk)
    n_off = u_hi - u_lo + 1
    n_heads = rel_bias.shape[1]
    return pl.pallas_call(
        functools.partial(_bias_body, tq=tq, tk=tk, u_lo=u_lo),
        grid=(n_heads, n_off),
        in_specs=[pl.BlockSpec(memory_space=pltpu.SMEM)],
        out_specs=pl.BlockSpec((1, 1, tq, tk), lambda h, u: (h, u, 0, 0)),
        out_shape=jax.ShapeDtypeStruct((n_heads, n_off, tq, tk), F32),
        compiler_params=_params(2),
        name="t5_bias",
    )(rel_bias.astype(F32))


def _near_count(tq, tk, u_lo, u_hi):
    r = tk // tq
    return max(sum(1 for t in range(-8, 16) if u_lo < r * t - qi < u_hi) for qi in range(r))


def _diff_body(lam_ref, far_ref, q_ref, qn_ref, k_ref, v_ref, bias_ref, g_ref, o_ref,
               lhs_scr, m_scr, acc_scr, s0_scr, s1_scr, *, tq, tk, n_kv, n_near, u_lo, u_hi, out_scale):
    h = pl.program_id(1)
    qi = pl.program_id(2)
    cur = qi % 2
    r = tk // tq

    def stack(nxt):
        _stack_halves([(qn_ref if nxt else q_ref)[...]], lhs_scr.at[1 - cur if nxt else cur], tq)

    def first_near(q_idx):
        return jnp.clip((q_idx + u_lo) // r + 1, 0, n_kv - n_near)

    def tile_of(p, near, q_idx):
        t_a = first_near(q_idx)
        if near:
            return t_a + (p - (n_kv - n_near))
        return jnp.where(p < t_a, p, p + n_near)

    def score(p, near, nxt):
        q_idx = qi + 1 if nxt else qi
        t = tile_of(p, near, q_idx)
        s = _qk(lhs_scr[1 - cur if nxt else cur], k_ref[_key_rows(t, tk), :])
        if not near:
            return s
        u = jnp.clip(t * r - q_idx, u_lo, u_hi) - u_lo
        return (s.reshape(2, tq, tk) + bias_ref[0, u][None]).reshape(2 * tq, tk)

    def consume(s, p, near):
        t = tile_of(p, near, qi)
        v = v_ref[_key_rows(t, tk), :]
        if near:
            _consume(s, v, m_scr, acc_scr)
        else:
            _consume(s, v, m_scr, acc_scr, const=jnp.where(t < first_near(qi), far_ref[0, h], far_ref[1, h]))

    _pipelined_attention(qi, n_kv, n_near, stack, score, consume, m_scr, acc_scr, (s0_scr, s1_scr))
    o = acc_scr[:, :LANES] / acc_scr[:, LANES:]
    o = o[:tq] - lam_ref[0] * o[tq:]
    o_ref[...] = (_rms(o, g_ref[...], DIFF_SUBLN_EPS) * out_scale).astype(o_ref.dtype)


def _diff(lam, far, q, k, v, bias, subln_g, nseq, seq_len, row_off, lam_init, tq, tk):
    n_heads = q.shape[1] // LANES
    nq = seq_len // tq
    assert nq == 1 or nq % 2 == 0
    qoff = row_off // tq
    soff = row_off // seq_len
    u_lo, u_hi = _bias_span(tq, tk)
    n_off = u_hi - u_lo + 1
    kv_spec = pl.BlockSpec((seq_len, LANES), lambda b, h, i: (soff + b, h))
    n_kv = seq_len // tk
    n_near = min(_near_count(tq, tk, u_lo, u_hi), n_kv)
    return pl.pallas_call(
        functools.partial(_diff_body, tq=tq, tk=tk, n_kv=n_kv, n_near=n_near, u_lo=u_lo, u_hi=u_hi,
                          out_scale=1.0 - lam_init),
        grid=(nseq, n_heads, nq),
        in_specs=[pl.BlockSpec(memory_space=pltpu.SMEM),
                  pl.BlockSpec(memory_space=pltpu.SMEM),
                  pl.BlockSpec((tq, LANES), lambda b, h, i: (qoff + b * nq + i, h)),
                  pl.BlockSpec((tq, LANES), lambda b, h, i: (qoff + b * nq + jnp.minimum(i + 1, nq - 1), h)),
                  kv_spec, kv_spec,
                  pl.BlockSpec((1, n_off, tq, tk), lambda b, h, i: (h, 0, 0, 0),
                               pipeline_mode=pl.Buffered(1)),
                  pl.BlockSpec((1, LANES), lambda b, h, i: (0, 0))],
        out_specs=pl.BlockSpec((tq, LANES), lambda b, h, i: (b * nq + i, h)),
        out_shape=jax.ShapeDtypeStruct((nseq * seq_len, q.shape[1]), BF16),
        scratch_shapes=_attn_scratch(2 * tq, tk),
        compiler_params=_params(3),
        name="diff_attn",
    )(lam, far, q, q, k, v, bias, subln_g.reshape(1, LANES))


def _merge_body(x_ref, yh_ref, yg_ref, yd_ref, gate_ref, wb_ref, wo_ref, o_ref):
    d = x_ref.shape[1]
    merged = (gate_ref[:, 0:d].astype(F32) * _dot(yh_ref[...], wb_ref[0])
              + gate_ref[:, d:2 * d].astype(F32) * _dot(yg_ref[...], wb_ref[1])
              + gate_ref[:, 2 * d:3 * d].astype(F32) * _dot(yd_ref[...], wb_ref[2]))
    o_ref[...] = x_ref[...] + _dot(merged.astype(BF16), wo_ref[...])


def _merge(x, y_hy, y_gqa, y_diff, gates, wb, wo):
    t, d = x.shape
    db = y_hy.shape[1]
    tm = min(MERGE_TM, t)
    row = lambda i: (i, 0)
    return pl.pallas_call(
        _merge_body,
        grid=(t // tm,),
        in_specs=[pl.BlockSpec((tm, d), row),
                  pl.BlockSpec((tm, db), row), pl.BlockSpec((tm, db), row), pl.BlockSpec((tm, db), row),
                  pl.BlockSpec((tm, 3 * d), row),
                  pl.BlockSpec(wb.shape, lambda i: (0, 0, 0)),
                  pl.BlockSpec(wo.shape, lambda i: (0, 0))],
        out_specs=pl.BlockSpec((tm, d), row),
        out_shape=jax.ShapeDtypeStruct((t, d), F32),
        compiler_params=_params(1),
        name="merge",
    )(x, y_hy, y_gqa, y_diff, gates, wb, wo)


def _rope_tables(max_len):
    pos = np.arange(max_len)
    half = HEAD_DIM // 2
    inv = ROPE_THETA ** (-jnp.arange(0, half, 2, dtype=F32) / half)
    row = jnp.asarray(pos // GRID_W, F32)
    col = jnp.asarray(pos % GRID_W, F32)
    ang = jnp.concatenate([row[:, None] * inv, col[:, None] * inv], axis=-1)
    cos, sin = jnp.cos(ang), jnp.sin(ang)
    cos_h = jnp.concatenate([cos, cos], axis=1)
    sin_h = jnp.concatenate([-sin, sin], axis=1)
    return jnp.tile(cos_h, (1, LANES // HEAD_DIM)), jnp.tile(sin_h, (1, LANES // HEAD_DIM))


def _column_layout(d_model, d_hy):
    d_gqa = N_Q_HEADS * HEAD_DIM
    d_kv = N_KV_HEADS * HEAD_DIM
    d_diff = N_DIFF_HEADS * 2 * HEAD_DIM
    widths = [3 * d_hy, d_gqa, d_kv, d_kv, d_diff, d_diff, d_diff, 3 * d_model]
    offs = [0]
    for w in widths:
        offs.append(offs[-1] + w)
    deint = np.concatenate([np.arange(0, HEAD_DIM, 2), np.arange(1, HEAD_DIM, 2)])
    q_heads = [kv * GQA_GROUP + g for g in range(GQA_GROUP) for kv in range(N_KV_HEADS)]
    q_cols = np.concatenate([offs[1] + h * HEAD_DIM + deint for h in q_heads])
    k_cols = np.concatenate([offs[2] + h * HEAD_DIM + deint for h in range(N_KV_HEADS)])
    qkv_cols = np.concatenate([q_cols, k_cols, np.arange(offs[3], offs[4])])
    out_rows = np.concatenate([h * HEAD_DIM + np.arange(HEAD_DIM) for h in q_heads])
    return offs, qkv_cols, deint, out_rows


def kernel(x_prompt, x_sample, ffn1_norm, ffn1_w_in, ffn1_w_out, mix_norm, w_in, hy_conv_w, hy_conv_b, hy_filt_w1, hy_filt_b1, hy_filt_w2, hy_filt_b2, hy_filt_w3, hy_filt_freq, hy_skip, gqa_q_norm, gqa_k_norm, diff_lambda, diff_subln, rel_bias, w_branch, w_out, ffn2_norm, ffn2_w_in, ffn2_w_out, final_norm):
    b1, l1, d = x_prompt.shape
    b2, l2, _ = x_sample.shape
    depth = w_in.shape[0]
    d_hy = hy_skip.shape[1]
    bl1 = b1 * l1
    groups = ((bl1, l1), (b2 * l2, l2))
    assert bl1 % l2 == 0 and l1 % GRID_W == 0 and l2 % GRID_W == 0

    x = jnp.concatenate([x_prompt.reshape(bl1, d), x_sample.reshape(b2 * l2, d)], axis=0)

    offs, qkv_cols, deint, out_rows = _column_layout(d, d_hy)
    cos_t, sin_t = _rope_tables(max(l1, l2))
    bd = jnp.asarray(np.kron(np.eye(LANES // HEAD_DIM), np.full((HEAD_DIM, HEAD_DIM), 1.0 / HEAD_DIM)), BF16)
    tabs = {l: _dft_tables(l) for l in sorted({l1, l2})}
    seq_groups = [(b1, l1, 0), (b2, l2, bl1)]
    diff_tiles = {l: (min(DIFF_TQ, l), min(DIFF_TK, l)) for l in (l1, l2)}
    bias = {tt: _bias_tiles(rel_bias, *tt) for tt in sorted(set(diff_tiles.values()))}
    far_bias = rel_bias[jnp.array([N_BUCKETS // 2 - 1, N_BUCKETS - 1])].astype(F32) * LOG2E

    for l in range(depth):
        x = _ffn(x, ffn1_norm[l], ffn1_w_in[l].astype(BF16), ffn1_w_out[l].astype(BF16))

        w_l = w_in[l].astype(BF16)
        w_qkv = w_in[l][:, qkv_cols].astype(BF16)
        qn = jnp.tile(gqa_q_norm[l][deint], N_Q_HEADS).reshape(1, -1)
        kn = jnp.tile(gqa_k_norm[l][deint], N_KV_HEADS).reshape(1, -1)
        hy, gq, gk, gv, dq, dk, dv, gates = _inproj(x, mix_norm[l], w_l, w_qkv, cos_t, sin_t, bd, qn, kn,
                                                     offs, groups)

        z, x0 = _shortconv(hy, hy_conv_w[l], hy_conv_b[l], groups)
        lp = diff_lambda[l].astype(F32)
        lam_init = 0.8 - 0.6 * math.exp(-0.3 * l)
        lam = (jnp.exp(jnp.sum(lp[0] * lp[1])) - jnp.exp(jnp.sum(lp[2] * lp[3])) + lam_init).reshape(1)

        y_hy, y_gqa, y_diff = [], [], []
        for nseq, sl, off in seq_groups:
            rows = slice(off, off + nseq * sl)
            kf = _hyena_kf(sl, tabs[sl], hy_filt_w1[l], hy_filt_b1[l], hy_filt_w2[l], hy_filt_b2[l],
                           hy_filt_w3[l], hy_filt_freq[l])
            y_hy.append(_hyena_longconv(z[rows], x0[rows], kf, hy_skip[l], nseq, sl, tabs[sl]))
            y_gqa.append(_gqa(gq, gk, gv, nseq, sl, off))
            tq, tk = diff_tiles[sl]
            y_diff.append(_diff(lam, far_bias, dq, dk, dv, bias[(tq, tk)], diff_subln[l], nseq, sl, off,
                                lam_init, tq, tk))
        y_hy = jnp.concatenate(y_hy, axis=0)
        y_gqa = jnp.concatenate(y_gqa, axis=0)
        y_diff = jnp.concatenate(y_diff, axis=0)

        wb = w_branch[l].at[1].set(w_branch[l][1][out_rows]).astype(BF16)
        x = _merge(x, y_hy, y_gqa, y_diff, gates, wb, w_out[l].astype(BF16))
        w2_in, w2_out = ffn2_w_in[l].astype(BF16), ffn2_w_out[l].astype(BF16)
        if l < depth - 1:
            x = _ffn(x, ffn2_norm[l], w2_in, w2_out)

    y1 = _ffn(x, ffn2_norm[depth - 1], w2_in, w2_out, final_g=final_norm, row_off=0, n_rows=bl1)
    y2 = _ffn(x, ffn2_norm[depth - 1], w2_in, w2_out, final_g=final_norm, row_off=bl1, n_rows=b2 * l2)
    return y1.reshape(b1, l1, d), y2.reshape(b2, l2, d)
```

```python
import functools
import math

import numpy as np
import jax
import jax.numpy as jnp
from jax import lax
from jax.experimental import pallas as pl
from jax.experimental.pallas import tpu as pltpu

F32 = jnp.float32
BF16 = jnp.bfloat16

NORM_EPS = 1e-6
GRID_W = 64
HEAD_DIM = 64
N_Q_HEADS = 8
N_KV_HEADS = 2
GQA_GROUP = N_Q_HEADS // N_KV_HEADS
ROPE_THETA = 10000.0
N_DIFF_HEADS = 4
DIFF_SUBLN_EPS = 1e-5
N_BUCKETS = 32
MAX_DISTANCE = 128
POS_BANDS = 16
DECAY_TARGET = 1e-2
FAST_DECAY_PCT = 0.3
SLOW_DECAY_PCT = 1.5

LANES = 128
MXU_WIDTH = 256
VMEM_LIMIT_BYTES = 56 * 1024 * 1024

FFN_TM = 512
PROJ_TM = 512
MERGE_TM = 512
CONV_TM = 512
GQA_TQ = 128
GQA_TK = 1024
DIFF_TQ = 512
DIFF_TK = 1024
FFT_CC = 8
CONV_CHAINS = 2
FILT_TM = 512

NEG_BIG = -1e30
LOG2E = math.log2(math.e)
KV_TILES_PER_ITER = 8


def _params(n_axes):
    return pltpu.CompilerParams(dimension_semantics=("arbitrary",) * n_axes,
                                vmem_limit_bytes=VMEM_LIMIT_BYTES)


def _dot(a, b):
    return jnp.dot(a, b, preferred_element_type=F32)


def _split_bf16(x):
    hi = x.astype(BF16)
    return hi, (x - hi.astype(F32)).astype(BF16)


def _dot3(a, b):
    ah, al = _split_bf16(a)
    bh, bl = _split_bf16(b)
    return _dot(ah, bh) + (_dot(ah, bl) + _dot(al, bh))


def _rms(x, g, eps):
    ms = jnp.mean(x * x, axis=-1, keepdims=True)
    return x * lax.rsqrt(ms + eps) * g


def _ffn_body(*refs, chunks, final):
    if final:
        x_ref, g_ref, w_in_ref, w_out_ref, gf_ref, o_ref = refs
    else:
        x_ref, g_ref, w_in_ref, w_out_ref, o_ref = refs
    d_ff = w_out_ref.shape[0]
    x = x_ref[...]
    h = _rms(x, g_ref[...], NORM_EPS).astype(BF16)
    acc = None
    for a, b in chunks:
        gate = _dot(h, w_in_ref[:, a:b])
        up = _dot(h, w_in_ref[:, d_ff + a:d_ff + b])
        part = _dot((gate * jax.nn.sigmoid(gate) * up).astype(BF16), w_out_ref[a:b, :])
        acc = part if acc is None else acc + part
    y = x + 0.5 * acc
    if final:
        y = _rms(y, gf_ref[...], NORM_EPS)
    o_ref[...] = y


def _ffn_chunks(d_ff):
    n_blk = d_ff // MXU_WIDTH
    if d_ff % MXU_WIDTH or n_blk < 2:
        return ((0, d_ff),)
    cut = (n_blk + 1) // 2 * MXU_WIDTH
    return ((0, cut), (cut, d_ff))


def _ffn(x, norm_g, w_in, w_out, final_g=None):
    t, d = x.shape
    d_ff = w_out.shape[0]
    tm = min(FFN_TM, t)
    final = final_g is not None
    const = lambda i: (0, 0)
    in_specs = [
        pl.BlockSpec((tm, d), lambda i: (i, 0)),
        pl.BlockSpec((1, d), const),
        pl.BlockSpec(w_in.shape, const, pipeline_mode=pl.Buffered(1)),
        pl.BlockSpec(w_out.shape, const, pipeline_mode=pl.Buffered(1)),
    ]
    args = [x, norm_g.reshape(1, d), w_in, w_out]
    if final:
        in_specs.append(pl.BlockSpec((1, d), const))
        args.append(final_g.reshape(1, d))
    return pl.pallas_call(
        functools.partial(_ffn_body, chunks=_ffn_chunks(d_ff), final=final),
        grid=(t // tm,),
        in_specs=in_specs,
        out_specs=pl.BlockSpec((tm, d), lambda i: (i, 0)),
        out_shape=jax.ShapeDtypeStruct((t, d), F32),
        compiler_params=_params(1),
        name="ffn",
    )(*args)


def _head_rms(x, bd):
    x2 = x * x
    hi = x2.astype(BF16)
    lo = (x2 - hi.astype(F32)).astype(BF16)
    ms = _dot(hi, bd) + _dot(lo, bd)
    return x * lax.rsqrt(ms + NORM_EPS)


def _rope(x, cos, sin_signed):
    lane = lax.broadcasted_iota(jnp.int32, x.shape, 1)
    partner = jnp.where((lane % HEAD_DIM) < HEAD_DIM // 2,
                        pltpu.roll(x, LANES - HEAD_DIM // 2, 1),
                        pltpu.roll(x, HEAD_DIM // 2, 1))
    return x * cos + partner * sin_signed


def _inproj_body(x_ref, g_ref, w_ref, wqkv_ref, cos_ref, sin_ref, bd_ref, qn_ref, kn_ref,
                 hy_ref, gq_ref, gk_ref, gv_ref, dq_ref, dk_ref, dv_ref, gate_ref, *, offs):
    o_hy, o_gq, o_gk, o_gv, o_dq, o_dk, o_dv, o_gate, o_end = offs
    h = _rms(x_ref[...], g_ref[...], NORM_EPS).astype(BF16)

    def seg(a, b):
        return _dot(h, w_ref[:, a:b])

    hy_ref[...] = seg(o_hy, o_gq)
    cos = cos_ref[...]
    sin = sin_ref[...]
    bd = bd_ref[...]
    scale = HEAD_DIM ** -0.5 * LOG2E
    qkv = _dot(h, wqkv_ref[...])
    wq, wk = o_gk - o_gq, o_gv - o_gk
    for c in range(wq // LANES):
        y = _head_rms(qkv[:, c * LANES:(c + 1) * LANES], bd) * qn_ref[:, c * LANES:(c + 1) * LANES]
        gq_ref[:, c * LANES:(c + 1) * LANES] = (_rope(y, cos, sin) * scale).astype(BF16)
    for c in range(wk // LANES):
        y = _head_rms(qkv[:, wq + c * LANES:wq + (c + 1) * LANES], bd) * kn_ref[:, c * LANES:(c + 1) * LANES]
        gk_ref[:, c * LANES:(c + 1) * LANES] = _rope(y, cos, sin).astype(BF16)
    gv_ref[...] = qkv[:, wq + wk:].astype(BF16)
    dq_ref[...] = (seg(o_dq, o_dk) * scale).astype(BF16)
    dk_ref[...] = seg(o_dk, o_dv).astype(BF16)
    dv_ref[...] = seg(o_dv, o_gate).astype(BF16)
    d = x_ref.shape[1]
    for c in range((o_end - o_gate) // d):
        gate_ref[:, c * d:(c + 1) * d] = jax.nn.sigmoid(seg(o_gate + c * d, o_gate + (c + 1) * d)).astype(BF16)


def _inproj(x, norm_g, w, wqkv, cos_t, sin_t, bd, qn, kn, offs, seq_len):
    t, d = x.shape
    tm = min(PROJ_TM, seq_len)
    widths = [offs[i + 1] - offs[i] for i in range(8)]

    def pos_map(i):
        return (i % (seq_len // tm), 0)

    full = lambda i: (0, 0)
    row = lambda i: (i, 0)
    out_dtypes = [F32] + [BF16] * 7
    return pl.pallas_call(
        functools.partial(_inproj_body, offs=tuple(offs)),
        grid=(t // tm,),
        in_specs=[
            pl.BlockSpec((tm, d), row),
            pl.BlockSpec((1, d), full),
            pl.BlockSpec(w.shape, full, pipeline_mode=pl.Buffered(1)),
            pl.BlockSpec(wqkv.shape, full, pipeline_mode=pl.Buffered(1)),
            pl.BlockSpec((tm, LANES), pos_map),
            pl.BlockSpec((tm, LANES), pos_map),
            pl.BlockSpec(bd.shape, full),
            pl.BlockSpec(qn.shape, full),
            pl.BlockSpec(kn.shape, full),
        ],
        out_specs=[pl.BlockSpec((tm, wd), row) for wd in widths],
        out_shape=[jax.ShapeDtypeStruct((t, wd), dt) for wd, dt in zip(widths, out_dtypes)],
        compiler_params=_params(1),
        name="inproj",
    )(x, norm_g.reshape(1, d), w, wqkv, cos_t, sin_t, bd, qn, kn)


def _shortconv_body(u_ref, prev_ref, next_ref, w_ref, b_ref, z_ref, x0_ref, *, tm, seq_len, d_hy):
    pos = (pl.program_id(0) * tm) % seq_len
    u = u_ref[...]
    rows = lax.broadcasted_iota(jnp.int32, u.shape, 0)
    before = jnp.where(pos == 0, 0.0, prev_ref[7:8, :])
    after = jnp.where(pos + tm == seq_len, 0.0, next_ref[0:1, :])
    up = jnp.where(rows == 0, before, pltpu.roll(u, 1, 0))
    dn = jnp.where(rows == tm - 1, after, pltpu.roll(u, tm - 1, 0))
    y = up * w_ref[0:1, :] + u * w_ref[1:2, :] + dn * w_ref[2:3, :] + b_ref[...]
    x0_ref[...] = y[:, :d_hy]
    z_ref[...] = y[:, 2 * d_hy:] * y[:, d_hy:2 * d_hy]


def _shortconv(u, w, b, seq_len):
    t, c3 = u.shape
    d_hy = c3 // 3
    tm = min(CONV_TM, seq_len)
    nb8 = t // 8
    return pl.pallas_call(
        functools.partial(_shortconv_body, tm=tm, seq_len=seq_len, d_hy=d_hy),
        grid=(t // tm,),
        in_specs=[
            pl.BlockSpec((tm, c3), lambda i: (i, 0)),
            pl.BlockSpec((8, c3), lambda i: (jnp.maximum(i * (tm // 8) - 1, 0), 0)),
            pl.BlockSpec((8, c3), lambda i: (jnp.minimum((i + 1) * (tm // 8), nb8 - 1), 0)),
            pl.BlockSpec((3, c3), lambda i: (0, 0)),
            pl.BlockSpec((1, c3), lambda i: (0, 0)),
        ],
        out_specs=[pl.BlockSpec((tm, d_hy), lambda i: (i, 0))] * 2,
        out_shape=[jax.ShapeDtypeStruct((t, d_hy), F32)] * 2,
        compiler_params=_params(1),
        name="shortconv",
    )(u, u, u, w, b.reshape(1, c3))


def _filter_body(f_ref, w1_ref, b1_ref, w2_ref, b2_ref, w3_ref, fr_ref, dl_ref, o_ref, *, n_feat):
    f = f_ref[...]
    fr = fr_ref[...]
    h = jnp.sin(fr * (_dot3(f, w1_ref[...]) + b1_ref[...]))
    h = jnp.sin(fr * (_dot3(h, w2_ref[...]) + b2_ref[...]))
    h = _dot3(h, w3_ref[...])
    tpos = f[:, 0:1]
    valid = f[:, n_feat:n_feat + 1]
    o_ref[...] = h * jnp.exp(-tpos * dl_ref[...]) * valid


def _hyena_filter(feats, w1p, b1, w2, b2, w3, freq, deltas_abs, seq_len, n_feat):
    n2l, fw = feats.shape
    d_hy = w3.shape[1] // 2
    hid = w2.shape[0]
    tm = min(FILT_TM, seq_len)
    nbl = seq_len // tm
    full = lambda i: (0, 0)
    return pl.pallas_call(
        functools.partial(_filter_body, n_feat=n_feat),
        grid=(n2l // tm,),
        in_specs=[
            pl.BlockSpec((tm, fw), lambda i: (i, 0)),
            pl.BlockSpec(w1p.shape, full),
            pl.BlockSpec((1, hid), full),
            pl.BlockSpec(w2.shape, full),
            pl.BlockSpec((1, hid), full),
            pl.BlockSpec((hid, d_hy), lambda i: (0, i // nbl)),
            pl.BlockSpec((1, hid), full),
            pl.BlockSpec((1, d_hy), full),
        ],
        out_specs=pl.BlockSpec((tm, d_hy), lambda i: (i, 0)),
        out_shape=jax.ShapeDtypeStruct((n2l, d_hy), F32),
        compiler_params=_params(1),
        name="hyena_filter",
    )(feats, w1p, b1.reshape(1, hid), w2, b2.reshape(1, hid), w3, freq.reshape(1, hid),
      deltas_abs.reshape(1, d_hy))


def _fwd_spectrum(f1, x, twr, twi, w2f, cc):
    n1 = twr.shape[0]
    a = _dot3(f1, x)
    rows = []
    for c in range(cc):
        ar = a[:n1, c * LANES:(c + 1) * LANES]
        ai = a[n1:, c * LANES:(c + 1) * LANES]
        rows.append(jnp.concatenate([ar * twr - ai * twi, ar * twi + ai * twr], axis=1))
    xs = _dot3(jnp.concatenate(rows, axis=0), w2f)
    return xs[:, :LANES], xs[:, LANES:]


def _kf_body(f1_ref, x_ref, twr_ref, twi_ref, w2f_ref, o_ref, *, cc, scale):
    n1 = twr_ref.shape[0]
    xr, xi = _fwd_spectrum(f1_ref[...], x_ref[...], twr_ref[...], twi_ref[...], w2f_ref[...], cc)
    for c in range(cc):
        o_ref[0, :, c * LANES:(c + 1) * LANES] = xr[c * n1:(c + 1) * n1] * scale
        o_ref[1, :, c * LANES:(c + 1) * LANES] = xi[c * n1:(c + 1) * n1] * scale


def _conv_chain(f1, finv, z, x0, kf_r, kf_i, skip, twr, twi, w2f, w2i, cc):
    n1 = twr.shape[0]
    xr, xi = _fwd_spectrum(f1, z, twr, twi, w2f, cc)
    kr = jnp.concatenate([kf_r[:, c * LANES:(c + 1) * LANES] for c in range(cc)], axis=0)
    ki = jnp.concatenate([kf_i[:, c * LANES:(c + 1) * LANES] for c in range(cc)], axis=0)
    y = jnp.concatenate([xr * kr - xi * ki, xr * ki + xi * kr], axis=1)
    cs = _dot3(y, w2i)
    d_re, d_im = [], []
    for c in range(cc):
        cr = cs[c * n1:(c + 1) * n1, :LANES]
        ci = cs[c * n1:(c + 1) * n1, LANES:]
        d_re.append(cr * twr + ci * twi)
        d_im.append(ci * twr - cr * twi)
    d = jnp.concatenate([jnp.concatenate(d_re, axis=1), jnp.concatenate(d_im, axis=1)], axis=0)
    conv = _dot3(finv, d)
    return x0 * (conv + z * skip)


def _conv_body(f1_ref, finv_ref, z_ref, x0_ref, kf_ref, skip_ref, twr_ref, twi_ref, w2f_ref, w2i_ref,
               o_ref, *, cc, n_sub):
    w = cc * LANES
    for j in range(n_sub):
        cols = slice(j * w, (j + 1) * w)
        o_ref[0, :, cols] = _conv_chain(
            f1_ref[...], finv_ref[...], z_ref[0, :, cols], x0_ref[0, :, cols], kf_ref[0, :, cols],
            kf_ref[1, :, cols], skip_ref[:, cols], twr_ref[...], twi_ref[...], w2f_ref[...], w2i_ref[...],
            cc).astype(o_ref.dtype)


def _dft_tables(seq_len):
    n = 2 * seq_len
    n1 = n // LANES
    n1h = n1 // 2
    k1 = np.arange(n1)[:, None]
    ang1 = 2.0 * np.pi * k1 * np.arange(n1)[None, :] / n1
    f1 = np.concatenate([np.cos(ang1), -np.sin(ang1)], axis=0)
    finv = np.concatenate([np.cos(ang1[:n1h]), -np.sin(ang1[:n1h])], axis=1)
    ang2 = 2.0 * np.pi * np.arange(LANES)[:, None] * np.arange(LANES)[None, :] / LANES
    f2r, f2i = np.cos(ang2), -np.sin(ang2)
    w2f = np.block([[f2r, f2i], [-f2i, f2r]])
    w2i = np.block([[f2r, -f2i], [f2i, f2r]])
    angt = 2.0 * np.pi * k1 * np.arange(LANES)[None, :] / n
    f = lambda a: jnp.asarray(a, F32)
    return dict(n=n, n1=n1, n1h=n1h, f1_full=f(f1), f1_half=f(f1[:, :n1h]), finv=f(finv),
                w2f=f(w2f), w2i=f(w2i), twr=f(np.cos(angt)), twi=f(-np.sin(angt)))


def _to_blocked(x, nseq):
    t, c = x.shape
    r = t // (nseq * LANES)
    return x.reshape(nseq, r, LANES, c).transpose(0, 1, 3, 2).reshape(nseq, r, c * LANES)


def _from_blocked(y, c):
    nseq, r, _ = y.shape
    return y.reshape(nseq, r, c, LANES).transpose(0, 1, 3, 2).reshape(nseq * r * LANES, c)


def _filter_features(seq_len, fw):
    idx = np.arange(2 * seq_len)
    lag = np.where(idx < seq_len, idx, 2 * seq_len - idx).astype(np.int64)
    lag = np.minimum(lag, seq_len - 1)
    t = jnp.linspace(0.0, 1.0, seq_len, dtype=F32)[:, None]
    band = jnp.linspace(1e-4, POS_BANDS - 1, POS_BANDS, dtype=F32)
    ang = (2.0 * math.pi / seq_len) * jnp.arange(seq_len, dtype=F32)[:, None] * band[None, :]
    feats = jnp.concatenate([t, jnp.cos(ang), -jnp.sin(ang)], axis=-1)
    n_feat = feats.shape[1]
    valid = jnp.asarray((idx != seq_len).astype(np.float32))[:, None]
    rows = jnp.concatenate([feats[lag], valid], axis=1)
    return jnp.pad(rows, ((0, 0), (0, fw - n_feat - 1))), n_feat


def _hyena_kf(seq_len, tabs, w1, b1, w2, b2, w3, freq):
    d_hy = w3.shape[1] // 2
    feats, n_feat = _filter_features(seq_len, LANES)
    w1p = jnp.pad(w1, ((0, LANES - w1.shape[0]), (0, 0)))
    max_decay = math.log(DECAY_TARGET) / FAST_DECAY_PCT
    min_decay = math.log(DECAY_TARGET) / SLOW_DECAY_PCT
    deltas = jnp.abs(jnp.linspace(min_decay, max_decay, d_hy, dtype=F32))
    kern = _hyena_filter(feats, w1p, b1, w2, b2, w3, freq, deltas, seq_len, n_feat)
    n, n1 = tabs["n"], tabs["n1"]
    cc = FFT_CC
    w = cc * LANES
    full = lambda j: (0, 0)
    return pl.pallas_call(
        functools.partial(_kf_body, cc=cc, scale=1.0 / n),
        grid=(d_hy // cc,),
        in_specs=[pl.BlockSpec((2 * n1, n1), full),
                  pl.BlockSpec((n1, w), lambda j: (0, j)),
                  pl.BlockSpec((n1, LANES), full), pl.BlockSpec((n1, LANES), full),
                  pl.BlockSpec((2 * LANES, 2 * LANES), full)],
        out_specs=pl.BlockSpec((2, n1, w), lambda j: (0, 0, j)),
        out_shape=jax.ShapeDtypeStruct((2, n1, d_hy * LANES), F32),
        compiler_params=_params(1),
        name="hyena_kf",
    )(tabs["f1_full"], _to_blocked(kern, 1)[0], tabs["twr"], tabs["twi"], tabs["w2f"])


def _hyena_longconv(z, x0, kf, skip, nseq, seq_len, tabs):
    c = z.shape[1]
    n1, n1h = tabs["n1"], tabs["n1h"]
    cc = FFT_CC
    n_sub = CONV_CHAINS
    w = n_sub * cc * LANES
    full = lambda b, j: (0, 0)
    rowblk = pl.BlockSpec((1, n1h, w), lambda b, j: (b, 0, j))
    y = pl.pallas_call(
        functools.partial(_conv_body, cc=cc, n_sub=n_sub),
        grid=(nseq, c // (n_sub * cc)),
        in_specs=[pl.BlockSpec((2 * n1, n1h), full),
                  pl.BlockSpec((n1h, 2 * n1), full),
                  rowblk, rowblk,
                  pl.BlockSpec((2, n1, w), lambda b, j: (0, 0, j)),
                  pl.BlockSpec((1, w), lambda b, j: (0, j)),
                  pl.BlockSpec((n1, LANES), full), pl.BlockSpec((n1, LANES), full),
                  pl.BlockSpec((2 * LANES, 2 * LANES), full), pl.BlockSpec((2 * LANES, 2 * LANES), full)],
        out_specs=rowblk,
        out_shape=jax.ShapeDtypeStruct((nseq, n1h, c * LANES), BF16),
        compiler_params=_params(2),
        name="hyena_conv",
    )(tabs["f1_half"], tabs["finv"], _to_blocked(z, nseq), _to_blocked(x0, nseq), kf,
      jnp.repeat(skip.astype(F32), LANES).reshape(1, c * LANES),
      tabs["twr"], tabs["twi"], tabs["w2f"], tabs["w2i"])
    return _from_blocked(y, c)


def _stack_halves(q_cols, lhs_ref, tq):
    lane = lax.broadcasted_iota(jnp.int32, (tq, LANES), 1)
    low = lane < HEAD_DIM
    zero = jnp.zeros((tq, LANES), lhs_ref.dtype)
    for c, q in enumerate(q_cols):
        lhs_ref[(2 * c) * tq:(2 * c + 1) * tq, :] = jnp.where(low, q, zero)
        lhs_ref[(2 * c + 1) * tq:(2 * c + 2) * tq, :] = jnp.where(low, zero, q)


def _consume(s, v, m_scr, acc_scr, const=None):
    tk = s.shape[1]
    m_prev = m_scr[...]
    row_max = jnp.max(s, axis=1, keepdims=True)
    if const is not None:
        row_max = row_max + const
    m_new = jnp.maximum(m_prev, row_max)
    shift = m_new if const is None else m_new - const
    p = jnp.exp2(s - jnp.concatenate([shift] * (tk // LANES), axis=1)).astype(BF16)
    alpha = jnp.exp2(m_prev - m_new)
    v_ext = jnp.concatenate([v, jnp.ones_like(v)], axis=1)
    acc_scr[...] = jnp.concatenate([alpha, alpha], axis=1) * acc_scr[...] + _dot(p, v_ext)
    m_scr[...] = m_new


def _qk(lhs, k):
    return lax.dot_general(lhs, k, (((1,), (1,)), ((), ())), preferred_element_type=F32)


def _key_rows(t, tk):
    return pl.ds(t * tk if isinstance(t, int) else pl.multiple_of(t * tk, tk), tk)


def _pipelined_attention(qi, n_kv, n_special, stack_fn, score_fn, consume_fn, m_scr, acc_scr, s_scr):
    @pl.when(qi == 0)
    def _():
        stack_fn(False)
        s_scr[0][...] = score_fn(0, n_special == n_kv, False)

    m_scr[...] = jnp.full_like(m_scr, NEG_BIG)
    acc_scr[...] = jnp.zeros_like(acc_scr)
    unroll = min(KV_TILES_PER_ITER, n_kv)
    assert n_kv % unroll == 0 and n_special <= unroll

    def run(p0, last):
        for j in range(unroll):
            special = last and j >= unroll - n_special
            if last and j == unroll - 1:
                stack_fn(True)
                s_scr[(j + 1) % 2][...] = score_fn(0, n_special == n_kv, True)
            else:
                s_scr[(j + 1) % 2][...] = score_fn(p0 + j + 1, last and j + 1 >= unroll - n_special, False)
            consume_fn(s_scr[j % 2][...], p0 + j, special)

    def body(i, carry):
        run(i * unroll, False)
        return carry

    lax.fori_loop(0, n_kv // unroll - 1, body, 0)
    run(n_kv - unroll, True)
    if unroll % 2 == 1:
        s_scr[0][...] = s_scr[1][...]


def _attn_scratch(m, tk):
    return [pltpu.VMEM((2, m, LANES), BF16), pltpu.VMEM((m, LANES), F32), pltpu.VMEM((m, 2 * LANES), F32),
            pltpu.VMEM((m, tk), F32), pltpu.VMEM((m, tk), F32)]


def _gqa_body(q_ref, qn_ref, k_ref, v_ref, o_ref, lhs_scr, m_scr, acc_scr, s0_scr, s1_scr, *, tq, tk, n_kv):
    qi = pl.program_id(1)
    cur = qi % 2
    n_col = q_ref.shape[1] // LANES

    def stack(nxt):
        ref = qn_ref if nxt else q_ref
        slot = 1 - cur if nxt else cur
        _stack_halves([ref[:, c * LANES:(c + 1) * LANES] for c in range(n_col)], lhs_scr.at[slot], tq)

    def score(t, special, nxt):
        return _qk(lhs_scr[1 - cur if nxt else cur], k_ref[_key_rows(t, tk), :])

    def consume(s, t, special):
        _consume(s, v_ref[_key_rows(t, tk), :], m_scr, acc_scr)

    _pipelined_attention(qi, n_kv, 0, stack, score, consume, m_scr, acc_scr, (s0_scr, s1_scr))
    o = acc_scr[:, :LANES] / acc_scr[:, LANES:]
    low = lax.broadcasted_iota(jnp.int32, (tq, LANES), 1) < HEAD_DIM
    for c in range(n_col):
        o_ref[:, c * LANES:(c + 1) * LANES] = jnp.where(
            low, o[(2 * c) * tq:(2 * c + 1) * tq], o[(2 * c + 1) * tq:(2 * c + 2) * tq]
        ).astype(o_ref.dtype)


def _gqa(q, k, v, nseq, seq_len):
    dq = q.shape[1]
    tq = min(GQA_TQ, seq_len)
    tk = min(GQA_TK, seq_len)
    nq = seq_len // tq
    assert nq == 1 or nq % 2 == 0
    m = 2 * (dq // LANES) * tq
    return pl.pallas_call(
        functools.partial(_gqa_body, tq=tq, tk=tk, n_kv=seq_len // tk),
        grid=(nseq, nq),
        in_specs=[pl.BlockSpec((tq, dq), lambda b, i: (b * nq + i, 0)),
                  pl.BlockSpec((tq, dq), lambda b, i: (b * nq + jnp.minimum(i + 1, nq - 1), 0)),
                  pl.BlockSpec((seq_len, LANES), lambda b, i: (b, 0)),
                  pl.BlockSpec((seq_len, LANES), lambda b, i: (b, 0))],
        out_specs=pl.BlockSpec((tq, dq), lambda b, i: (b * nq + i, 0)),
        out_shape=jax.ShapeDtypeStruct((nseq * seq_len, dq), BF16),
        scratch_shapes=_attn_scratch(m, tk),
        compiler_params=_params(2),
        name="gqa_attn",
    )(q, q, k, v)


def _bias_span(tq, tk):
    u_lo = -((MAX_DISTANCE + tk - 1 + tq - 1) // tq)
    u_hi = (MAX_DISTANCE + tq - 1 + tq - 1) // tq
    return u_lo, u_hi


def _t5_bucket(rel):
    nb = N_BUCKETS // 2
    max_exact = nb // 2
    ret = jnp.where(rel > 0, nb, 0)
    n = jnp.abs(rel)
    nf = jnp.maximum(n, 1).astype(F32)
    large = max_exact + (jnp.log(nf / max_exact) / math.log(MAX_DISTANCE / max_exact)
                         * (nb - max_exact)).astype(jnp.int32)
    large = jnp.minimum(large, nb - 1)
    return ret + jnp.where(n < max_exact, n, large)


def _bias_body(tab_ref, o_ref, *, tq, tk, u_lo):
    h = pl.program_id(0)
    u = pl.program_id(1) + u_lo
    rel = (u * tq + lax.broadcasted_iota(jnp.int32, (tq, tk), 1)
           - lax.broadcasted_iota(jnp.int32, (tq, tk), 0))
    bucket = _t5_bucket(rel)
    bias = jnp.zeros((tq, tk), F32)
    for b in range(N_BUCKETS):
        bias = jnp.where(bucket == b, tab_ref[b, h], bias)
    o_ref[0, 0] = bias * LOG2E


def _bias_tiles(rel_bias, tq, tk):
    u_lo, u_hi = _bias_span(tq, tk)
    n_off = u_hi - u_lo + 1
    n_heads = rel_bias.shape[1]
    return pl.pallas_call(
        functools.partial(_bias_body, tq=tq, tk=tk, u_lo=u_lo),
        grid=(n_heads, n_off),
        in_specs=[pl.BlockSpec(memory_space=pltpu.SMEM)],
        out_specs=pl.BlockSpec((1, 1, tq, tk), lambda h, u: (h, u, 0, 0)),
        out_shape=jax.ShapeDtypeStruct((n_heads, n_off, tq, tk), F32),
        compiler_params=_params(2),
        name="t5_bias",
    )(rel_bias.astype(F32))


def _near_count(tq, tk, u_lo, u_hi):
    r = tk // tq
    return max(sum(1 for t in range(-8, 16) if u_lo < r * t - qi < u_hi) for qi in range(r))


def _diff_body(lam_ref, far_ref, q_ref, qn_ref, k_ref, v_ref, bias_ref, g_ref, o_ref,
               lhs_scr, m_scr, acc_scr, s0_scr, s1_scr, *, tq, tk, n_kv, n_near, u_lo, u_hi, out_scale):
    h = pl.program_id(1)
    qi = pl.program_id(2)
    cur = qi % 2
    r = tk // tq

    def stack(nxt):
        _stack_halves([(qn_ref if nxt else q_ref)[...]], lhs_scr.at[1 - cur if nxt else cur], tq)

    def first_near(q_idx):
        return jnp.clip((q_idx + u_lo) // r + 1, 0, n_kv - n_near)

    def tile_of(p, near, q_idx):
        t_a = first_near(q_idx)
        if near:
            return t_a + (p - (n_kv - n_near))
        return jnp.where(p < t_a, p, p + n_near)

    def score(p, near, nxt):
        q_idx = qi + 1 if nxt else qi
        t = tile_of(p, near, q_idx)
        s = _qk(lhs_scr[1 - cur if nxt else cur], k_ref[_key_rows(t, tk), :])
        if not near:
            return s
        u = jnp.clip(t * r - q_idx, u_lo, u_hi) - u_lo
        return (s.reshape(2, tq, tk) + bias_ref[0, u][None]).reshape(2 * tq, tk)

    def consume(s, p, near):
        t = tile_of(p, near, qi)
        v = v_ref[_key_rows(t, tk), :]
        if near:
            _consume(s, v, m_scr, acc_scr)
        else:
            _consume(s, v, m_scr, acc_scr, const=jnp.where(t < first_near(qi), far_ref[0, h], far_ref[1, h]))

    _pipelined_attention(qi, n_kv, n_near, stack, score, consume, m_scr, acc_scr, (s0_scr, s1_scr))
    o = acc_scr[:, :LANES] / acc_scr[:, LANES:]
    o = o[:tq] - lam_ref[0] * o[tq:]
    o_ref[...] = (_rms(o, g_ref[...], DIFF_SUBLN_EPS) * out_scale).astype(o_ref.dtype)


def _diff(lam, far, q, k, v, bias, subln_g, nseq, seq_len, lam_init, tq, tk):
    n_heads = q.shape[1] // LANES
    nq = seq_len // tq
    assert nq == 1 or nq % 2 == 0
    u_lo, u_hi = _bias_span(tq, tk)
    n_off = u_hi - u_lo + 1
    kv_spec = pl.BlockSpec((seq_len, LANES), lambda b, h, i: (b, h))
    n_kv = seq_len // tk
    n_near = min(_near_count(tq, tk, u_lo, u_hi), n_kv)
    return pl.pallas_call(
        functools.partial(_diff_body, tq=tq, tk=tk, n_kv=n_kv, n_near=n_near, u_lo=u_lo, u_hi=u_hi,
                          out_scale=1.0 - lam_init),
        grid=(nseq, n_heads, nq),
        in_specs=[pl.BlockSpec(memory_space=pltpu.SMEM),
                  pl.BlockSpec(memory_space=pltpu.SMEM),
                  pl.BlockSpec((tq, LANES), lambda b, h, i: (b * nq + i, h)),
                  pl.BlockSpec((tq, LANES), lambda b, h, i: (b * nq + jnp.minimum(i + 1, nq - 1), h)),
                  kv_spec, kv_spec,
                  pl.BlockSpec((1, n_off, tq, tk), lambda b, h, i: (h, 0, 0, 0),
                               pipeline_mode=pl.Buffered(1)),
                  pl.BlockSpec((1, LANES), lambda b, h, i: (0, 0))],
        out_specs=pl.BlockSpec((tq, LANES), lambda b, h, i: (b * nq + i, h)),
        out_shape=jax.ShapeDtypeStruct((nseq * seq_len, q.shape[1]), BF16),
        scratch_shapes=_attn_scratch(2 * tq, tk),
        compiler_params=_params(3),
        name="diff_attn",
    )(lam, far, q, q, k, v, bias, subln_g.reshape(1, LANES))


def _merge_body(x_ref, yh_ref, yg_ref, yd_ref, gate_ref, wb_ref, wo_ref, o_ref):
    d = x_ref.shape[1]
    merged = (gate_ref[:, 0:d].astype(F32) * _dot(yh_ref[...], wb_ref[0])
              + gate_ref[:, d:2 * d].astype(F32) * _dot(yg_ref[...], wb_ref[1])
              + gate_ref[:, 2 * d:3 * d].astype(F32) * _dot(yd_ref[...], wb_ref[2]))
    o_ref[...] = x_ref[...] + _dot(merged.astype(BF16), wo_ref[...])


def _merge(x, y_hy, y_gqa, y_diff, gates, wb, wo):
    t, d = x.shape
    db = y_hy.shape[1]
    tm = min(MERGE_TM, t)
    row = lambda i: (i, 0)
    return pl.pallas_call(
        _merge_body,
        grid=(t // tm,),
        in_specs=[pl.BlockSpec((tm, d), row),
                  pl.BlockSpec((tm, db), row), pl.BlockSpec((tm, db), row), pl.BlockSpec((tm, db), row),
                  pl.BlockSpec((tm, 3 * d), row),
                  pl.BlockSpec(wb.shape, lambda i: (0, 0, 0)),
                  pl.BlockSpec(wo.shape, lambda i: (0, 0))],
        out_specs=pl.BlockSpec((tm, d), row),
        out_shape=jax.ShapeDtypeStruct((t, d), F32),
        compiler_params=_params(1),
        name="merge",
    )(x, y_hy, y_gqa, y_diff, gates, wb, wo)


def _rope_tables(max_len):
    pos = np.arange(max_len)
    half = HEAD_DIM // 2
    inv = ROPE_THETA ** (-jnp.arange(0, half, 2, dtype=F32) / half)
    row = jnp.asarray(pos // GRID_W, F32)
    col = jnp.asarray(pos % GRID_W, F32)
    ang = jnp.concatenate([row[:, None] * inv, col[:, None] * inv], axis=-1)
    cos, sin = jnp.cos(ang), jnp.sin(ang)
    cos_h = jnp.concatenate([cos, cos], axis=1)
    sin_h = jnp.concatenate([-sin, sin], axis=1)
    return jnp.tile(cos_h, (1, LANES // HEAD_DIM)), jnp.tile(sin_h, (1, LANES // HEAD_DIM))


def _column_layout(d_model, d_hy):
    d_gqa = N_Q_HEADS * HEAD_DIM
    d_kv = N_KV_HEADS * HEAD_DIM
    d_diff = N_DIFF_HEADS * 2 * HEAD_DIM
    widths = [3 * d_hy, d_gqa, d_kv, d_kv, d_diff, d_diff, d_diff, 3 * d_model]
    offs = [0]
    for w in widths:
        offs.append(offs[-1] + w)
    deint = np.concatenate([np.arange(0, HEAD_DIM, 2), np.arange(1, HEAD_DIM, 2)])
    q_heads = [kv * GQA_GROUP + g for g in range(GQA_GROUP) for kv in range(N_KV_HEADS)]
    q_cols = np.concatenate([offs[1] + h * HEAD_DIM + deint for h in q_heads])
    k_cols = np.concatenate([offs[2] + h * HEAD_DIM + deint for h in range(N_KV_HEADS)])
    qkv_cols = np.concatenate([q_cols, k_cols, np.arange(offs[3], offs[4])])
    out_rows = np.concatenate([h * HEAD_DIM + np.arange(HEAD_DIM) for h in q_heads])
    return offs, qkv_cols, deint, out_rows


def kernel(x_prompt, x_sample, ffn1_norm, ffn1_w_in, ffn1_w_out, mix_norm, w_in, hy_conv_w, hy_conv_b, hy_filt_w1, hy_filt_b1, hy_filt_w2, hy_filt_b2, hy_filt_w3, hy_filt_freq, hy_skip, gqa_q_norm, gqa_k_norm, diff_lambda, diff_subln, rel_bias, w_branch, w_out, ffn2_norm, ffn2_w_in, ffn2_w_out, final_norm):
    d = x_prompt.shape[-1]
    depth = w_in.shape[0]
    d_hy = hy_skip.shape[1]
    groups = [(x.shape[0], x.shape[1]) for x in (x_prompt, x_sample)]
    xs = [x_prompt.reshape(-1, d), x_sample.reshape(-1, d)]
    lens = sorted({sl for _, sl in groups})
    assert all(sl % GRID_W == 0 for sl in lens)

    offs, qkv_cols, deint, out_rows = _column_layout(d, d_hy)
    cos_t, sin_t = _rope_tables(max(lens))
    bd = jnp.asarray(np.kron(np.eye(LANES // HEAD_DIM), np.full((HEAD_DIM, HEAD_DIM), 1.0 / HEAD_DIM)), BF16)
    tabs = {sl: _dft_tables(sl) for sl in lens}
    diff_tiles = {sl: (min(DIFF_TQ, sl), min(DIFF_TK, sl)) for sl in lens}
    bias = {tt: _bias_tiles(rel_bias, *tt) for tt in sorted(set(diff_tiles.values()))}
    far_bias = rel_bias[jnp.array([N_BUCKETS // 2 - 1, N_BUCKETS - 1])].astype(F32) * LOG2E

    for l in range(depth):
        w1_in, w1_out = ffn1_w_in[l].astype(BF16), ffn1_w_out[l].astype(BF16)
        w2_in, w2_out = ffn2_w_in[l].astype(BF16), ffn2_w_out[l].astype(BF16)
        w_l = w_in[l].astype(BF16)
        w_qkv = w_in[l][:, qkv_cols].astype(BF16)
        qn = jnp.tile(gqa_q_norm[l][deint], N_Q_HEADS).reshape(1, -1)
        kn = jnp.tile(gqa_k_norm[l][deint], N_KV_HEADS).reshape(1, -1)
        wb = w_branch[l].at[1].set(w_branch[l][1][out_rows]).astype(BF16)
        wo = w_out[l].astype(BF16)
        lp = diff_lambda[l].astype(F32)
        lam_init = 0.8 - 0.6 * math.exp(-0.3 * l)
        lam = (jnp.exp(jnp.sum(lp[0] * lp[1])) - jnp.exp(jnp.sum(lp[2] * lp[3])) + lam_init).reshape(1)
        kfs = {sl: _hyena_kf(sl, tabs[sl], hy_filt_w1[l], hy_filt_b1[l], hy_filt_w2[l], hy_filt_b2[l],
                             hy_filt_w3[l], hy_filt_freq[l]) for sl in lens}
        final_g = final_norm if l == depth - 1 else None

        for g, (nseq, sl) in enumerate(groups):
            x = _ffn(xs[g], ffn1_norm[l], w1_in, w1_out)
            hy, gq, gk, gv, dq, dk, dv, gates = _inproj(x, mix_norm[l], w_l, w_qkv, cos_t, sin_t, bd, qn, kn,
                                                         offs, sl)
            z, x0 = _shortconv(hy, hy_conv_w[l], hy_conv_b[l], sl)
            y_hy = _hyena_longconv(z, x0, kfs[sl], hy_skip[l], nseq, sl, tabs[sl])
            y_gqa = _gqa(gq, gk, gv, nseq, sl)
            tq, tk = diff_tiles[sl]
            y_diff = _diff(lam, far_bias, dq, dk, dv, bias[(tq, tk)], diff_subln[l], nseq, sl, lam_init, tq, tk)
            x = _merge(x, y_hy, y_gqa, y_diff, gates, wb, wo)
            xs[g] = _ffn(x, ffn2_norm[l], w2_in, w2_out, final_g=final_g)

    return xs[0].reshape(x_prompt.shape), xs[1].reshape(x_sample.shape)
```

```python
import functools
import math

import numpy as np
import jax
import jax.numpy as jnp
from jax import lax
from jax.experimental import pallas as pl
from jax.experimental.pallas import tpu as pltpu

F32 = jnp.float32
BF16 = jnp.bfloat16

NORM_EPS = 1e-6
GRID_W = 64
HEAD_DIM = 64
N_Q_HEADS = 8
N_KV_HEADS = 2
GQA_GROUP = N_Q_HEADS // N_KV_HEADS
ROPE_THETA = 10000.0
N_DIFF_HEADS = 4
DIFF_SUBLN_EPS = 1e-5
N_BUCKETS = 32
MAX_DISTANCE = 128
POS_BANDS = 16
DECAY_TARGET = 1e-2
FAST_DECAY_PCT = 0.3
SLOW_DECAY_PCT = 1.5

LANES = 128
SUBLANES = 8
MXU_WIDTH = 256
VMEM_LIMIT_BYTES = 56 * 1024 * 1024

FFN_TM = 512
PROJ_TM = 512
MERGE_TM = 512
CONV_TM = 512
GQA_TQ = 128
GQA_TK = 1024
DIFF_TQ = 512
DIFF_TK = 1024
FFT_CC = 8
CONV_CHAINS = 4
FILT_TM = 512

NEG_BIG = -1e30
LOG2E = math.log2(math.e)
KV_TILES_PER_ITER = 8


def _params(n_axes):
    return pltpu.CompilerParams(dimension_semantics=("arbitrary",) * n_axes,
                                vmem_limit_bytes=VMEM_LIMIT_BYTES)


def _dot(a, b):
    return jnp.dot(a, b, preferred_element_type=F32)


def _split_bf16(x):
    hi = x.astype(BF16)
    return hi, (x - hi.astype(F32)).astype(BF16)


def _dot3(a, b):
    ah, al = _split_bf16(a)
    bh, bl = _split_bf16(b)
    return _dot(ah, bh) + (_dot(ah, bl) + _dot(al, bh))


def _rms(x, g, eps):
    ms = jnp.mean(x * x, axis=-1, keepdims=True)
    return x * lax.rsqrt(ms + eps) * g


def _ffn_body(*refs, chunks, final):
    if final:
        x_ref, g_ref, w_in_ref, w_out_ref, gf_ref, o_ref = refs
    else:
        x_ref, g_ref, w_in_ref, w_out_ref, o_ref = refs
    d_ff = w_out_ref.shape[0]
    x = x_ref[...]
    h = _rms(x, g_ref[...], NORM_EPS).astype(BF16)
    acc = None
    for a, b in chunks:
        gate = _dot(h, w_in_ref[:, a:b])
        up = _dot(h, w_in_ref[:, d_ff + a:d_ff + b])
        part = _dot((gate * jax.nn.sigmoid(gate) * up).astype(BF16), w_out_ref[a:b, :])
        acc = part if acc is None else acc + part
    y = x + 0.5 * acc
    if final:
        y = _rms(y, gf_ref[...], NORM_EPS)
    o_ref[...] = y


def _ffn_chunks(d_ff):
    n_blk = d_ff // MXU_WIDTH
    if d_ff % MXU_WIDTH or n_blk < 2:
        return ((0, d_ff),)
    cut = (n_blk + 1) // 2 * MXU_WIDTH
    return ((0, cut), (cut, d_ff))


def _ffn(x, norm_g, w_in, w_out, final_g=None):
    t, d = x.shape
    d_ff = w_out.shape[0]
    tm = min(FFN_TM, t)
    final = final_g is not None
    const = lambda i: (0, 0)
    in_specs = [
        pl.BlockSpec((tm, d), lambda i: (i, 0)),
        pl.BlockSpec((1, d), const),
        pl.BlockSpec(w_in.shape, const, pipeline_mode=pl.Buffered(1)),
        pl.BlockSpec(w_out.shape, const, pipeline_mode=pl.Buffered(1)),
    ]
    args = [x, norm_g.reshape(1, d), w_in, w_out]
    if final:
        in_specs.append(pl.BlockSpec((1, d), const))
        args.append(final_g.reshape(1, d))
    return pl.pallas_call(
        functools.partial(_ffn_body, chunks=_ffn_chunks(d_ff), final=final),
        grid=(t // tm,),
        in_specs=in_specs,
        out_specs=pl.BlockSpec((tm, d), lambda i: (i, 0)),
        out_shape=jax.ShapeDtypeStruct((t, d), F32),
        compiler_params=_params(1),
        name="ffn",
    )(*args)


def _head_rms(x, bd):
    x2 = x * x
    hi = x2.astype(BF16)
    lo = (x2 - hi.astype(F32)).astype(BF16)
    ms = _dot(hi, bd) + _dot(lo, bd)
    return x * lax.rsqrt(ms + NORM_EPS)


def _rope(x, cos, sin_signed):
    lane = lax.broadcasted_iota(jnp.int32, x.shape, 1)
    partner = jnp.where((lane % HEAD_DIM) < HEAD_DIM // 2,
                        pltpu.roll(x, LANES - HEAD_DIM // 2, 1),
                        pltpu.roll(x, HEAD_DIM // 2, 1))
    return x * cos + partner * sin_signed


def _inproj_body(x_ref, g_ref, w_ref, wqkv_ref, cos_ref, sin_ref, bd_ref, qn_ref, kn_ref,
                 hy_ref, gq_ref, gk_ref, gv_ref, dq_ref, dk_ref, dv_ref, gate_ref, *, offs):
    o_hy, o_gq, o_gk, o_gv, o_dq, o_dk, o_dv, o_gate, o_end = offs
    h = _rms(x_ref[...], g_ref[...], NORM_EPS).astype(BF16)

    def seg(a, b):
        return _dot(h, w_ref[:, a:b])

    hy_ref[...] = seg(o_hy, o_gq)
    cos = cos_ref[...]
    sin = sin_ref[...]
    bd = bd_ref[...]
    scale = HEAD_DIM ** -0.5 * LOG2E
    qkv = _dot(h, wqkv_ref[...])
    wq, wk = o_gk - o_gq, o_gv - o_gk
    for c in range(wq // LANES):
        y = _head_rms(qkv[:, c * LANES:(c + 1) * LANES], bd) * qn_ref[:, c * LANES:(c + 1) * LANES]
        gq_ref[:, c * LANES:(c + 1) * LANES] = (_rope(y, cos, sin) * scale).astype(BF16)
    for c in range(wk // LANES):
        y = _head_rms(qkv[:, wq + c * LANES:wq + (c + 1) * LANES], bd) * kn_ref[:, c * LANES:(c + 1) * LANES]
        gk_ref[:, c * LANES:(c + 1) * LANES] = _rope(y, cos, sin).astype(BF16)
    gv_ref[...] = qkv[:, wq + wk:].astype(BF16)
    dq_ref[...] = (seg(o_dq, o_dk) * scale).astype(BF16)
    dk_ref[...] = seg(o_dk, o_dv).astype(BF16)
    dv_ref[...] = seg(o_dv, o_gate).astype(BF16)
    d = x_ref.shape[1]
    for c in range((o_end - o_gate) // d):
        gate_ref[:, c * d:(c + 1) * d] = jax.nn.sigmoid(seg(o_gate + c * d, o_gate + (c + 1) * d)).astype(BF16)


def _inproj(x, norm_g, w, wqkv, cos_t, sin_t, bd, qn, kn, offs, seq_len):
    t, d = x.shape
    tm = min(PROJ_TM, seq_len)
    widths = [offs[i + 1] - offs[i] for i in range(8)]

    def pos_map(i):
        return (i % (seq_len // tm), 0)

    full = lambda i: (0, 0)
    row = lambda i: (i, 0)
    out_dtypes = [F32] + [BF16] * 7
    return pl.pallas_call(
        functools.partial(_inproj_body, offs=tuple(offs)),
        grid=(t // tm,),
        in_specs=[
            pl.BlockSpec((tm, d), row),
            pl.BlockSpec((1, d), full),
            pl.BlockSpec(w.shape, full, pipeline_mode=pl.Buffered(1)),
            pl.BlockSpec(wqkv.shape, full, pipeline_mode=pl.Buffered(1)),
            pl.BlockSpec((tm, LANES), pos_map),
            pl.BlockSpec((tm, LANES), pos_map),
            pl.BlockSpec(bd.shape, full),
            pl.BlockSpec(qn.shape, full),
            pl.BlockSpec(kn.shape, full),
        ],
        out_specs=[pl.BlockSpec((tm, wd), row) for wd in widths],
        out_shape=[jax.ShapeDtypeStruct((t, wd), dt) for wd, dt in zip(widths, out_dtypes)],
        compiler_params=_params(1),
        name="inproj",
    )(x, norm_g.reshape(1, d), w, wqkv, cos_t, sin_t, bd, qn, kn)


def _shortconv_body(u_ref, prev_ref, next_ref, w_ref, b_ref, z_ref, x0_ref, *, tm, seq_len, d_hy):
    pos = (pl.program_id(0) * tm) % seq_len
    u = u_ref[...]
    rows = lax.broadcasted_iota(jnp.int32, u.shape, 0)
    before = jnp.where(pos == 0, 0.0, prev_ref[SUBLANES - 1:SUBLANES, :])
    after = jnp.where(pos + tm == seq_len, 0.0, next_ref[0:1, :])
    up = jnp.where(rows == 0, before, pltpu.roll(u, 1, 0))
    dn = jnp.where(rows == tm - 1, after, pltpu.roll(u, tm - 1, 0))
    y = up * w_ref[0:1, :] + u * w_ref[1:2, :] + dn * w_ref[2:3, :] + b_ref[...]
    x0_ref[...] = y[:, :d_hy]
    z_ref[...] = y[:, 2 * d_hy:] * y[:, d_hy:2 * d_hy]


def _shortconv(u, w, b, seq_len):
    t, c3 = u.shape
    d_hy = c3 // 3
    tm = min(CONV_TM, seq_len)
    nb8 = t // SUBLANES
    return pl.pallas_call(
        functools.partial(_shortconv_body, tm=tm, seq_len=seq_len, d_hy=d_hy),
        grid=(t // tm,),
        in_specs=[
            pl.BlockSpec((tm, c3), lambda i: (i, 0)),
            pl.BlockSpec((SUBLANES, c3), lambda i: (jnp.maximum(i * (tm // SUBLANES) - 1, 0), 0)),
            pl.BlockSpec((SUBLANES, c3), lambda i: (jnp.minimum((i + 1) * (tm // SUBLANES), nb8 - 1), 0)),
            pl.BlockSpec((3, c3), lambda i: (0, 0)),
            pl.BlockSpec((1, c3), lambda i: (0, 0)),
        ],
        out_specs=[pl.BlockSpec((tm, d_hy), lambda i: (i, 0))] * 2,
        out_shape=[jax.ShapeDtypeStruct((t, d_hy), F32)] * 2,
        compiler_params=_params(1),
        name="shortconv",
    )(u, u, u, w, b.reshape(1, c3))


def _filter_body(f_ref, w1_ref, b1_ref, w2_ref, b2_ref, w3_ref, fr_ref, dl_ref, o_ref, *, n_feat):
    half = f_ref.shape[0] // 2
    d_hy = o_ref.shape[1]
    halves = (f_ref[:half, :], f_ref[half:, :])
    fr = fr_ref[...]
    h = jnp.sin(fr * (_dot3(jnp.concatenate(halves, axis=1), w1_ref[...]) + b1_ref[...]))
    h = jnp.sin(fr * (_dot3(h, w2_ref[...]) + b2_ref[...]))
    h = _dot3(h, w3_ref[0])
    for k, f in enumerate(halves):
        decay = jnp.exp(-f[:, 0:1] * dl_ref[...])
        o_ref[k * half:(k + 1) * half, :] = h[:, k * d_hy:(k + 1) * d_hy] * decay * f[:, n_feat:n_feat + 1]


def _hyena_filter(feats, w1p, b1, w2, b2, w3, freq, deltas_abs, seq_len, n_feat):
    n2l, fw = feats.shape
    d_hy = w3.shape[1] // 2
    hid = w2.shape[0]
    tm = min(FILT_TM, seq_len)
    nbl = seq_len // tm
    full = lambda i: (0, 0)
    twice = lambda m: jnp.kron(jnp.eye(2, dtype=m.dtype), m)
    pair = lambda v: jnp.tile(v, 2).reshape(1, -1)
    return pl.pallas_call(
        functools.partial(_filter_body, n_feat=n_feat),
        grid=(n2l // tm,),
        in_specs=[
            pl.BlockSpec((tm, fw), lambda i: (i, 0)),
            pl.BlockSpec((2 * fw, 2 * hid), full),
            pl.BlockSpec((1, 2 * hid), full),
            pl.BlockSpec((2 * hid, 2 * hid), full),
            pl.BlockSpec((1, 2 * hid), full),
            pl.BlockSpec((1, 2 * hid, 2 * d_hy), lambda i: (i // nbl, 0, 0)),
            pl.BlockSpec((1, 2 * hid), full),
            pl.BlockSpec((1, d_hy), full),
        ],
        out_specs=pl.BlockSpec((tm, d_hy), lambda i: (i, 0)),
        out_shape=jax.ShapeDtypeStruct((n2l, d_hy), F32),
        compiler_params=_params(1),
        name="hyena_filter",
    )(feats, twice(w1p), pair(b1), twice(w2), pair(b2),
      jnp.stack([twice(w3[:, :d_hy]), twice(w3[:, d_hy:])]), pair(freq), deltas_abs.reshape(1, d_hy))


def _fwd_spectrum(f1, x, twr, twi, w2f, cc):
    n1 = twr.shape[0]
    a = _dot3(f1, x)
    rows = []
    for c in range(cc):
        ar = a[:n1, c * LANES:(c + 1) * LANES]
        ai = a[n1:, c * LANES:(c + 1) * LANES]
        rows.append(jnp.concatenate([ar * twr - ai * twi, ar * twi + ai * twr], axis=1))
    xs = _dot3(jnp.concatenate(rows, axis=0), w2f)
    return xs[:, :LANES], xs[:, LANES:]


def _kf_body(f1_ref, x_ref, twr_ref, twi_ref, w2f_ref, o_ref, *, cc, scale):
    n1 = twr_ref.shape[0]
    xr, xi = _fwd_spectrum(f1_ref[...], x_ref[...], twr_ref[...], twi_ref[...], w2f_ref[...], cc)
    for c in range(cc):
        o_ref[0, :, c * LANES:(c + 1) * LANES] = xr[c * n1:(c + 1) * n1] * scale
        o_ref[1, :, c * LANES:(c + 1) * LANES] = xi[c * n1:(c + 1) * n1] * scale


def _conv_body(f1_ref, finv_ref, z_ref, x0_ref, kf_ref, skip_ref, twr_ref, twi_ref, w2f_ref, w2i_ref,
               o_ref, *, cc, n_sub):
    n1 = twr_ref.shape[0]
    w = cc * LANES
    twr, twi = twr_ref[...], twi_ref[...]
    f1, finv, w2f, w2i = f1_ref[...], finv_ref[...], w2f_ref[...], w2i_ref[...]
    cols = [slice(j * w, (j + 1) * w) for j in range(n_sub)]
    zs = [z_ref[0, :, c] for c in cols]
    a = [_dot3(f1, z) for z in zs]
    b = []
    for aj in a:
        rows = []
        for c in range(cc):
            ar = aj[:n1, c * LANES:(c + 1) * LANES]
            ai = aj[n1:, c * LANES:(c + 1) * LANES]
            rows.append(jnp.concatenate([ar * twr - ai * twi, ar * twi + ai * twr], axis=1))
        b.append(jnp.concatenate(rows, axis=0))
    xs = [_dot3(bj, w2f) for bj in b]
    ys = []
    for xj, cj in zip(xs, cols):
        xr, xi = xj[:, :LANES], xj[:, LANES:]
        kr = jnp.concatenate([kf_ref[0, :, cj][:, c * LANES:(c + 1) * LANES] for c in range(cc)], axis=0)
        ki = jnp.concatenate([kf_ref[1, :, cj][:, c * LANES:(c + 1) * LANES] for c in range(cc)], axis=0)
        ys.append(jnp.concatenate([xr * kr - xi * ki, xr * ki + xi * kr], axis=1))
    cs = [_dot3(yj, w2i) for yj in ys]
    ds = []
    for cj in cs:
        d_re, d_im = [], []
        for c in range(cc):
            cr = cj[c * n1:(c + 1) * n1, :LANES]
            ci = cj[c * n1:(c + 1) * n1, LANES:]
            d_re.append(cr * twr + ci * twi)
            d_im.append(ci * twr - cr * twi)
        ds.append(jnp.concatenate([jnp.concatenate(d_re, axis=1), jnp.concatenate(d_im, axis=1)], axis=0))
    convs = [_dot3(finv, dj) for dj in ds]
    for conv, z, cj in zip(convs, zs, cols):
        o_ref[0, :, cj] = (x0_ref[0, :, cj] * (conv + z * skip_ref[:, cj])).astype(o_ref.dtype)


def _dft_tables(seq_len):
    n = 2 * seq_len
    n1 = n // LANES
    n1h = n1 // 2
    k1 = np.arange(n1)[:, None]
    ang1 = 2.0 * np.pi * k1 * np.arange(n1)[None, :] / n1
    f1 = np.concatenate([np.cos(ang1), -np.sin(ang1)], axis=0)
    finv = np.concatenate([np.cos(ang1[:n1h]), -np.sin(ang1[:n1h])], axis=1)
    ang2 = 2.0 * np.pi * np.arange(LANES)[:, None] * np.arange(LANES)[None, :] / LANES
    f2r, f2i = np.cos(ang2), -np.sin(ang2)
    w2f = np.block([[f2r, f2i], [-f2i, f2r]])
    w2i = np.block([[f2r, -f2i], [f2i, f2r]])
    angt = 2.0 * np.pi * k1 * np.arange(LANES)[None, :] / n
    f = lambda a: jnp.asarray(a, F32)
    return dict(n=n, n1=n1, n1h=n1h, f1_full=f(f1), f1_half=f(f1[:, :n1h]), finv=f(finv),
                w2f=f(w2f), w2i=f(w2i), twr=f(np.cos(angt)), twi=f(-np.sin(angt)))


def _to_blocked(x, nseq):
    t, c = x.shape
    r = t // (nseq * LANES)
    return x.reshape(nseq, r, LANES, c).transpose(0, 1, 3, 2).reshape(nseq, r, c * LANES)


def _from_blocked(y, c):
    nseq, r, _ = y.shape
    return y.reshape(nseq, r, c, LANES).transpose(0, 1, 3, 2).reshape(nseq * r * LANES, c)


def _filter_features(seq_len, fw):
    idx = np.arange(2 * seq_len)
    lag = np.where(idx < seq_len, idx, 2 * seq_len - idx).astype(np.int64)
    lag = np.minimum(lag, seq_len - 1)
    t = jnp.linspace(0.0, 1.0, seq_len, dtype=F32)[:, None]
    band = jnp.linspace(1e-4, POS_BANDS - 1, POS_BANDS, dtype=F32)
    ang = (2.0 * math.pi / seq_len) * jnp.arange(seq_len, dtype=F32)[:, None] * band[None, :]
    feats = jnp.concatenate([t, jnp.cos(ang), -jnp.sin(ang)], axis=-1)
    n_feat = feats.shape[1]
    valid = jnp.asarray((idx != seq_len).astype(np.float32))[:, None]
    rows = jnp.concatenate([feats[lag], valid], axis=1)
    return jnp.pad(rows, ((0, 0), (0, fw - n_feat - 1))), n_feat


def _hyena_kf(seq_len, tabs, w1, b1, w2, b2, w3, freq):
    d_hy = w3.shape[1] // 2
    feats, n_feat = _filter_features(seq_len, LANES)
    w1p = jnp.pad(w1, ((0, LANES - w1.shape[0]), (0, 0)))
    max_decay = math.log(DECAY_TARGET) / FAST_DECAY_PCT
    min_decay = math.log(DECAY_TARGET) / SLOW_DECAY_PCT
    deltas = jnp.abs(jnp.linspace(min_decay, max_decay, d_hy, dtype=F32))
    kern = _hyena_filter(feats, w1p, b1, w2, b2, w3, freq, deltas, seq_len, n_feat)
    n, n1 = tabs["n"], tabs["n1"]
    cc = FFT_CC
    w = cc * LANES
    full = lambda j: (0, 0)
    return pl.pallas_call(
        functools.partial(_kf_body, cc=cc, scale=1.0 / n),
        grid=(d_hy // cc,),
        in_specs=[pl.BlockSpec((2 * n1, n1), full),
                  pl.BlockSpec((n1, w), lambda j: (0, j)),
                  pl.BlockSpec((n1, LANES), full), pl.BlockSpec((n1, LANES), full),
                  pl.BlockSpec((2 * LANES, 2 * LANES), full)],
        out_specs=pl.BlockSpec((2, n1, w), lambda j: (0, 0, j)),
        out_shape=jax.ShapeDtypeStruct((2, n1, d_hy * LANES), F32),
        compiler_params=_params(1),
        name="hyena_kf",
    )(tabs["f1_full"], _to_blocked(kern, 1)[0], tabs["twr"], tabs["twi"], tabs["w2f"])


def _hyena_longconv(z, x0, kf, skip, nseq, seq_len, tabs):
    c = z.shape[1]
    n1, n1h = tabs["n1"], tabs["n1h"]
    cc = FFT_CC
    n_sub = CONV_CHAINS
    w = n_sub * cc * LANES
    full = lambda b, j: (0, 0)
    rowblk = pl.BlockSpec((1, n1h, w), lambda b, j: (b, 0, j))
    y = pl.pallas_call(
        functools.partial(_conv_body, cc=cc, n_sub=n_sub),
        grid=(nseq, c // (n_sub * cc)),
        in_specs=[pl.BlockSpec((2 * n1, n1h), full),
                  pl.BlockSpec((n1h, 2 * n1), full),
                  rowblk, rowblk,
                  pl.BlockSpec((2, n1, w), lambda b, j: (0, 0, j)),
                  pl.BlockSpec((1, w), lambda b, j: (0, j)),
                  pl.BlockSpec((n1, LANES), full), pl.BlockSpec((n1, LANES), full),
                  pl.BlockSpec((2 * LANES, 2 * LANES), full), pl.BlockSpec((2 * LANES, 2 * LANES), full)],
        out_specs=rowblk,
        out_shape=jax.ShapeDtypeStruct((nseq, n1h, c * LANES), BF16),
        compiler_params=_params(2),
        name="hyena_conv",
    )(tabs["f1_half"], tabs["finv"], _to_blocked(z, nseq), _to_blocked(x0, nseq), kf,
      jnp.repeat(skip.astype(F32), LANES).reshape(1, c * LANES),
      tabs["twr"], tabs["twi"], tabs["w2f"], tabs["w2i"])
    return _from_blocked(y, c)


def _stack_halves(q_cols, lhs_ref, tq):
    lane = lax.broadcasted_iota(jnp.int32, (tq, LANES), 1)
    low = lane < HEAD_DIM
    zero = jnp.zeros((tq, LANES), lhs_ref.dtype)
    for c, q in enumerate(q_cols):
        lhs_ref[(2 * c) * tq:(2 * c + 1) * tq, :] = jnp.where(low, q, zero)
        lhs_ref[(2 * c + 1) * tq:(2 * c + 2) * tq, :] = jnp.where(low, zero, q)


def _consume(s, v, m_scr, acc_scr, const=None):
    tk = s.shape[1]
    m_prev = m_scr[...]
    row_max = jnp.max(s, axis=1, keepdims=True)
    if const is not None:
        row_max = row_max + const
    m_new = jnp.maximum(m_prev, row_max)
    shift = m_new if const is None else m_new - const
    p = jnp.exp2(s - jnp.concatenate([shift] * (tk // LANES), axis=1)).astype(BF16)
    alpha = jnp.exp2(m_prev - m_new)
    v_ext = jnp.concatenate([v, jnp.ones_like(v)], axis=1)
    acc_scr[...] = jnp.concatenate([alpha, alpha], axis=1) * acc_scr[...] + _dot(p, v_ext)
    m_scr[...] = m_new


def _qk(lhs, k):
    return lax.dot_general(lhs, k, (((1,), (1,)), ((), ())), preferred_element_type=F32)


def _key_rows(t, tk):
    return pl.ds(t * tk if isinstance(t, int) else pl.multiple_of(t * tk, tk), tk)


def _pipelined_attention(qi, n_kv, n_special, stack_fn, score_fn, consume_fn, m_scr, acc_scr, s_scr):
    @pl.when(qi == 0)
    def _():
        stack_fn(False)
        s_scr[0][...] = score_fn(0, n_special == n_kv, False)

    m_scr[...] = jnp.full_like(m_scr, NEG_BIG)
    acc_scr[...] = jnp.zeros_like(acc_scr)
    unroll = min(KV_TILES_PER_ITER, n_kv)
    assert n_kv % unroll == 0 and n_special <= unroll

    def run(p0, last):
        for j in range(unroll):
            special = last and j >= unroll - n_special
            if last and j == unroll - 1:
                stack_fn(True)
                s_scr[(j + 1) % 2][...] = score_fn(0, n_special == n_kv, True)
            else:
                s_scr[(j + 1) % 2][...] = score_fn(p0 + j + 1, last and j + 1 >= unroll - n_special, False)
            consume_fn(s_scr[j % 2][...], p0 + j, special)

    def body(i, carry):
        run(i * unroll, False)
        return carry

    lax.fori_loop(0, n_kv // unroll - 1, body, 0)
    run(n_kv - unroll, True)
    if unroll % 2 == 1:
        s_scr[0][...] = s_scr[1][...]


def _attn_scratch(m, tk):
    return [pltpu.VMEM((2, m, LANES), BF16), pltpu.VMEM((m, LANES), F32), pltpu.VMEM((m, 2 * LANES), F32),
            pltpu.VMEM((m, tk), F32), pltpu.VMEM((m, tk), F32)]


def _gqa_body(q_ref, qn_ref, k_ref, v_ref, o_ref, lhs_scr, m_scr, acc_scr, s0_scr, s1_scr, *, tq, tk, n_kv):
    qi = pl.program_id(1)
    cur = qi % 2
    n_col = q_ref.shape[1] // LANES

    def stack(nxt):
        ref = qn_ref if nxt else q_ref
        slot = 1 - cur if nxt else cur
        _stack_halves([ref[:, c * LANES:(c + 1) * LANES] for c in range(n_col)], lhs_scr.at[slot], tq)

    def score(t, special, nxt):
        return _qk(lhs_scr[1 - cur if nxt else cur], k_ref[_key_rows(t, tk), :])

    def consume(s, t, special):
        _consume(s, v_ref[_key_rows(t, tk), :], m_scr, acc_scr)

    _pipelined_attention(qi, n_kv, 0, stack, score, consume, m_scr, acc_scr, (s0_scr, s1_scr))
    o = acc_scr[:, :LANES] / acc_scr[:, LANES:]
    low = lax.broadcasted_iota(jnp.int32, (tq, LANES), 1) < HEAD_DIM
    for c in range(n_col):
        o_ref[:, c * LANES:(c + 1) * LANES] = jnp.where(
            low, o[(2 * c) * tq:(2 * c + 1) * tq], o[(2 * c + 1) * tq:(2 * c + 2) * tq]
        ).astype(o_ref.dtype)


def _gqa(q, k, v, nseq, seq_len):
    dq = q.shape[1]
    tq = min(GQA_TQ, seq_len)
    tk = min(GQA_TK, seq_len)
    nq = seq_len // tq
    assert nq == 1 or nq % 2 == 0
    m = 2 * (dq // LANES) * tq
    return pl.pallas_call(
        functools.partial(_gqa_body, tq=tq, tk=tk, n_kv=seq_len // tk),
        grid=(nseq, nq),
        in_specs=[pl.BlockSpec((tq, dq), lambda b, i: (b * nq + i, 0)),
                  pl.BlockSpec((tq, dq), lambda b, i: (b * nq + jnp.minimum(i + 1, nq - 1), 0)),
                  pl.BlockSpec((seq_len, LANES), lambda b, i: (b, 0)),
                  pl.BlockSpec((seq_len, LANES), lambda b, i: (b, 0))],
        out_specs=pl.BlockSpec((tq, dq), lambda b, i: (b * nq + i, 0)),
        out_shape=jax.ShapeDtypeStruct((nseq * seq_len, dq), BF16),
        scratch_shapes=_attn_scratch(m, tk),
        compiler_params=_params(2),
        name="gqa_attn",
    )(q, q, k, v)


def _bias_span(tq, tk):
    u_lo = -((MAX_DISTANCE + tk - 1 + tq - 1) // tq)
    u_hi = (MAX_DISTANCE + tq - 1 + tq - 1) // tq
    return u_lo, u_hi


def _t5_bucket(rel):
    nb = N_BUCKETS // 2
    max_exact = nb // 2
    ret = jnp.where(rel > 0, nb, 0)
    n = jnp.abs(rel)
    nf = jnp.maximum(n, 1).astype(F32)
    large = max_exact + (jnp.log(nf / max_exact) / math.log(MAX_DISTANCE / max_exact)
                         * (nb - max_exact)).astype(jnp.int32)
    large = jnp.minimum(large, nb - 1)
    return ret + jnp.where(n < max_exact, n, large)


def _bias_body(tab_ref, o_ref, *, tq, tk, u_lo):
    h = pl.program_id(0)
    u = pl.program_id(1) + u_lo
    rel = (u * tq + lax.broadcasted_iota(jnp.int32, (tq, tk), 1)
           - lax.broadcasted_iota(jnp.int32, (tq, tk), 0))
    bucket = _t5_bucket(rel)
    bias = jnp.zeros((tq, tk), F32)
    for b in range(N_BUCKETS):
        bias = jnp.where(bucket == b, tab_ref[b, h], bias)
    o_ref[0, 0] = bias * LOG2E


def _bias_tiles(rel_bias, tq, tk):
    u_lo, u_hi = _bias_span(tq, tk)
    n_off = u_hi - u_lo + 1
    n_heads = rel_bias.shape[1]
    return pl.pallas_call(
        functools.partial(_bias_body, tq=tq, tk=tk, u_lo=u_lo),
        grid=(n_heads, n_off),
        in_specs=[pl.BlockSpec(memory_space=pltpu.SMEM)],
        out_specs=pl.BlockSpec((1, 1, tq, tk), lambda h, u: (h, u, 0, 0)),
        out_shape=jax.ShapeDtypeStruct((n_heads, n_off, tq, tk), F32),
        compiler_params=_params(2),
        name="t5_bias",
    )(rel_bias.astype(F32))


def _near_count(tq, tk, u_lo, u_hi):
    r = tk // tq
    return max((qi + u_hi - 1) // r - (qi + u_lo) // r for qi in range(r))


def _diff_body(lam_ref, far_ref, q_ref, qn_ref, k_ref, v_ref, bias_ref, g_ref, o_ref,
               lhs_scr, m_scr, acc_scr, s0_scr, s1_scr, *, tq, tk, n_kv, n_near, u_lo, u_hi, out_scale):
    h = pl.program_id(1)
    qi = pl.program_id(2)
    cur = qi % 2
    r = tk // tq

    def stack(nxt):
        _stack_halves([(qn_ref if nxt else q_ref)[...]], lhs_scr.at[1 - cur if nxt else cur], tq)

    def first_near(q_idx):
        return jnp.clip((q_idx + u_lo) // r + 1, 0, n_kv - n_near)

    def tile_of(p, near, q_idx):
        t_a = first_near(q_idx)
        if near:
            return t_a + (p - (n_kv - n_near))
        return jnp.where(p < t_a, p, p + n_near)

    def score(p, near, nxt):
        q_idx = qi + 1 if nxt else qi
        t = tile_of(p, near, q_idx)
        s = _qk(lhs_scr[1 - cur if nxt else cur], k_ref[_key_rows(t, tk), :])
        if not near:
            return s
        u = jnp.clip(t * r - q_idx, u_lo, u_hi) - u_lo
        return (s.reshape(2, tq, tk) + bias_ref[0, u][None]).reshape(2 * tq, tk)

    def consume(s, p, near):
        t = tile_of(p, near, qi)
        v = v_ref[_key_rows(t, tk), :]
        if near:
            _consume(s, v, m_scr, acc_scr)
        else:
            _consume(s, v, m_scr, acc_scr, const=jnp.where(t < first_near(qi), far_ref[0, h], far_ref[1, h]))

    _pipelined_attention(qi, n_kv, n_near, stack, score, consume, m_scr, acc_scr, (s0_scr, s1_scr))
    o = acc_scr[:, :LANES] / acc_scr[:, LANES:]
    o = o[:tq] - lam_ref[0] * o[tq:]
    o_ref[...] = (_rms(o, g_ref[...], DIFF_SUBLN_EPS) * out_scale).astype(o_ref.dtype)


def _diff(lam, far, q, k, v, bias, subln_g, nseq, seq_len, lam_init, tq, tk):
    n_heads = q.shape[1] // LANES
    nq = seq_len // tq
    assert nq == 1 or nq % 2 == 0
    u_lo, u_hi = _bias_span(tq, tk)
    n_off = u_hi - u_lo + 1
    kv_spec = pl.BlockSpec((seq_len, LANES), lambda b, h, i: (b, h))
    n_kv = seq_len // tk
    n_near = min(_near_count(tq, tk, u_lo, u_hi), n_kv)
    return pl.pallas_call(
        functools.partial(_diff_body, tq=tq, tk=tk, n_kv=n_kv, n_near=n_near, u_lo=u_lo, u_hi=u_hi,
                          out_scale=1.0 - lam_init),
        grid=(nseq, n_heads, nq),
        in_specs=[pl.BlockSpec(memory_space=pltpu.SMEM),
                  pl.BlockSpec(memory_space=pltpu.SMEM),
                  pl.BlockSpec((tq, LANES), lambda b, h, i: (b * nq + i, h)),
                  pl.BlockSpec((tq, LANES), lambda b, h, i: (b * nq + jnp.minimum(i + 1, nq - 1), h)),
                  kv_spec, kv_spec,
                  pl.BlockSpec((1, n_off, tq, tk), lambda b, h, i: (h, 0, 0, 0),
                               pipeline_mode=pl.Buffered(1)),
                  pl.BlockSpec((1, LANES), lambda b, h, i: (0, 0))],
        out_specs=pl.BlockSpec((tq, LANES), lambda b, h, i: (b * nq + i, h)),
        out_shape=jax.ShapeDtypeStruct((nseq * seq_len, q.shape[1]), BF16),
        scratch_shapes=_attn_scratch(2 * tq, tk),
        compiler_params=_params(3),
        name="diff_attn",
    )(lam, far, q, q, k, v, bias, subln_g.reshape(1, LANES))


def _merge_body(x_ref, yh_ref, yg_ref, yd_ref, gate_ref, wb_ref, wo_ref, o_ref):
    d = x_ref.shape[1]
    merged = (gate_ref[:, 0:d].astype(F32) * _dot(yh_ref[...], wb_ref[0])
              + gate_ref[:, d:2 * d].astype(F32) * _dot(yg_ref[...], wb_ref[1])
              + gate_ref[:, 2 * d:3 * d].astype(F32) * _dot(yd_ref[...], wb_ref[2]))
    o_ref[...] = x_ref[...] + _dot(merged.astype(BF16), wo_ref[...])


def _merge(x, y_hy, y_gqa, y_diff, gates, wb, wo):
    t, d = x.shape
    db = y_hy.shape[1]
    tm = min(MERGE_TM, t)
    row = lambda i: (i, 0)
    return pl.pallas_call(
        _merge_body,
        grid=(t // tm,),
        in_specs=[pl.BlockSpec((tm, d), row),
                  pl.BlockSpec((tm, db), row), pl.BlockSpec((tm, db), row), pl.BlockSpec((tm, db), row),
                  pl.BlockSpec((tm, 3 * d), row),
                  pl.BlockSpec(wb.shape, lambda i: (0, 0, 0)),
                  pl.BlockSpec(wo.shape, lambda i: (0, 0))],
        out_specs=pl.BlockSpec((tm, d), row),
        out_shape=jax.ShapeDtypeStruct((t, d), F32),
        compiler_params=_params(1),
        name="merge",
    )(x, y_hy, y_gqa, y_diff, gates, wb, wo)


def _rope_tables(max_len):
    pos = np.arange(max_len)
    half = HEAD_DIM // 2
    inv = ROPE_THETA ** (-jnp.arange(0, half, 2, dtype=F32) / half)
    row = jnp.asarray(pos // GRID_W, F32)
    col = jnp.asarray(pos % GRID_W, F32)
    ang = jnp.concatenate([row[:, None] * inv, col[:, None] * inv], axis=-1)
    cos, sin = jnp.cos(ang), jnp.sin(ang)
    cos_h = jnp.concatenate([cos, cos], axis=1)
    sin_h = jnp.concatenate([-sin, sin], axis=1)
    return jnp.tile(cos_h, (1, LANES // HEAD_DIM)), jnp.tile(sin_h, (1, LANES // HEAD_DIM))


def _column_layout(d_model, d_hy):
    d_gqa = N_Q_HEADS * HEAD_DIM
    d_kv = N_KV_HEADS * HEAD_DIM
    d_diff = N_DIFF_HEADS * 2 * HEAD_DIM
    widths = [3 * d_hy, d_gqa, d_kv, d_kv, d_diff, d_diff, d_diff, 3 * d_model]
    offs = [0]
    for w in widths:
        offs.append(offs[-1] + w)
    deint = np.concatenate([np.arange(0, HEAD_DIM, 2), np.arange(1, HEAD_DIM, 2)])
    q_heads = [kv * GQA_GROUP + g for g in range(GQA_GROUP) for kv in range(N_KV_HEADS)]
    q_cols = np.concatenate([offs[1] + h * HEAD_DIM + deint for h in q_heads])
    k_cols = np.concatenate([offs[2] + h * HEAD_DIM + deint for h in range(N_KV_HEADS)])
    qkv_cols = np.concatenate([q_cols, k_cols, np.arange(offs[3], offs[4])])
    out_rows = np.concatenate([h * HEAD_DIM + np.arange(HEAD_DIM) for h in q_heads])
    return offs, qkv_cols, deint, out_rows


def kernel(x_prompt, x_sample, ffn1_norm, ffn1_w_in, ffn1_w_out, mix_norm, w_in, hy_conv_w, hy_conv_b, hy_filt_w1, hy_filt_b1, hy_filt_w2, hy_filt_b2, hy_filt_w3, hy_filt_freq, hy_skip, gqa_q_norm, gqa_k_norm, diff_lambda, diff_subln, rel_bias, w_branch, w_out, ffn2_norm, ffn2_w_in, ffn2_w_out, final_norm):
    d = x_prompt.shape[-1]
    depth = w_in.shape[0]
    d_hy = hy_skip.shape[1]
    groups = [(x.shape[0], x.shape[1]) for x in (x_prompt, x_sample)]
    xs = [x_prompt.reshape(-1, d), x_sample.reshape(-1, d)]
    lens = sorted({sl for _, sl in groups})
    assert all(sl % GRID_W == 0 for sl in lens)

    offs, qkv_cols, deint, out_rows = _column_layout(d, d_hy)
    cos_t, sin_t = _rope_tables(max(lens))
    bd = jnp.asarray(np.kron(np.eye(LANES // HEAD_DIM), np.full((HEAD_DIM, HEAD_DIM), 1.0 / HEAD_DIM)), BF16)
    tabs = {sl: _dft_tables(sl) for sl in lens}
    diff_tiles = {sl: (min(DIFF_TQ, sl), min(DIFF_TK, sl)) for sl in lens}
    bias = {tt: _bias_tiles(rel_bias, *tt) for tt in sorted(set(diff_tiles.values()))}
    far_bias = rel_bias[jnp.array([N_BUCKETS // 2 - 1, N_BUCKETS - 1])].astype(F32) * LOG2E

    for l in range(depth):
        w1_in, w1_out = ffn1_w_in[l].astype(BF16), ffn1_w_out[l].astype(BF16)
        w2_in, w2_out = ffn2_w_in[l].astype(BF16), ffn2_w_out[l].astype(BF16)
        w_l = w_in[l].astype(BF16)
        w_qkv = w_in[l][:, offs[1]:offs[4]][:, qkv_cols - offs[1]].astype(BF16)
        qn = jnp.tile(gqa_q_norm[l][deint], N_Q_HEADS).reshape(1, -1)
        kn = jnp.tile(gqa_k_norm[l][deint], N_KV_HEADS).reshape(1, -1)
        wb = jnp.stack([w_branch[l][0], w_branch[l][1][out_rows], w_branch[l][2]]).astype(BF16)
        wo = w_out[l].astype(BF16)
        lp = diff_lambda[l].astype(F32)
        lam_init = 0.8 - 0.6 * math.exp(-0.3 * l)
        lam = (jnp.exp(jnp.sum(lp[0] * lp[1])) - jnp.exp(jnp.sum(lp[2] * lp[3])) + lam_init).reshape(1)
        kfs = {sl: _hyena_kf(sl, tabs[sl], hy_filt_w1[l], hy_filt_b1[l], hy_filt_w2[l], hy_filt_b2[l],
                             hy_filt_w3[l], hy_filt_freq[l]) for sl in lens}
        final_g = final_norm if l == depth - 1 else None

        for g, (nseq, sl) in enumerate(groups):
            x = _ffn(xs[g], ffn1_norm[l], w1_in, w1_out)
            hy, gq, gk, gv, dq, dk, dv, gates = _inproj(x, mix_norm[l], w_l, w_qkv, cos_t, sin_t, bd, qn, kn,
                                                         offs, sl)
            z, x0 = _shortconv(hy, hy_conv_w[l], hy_conv_b[l], sl)
            y_hy = _hyena_longconv(z, x0, kfs[sl], hy_skip[l], nseq, sl, tabs[sl])
            y_gqa = _gqa(gq, gk, gv, nseq, sl)
            tq, tk = diff_tiles[sl]
            y_diff = _diff(lam, far_bias, dq, dk, dv, bias[(tq, tk)], diff_subln[l], nseq, sl, lam_init, tq, tk)
            x = _merge(x, y_hy, y_gqa, y_diff, gates, wb, wo)
            xs[g] = _ffn(x, ffn2_norm[l], w2_in, w2_out, final_g=final_g)

    return xs[0].reshape(x_prompt.shape), xs[1].reshape(x_sample.shape)
```

```python
import functools
import math

import numpy as np
import jax
import jax.numpy as jnp
from jax import lax
from jax.experimental import pallas as pl
from jax.experimental.pallas import tpu as pltpu

F32 = jnp.float32
BF16 = jnp.bfloat16

NORM_EPS = 1e-6
GRID_W = 64
HEAD_DIM = 64
N_Q_HEADS = 8
N_KV_HEADS = 2
GQA_GROUP = N_Q_HEADS // N_KV_HEADS
ROPE_THETA = 10000.0
N_DIFF_HEADS = 4
DIFF_SUBLN_EPS = 1e-5
N_BUCKETS = 32
MAX_DISTANCE = 128
POS_BANDS = 16
DECAY_TARGET = 1e-2
FAST_DECAY_PCT = 0.3
SLOW_DECAY_PCT = 1.5

LANES = 128
SUBLANES = 8
MXU_WIDTH = 256
VMEM_LIMIT_BYTES = 56 * 1024 * 1024

FFN_TM = 512
PROJ_TM = 512
MERGE_TM = 512
CONV_TM = 512
GQA_TQ = 128
GQA_TK = 1024
DIFF_TQ = 512
DIFF_TK = 1024
FFT_CC = 8
CONV_CHAINS = 4
KF_CHAINS = 2
FILT_TM = 512

NEG_BIG = -1e30
LOG2E = math.log2(math.e)
KV_TILES_PER_ITER = 8


def _params(n_axes):
    return pltpu.CompilerParams(dimension_semantics=("arbitrary",) * n_axes,
                                vmem_limit_bytes=VMEM_LIMIT_BYTES)


def _dot(a, b):
    return jnp.dot(a, b, preferred_element_type=F32)


def _split_bf16(x):
    hi = x.astype(BF16)
    return hi, (x - hi.astype(F32)).astype(BF16)


def _dot3(a, b):
    ah, al = _split_bf16(a)
    bh, bl = _split_bf16(b)
    return _dot(ah, bh) + (_dot(ah, bl) + _dot(al, bh))


def _rms(x, g, eps):
    ms = jnp.mean(x * x, axis=-1, keepdims=True)
    return x * lax.rsqrt(ms + eps) * g


def _ffn_body(*refs, chunks, final):
    if final:
        x_ref, g_ref, w_in_ref, w_out_ref, gf_ref, o_ref = refs
    else:
        x_ref, g_ref, w_in_ref, w_out_ref, o_ref = refs
    d_ff = w_out_ref.shape[0]
    x = x_ref[...]
    h = _rms(x, g_ref[...], NORM_EPS).astype(BF16)
    acc = None
    for a, b in chunks:
        gate = _dot(h, w_in_ref[:, a:b])
        up = _dot(h, w_in_ref[:, d_ff + a:d_ff + b])
        part = _dot((gate * jax.nn.sigmoid(gate) * up).astype(BF16), w_out_ref[a:b, :])
        acc = part if acc is None else acc + part
    y = x + 0.5 * acc
    if final:
        y = _rms(y, gf_ref[...], NORM_EPS)
    o_ref[...] = y


def _ffn_chunks(d_ff):
    n_blk = d_ff // MXU_WIDTH
    if d_ff % MXU_WIDTH or n_blk < 2:
        return ((0, d_ff),)
    cut = (n_blk + 1) // 2 * MXU_WIDTH
    return ((0, cut), (cut, d_ff))


def _ffn(x, norm_g, w_in, w_out, final_g=None):
    t, d = x.shape
    d_ff = w_out.shape[0]
    tm = min(FFN_TM, t)
    final = final_g is not None
    const = lambda i: (0, 0)
    in_specs = [
        pl.BlockSpec((tm, d), lambda i: (i, 0)),
        pl.BlockSpec((1, d), const),
        pl.BlockSpec(w_in.shape, const, pipeline_mode=pl.Buffered(1)),
        pl.BlockSpec(w_out.shape, const, pipeline_mode=pl.Buffered(1)),
    ]
    args = [x, norm_g.reshape(1, d), w_in, w_out]
    if final:
        in_specs.append(pl.BlockSpec((1, d), const))
        args.append(final_g.reshape(1, d))
    return pl.pallas_call(
        functools.partial(_ffn_body, chunks=_ffn_chunks(d_ff), final=final),
        grid=(t // tm,),
        in_specs=in_specs,
        out_specs=pl.BlockSpec((tm, d), lambda i: (i, 0)),
        out_shape=jax.ShapeDtypeStruct((t, d), F32),
        compiler_params=_params(1),
        name="ffn",
    )(*args)


def _head_rms(x, bd):
    x2 = x * x
    hi = x2.astype(BF16)
    lo = (x2 - hi.astype(F32)).astype(BF16)
    ms = _dot(hi, bd) + _dot(lo, bd)
    return x * lax.rsqrt(ms + NORM_EPS)


def _rope(x, cos, sin_signed):
    lane = lax.broadcasted_iota(jnp.int32, x.shape, 1)
    partner = jnp.where((lane % HEAD_DIM) < HEAD_DIM // 2,
                        pltpu.roll(x, LANES - HEAD_DIM // 2, 1),
                        pltpu.roll(x, HEAD_DIM // 2, 1))
    return x * cos + partner * sin_signed


def _inproj_body(x_ref, g_ref, w_ref, wqkv_ref, cos_ref, sin_ref, bd_ref, qn_ref, kn_ref,
                 hy_ref, gq_ref, gk_ref, gv_ref, dq_ref, dk_ref, dv_ref, gate_ref, *, offs):
    o_hy, o_gq, o_gk, o_gv, o_dq, o_dk, o_dv, o_gate, o_end = offs
    h = _rms(x_ref[...], g_ref[...], NORM_EPS).astype(BF16)

    def seg(a, b):
        return _dot(h, w_ref[:, a:b])

    hy_ref[...] = seg(o_hy, o_gq)
    cos = cos_ref[...]
    sin = sin_ref[...]
    bd = bd_ref[...]
    scale = HEAD_DIM ** -0.5 * LOG2E
    qkv = _dot(h, wqkv_ref[...])
    wq, wk = o_gk - o_gq, o_gv - o_gk
    for c in range(wq // LANES):
        y = _head_rms(qkv[:, c * LANES:(c + 1) * LANES], bd) * qn_ref[:, c * LANES:(c + 1) * LANES]
        gq_ref[:, c * LANES:(c + 1) * LANES] = (_rope(y, cos, sin) * scale).astype(BF16)
    for c in range(wk // LANES):
        y = _head_rms(qkv[:, wq + c * LANES:wq + (c + 1) * LANES], bd) * kn_ref[:, c * LANES:(c + 1) * LANES]
        gk_ref[:, c * LANES:(c + 1) * LANES] = _rope(y, cos, sin).astype(BF16)
    gv_ref[...] = qkv[:, wq + wk:].astype(BF16)
    dq_ref[...] = (seg(o_dq, o_dk) * scale).astype(BF16)
    dk_ref[...] = seg(o_dk, o_dv).astype(BF16)
    dv_ref[...] = seg(o_dv, o_gate).astype(BF16)
    d = x_ref.shape[1]
    for c in range((o_end - o_gate) // d):
        gate_ref[:, c * d:(c + 1) * d] = jax.nn.sigmoid(seg(o_gate + c * d, o_gate + (c + 1) * d)).astype(BF16)


def _inproj(x, norm_g, w, wqkv, cos_t, sin_t, bd, qn, kn, offs, seq_len):
    t, d = x.shape
    tm = min(PROJ_TM, seq_len)
    widths = [offs[i + 1] - offs[i] for i in range(8)]

    def pos_map(i):
        return (i % (seq_len // tm), 0)

    full = lambda i: (0, 0)
    row = lambda i: (i, 0)
    out_dtypes = [F32] + [BF16] * 7
    return pl.pallas_call(
        functools.partial(_inproj_body, offs=tuple(offs)),
        grid=(t // tm,),
        in_specs=[
            pl.BlockSpec((tm, d), row),
            pl.BlockSpec((1, d), full),
            pl.BlockSpec(w.shape, full, pipeline_mode=pl.Buffered(1)),
            pl.BlockSpec(wqkv.shape, full, pipeline_mode=pl.Buffered(1)),
            pl.BlockSpec((tm, LANES), pos_map),
            pl.BlockSpec((tm, LANES), pos_map),
            pl.BlockSpec(bd.shape, full),
            pl.BlockSpec(qn.shape, full),
            pl.BlockSpec(kn.shape, full),
        ],
        out_specs=[pl.BlockSpec((tm, wd), row) for wd in widths],
        out_shape=[jax.ShapeDtypeStruct((t, wd), dt) for wd, dt in zip(widths, out_dtypes)],
        compiler_params=_params(1),
        name="inproj",
    )(x, norm_g.reshape(1, d), w, wqkv, cos_t, sin_t, bd, qn, kn)


def _shortconv_body(u_ref, prev_ref, next_ref, w_ref, b_ref, z_ref, x0_ref, *, tm, seq_len, d_hy):
    pos = (pl.program_id(0) * tm) % seq_len
    u = u_ref[...]
    rows = lax.broadcasted_iota(jnp.int32, u.shape, 0)
    before = jnp.where(pos == 0, 0.0, prev_ref[SUBLANES - 1:SUBLANES, :])
    after = jnp.where(pos + tm == seq_len, 0.0, next_ref[0:1, :])
    up = jnp.where(rows == 0, before, pltpu.roll(u, 1, 0))
    dn = jnp.where(rows == tm - 1, after, pltpu.roll(u, tm - 1, 0))
    y = up * w_ref[0:1, :] + u * w_ref[1:2, :] + dn * w_ref[2:3, :] + b_ref[...]
    x0_ref[...] = y[:, :d_hy]
    z_ref[...] = y[:, 2 * d_hy:] * y[:, d_hy:2 * d_hy]


def _shortconv(u, w, b, seq_len):
    t, c3 = u.shape
    d_hy = c3 // 3
    tm = min(CONV_TM, seq_len)
    nb8 = t // SUBLANES
    return pl.pallas_call(
        functools.partial(_shortconv_body, tm=tm, seq_len=seq_len, d_hy=d_hy),
        grid=(t // tm,),
        in_specs=[
            pl.BlockSpec((tm, c3), lambda i: (i, 0)),
            pl.BlockSpec((SUBLANES, c3), lambda i: (jnp.maximum(i * (tm // SUBLANES) - 1, 0), 0)),
            pl.BlockSpec((SUBLANES, c3), lambda i: (jnp.minimum((i + 1) * (tm // SUBLANES), nb8 - 1), 0)),
            pl.BlockSpec((3, c3), lambda i: (0, 0)),
            pl.BlockSpec((1, c3), lambda i: (0, 0)),
        ],
        out_specs=[pl.BlockSpec((tm, d_hy), lambda i: (i, 0))] * 2,
        out_shape=[jax.ShapeDtypeStruct((t, d_hy), F32)] * 2,
        compiler_params=_params(1),
        name="shortconv",
    )(u, u, u, w, b.reshape(1, c3))


def _filter_body(f_ref, w1_ref, b1_ref, w2_ref, b2_ref, w3_ref, fr_ref, dl_ref, o_ref, *, n_feat):
    half = f_ref.shape[0] // 2
    d_hy = o_ref.shape[1]
    halves = (f_ref[:half, :], f_ref[half:, :])
    fr = fr_ref[...]
    h = jnp.sin(fr * (_dot3(jnp.concatenate(halves, axis=1), w1_ref[...]) + b1_ref[...]))
    h = jnp.sin(fr * (_dot3(h, w2_ref[...]) + b2_ref[...]))
    h = _dot3(h, w3_ref[0])
    for k, f in enumerate(halves):
        decay = jnp.exp(-f[:, 0:1] * dl_ref[...])
        o_ref[k * half:(k + 1) * half, :] = h[:, k * d_hy:(k + 1) * d_hy] * decay * f[:, n_feat:n_feat + 1]


def _hyena_filter(feats, w1p, b1, w2, b2, w3, freq, deltas_abs, seq_len, n_feat):
    n2l, fw = feats.shape
    d_hy = w3.shape[1] // 2
    hid = w2.shape[0]
    tm = min(FILT_TM, seq_len)
    nbl = seq_len // tm
    full = lambda i: (0, 0)
    twice = lambda m: jnp.kron(jnp.eye(2, dtype=m.dtype), m)
    pair = lambda v: jnp.tile(v, 2).reshape(1, -1)
    return pl.pallas_call(
        functools.partial(_filter_body, n_feat=n_feat),
        grid=(n2l // tm,),
        in_specs=[
            pl.BlockSpec((tm, fw), lambda i: (i, 0)),
            pl.BlockSpec((2 * fw, 2 * hid), full),
            pl.BlockSpec((1, 2 * hid), full),
            pl.BlockSpec((2 * hid, 2 * hid), full),
            pl.BlockSpec((1, 2 * hid), full),
            pl.BlockSpec((1, 2 * hid, 2 * d_hy), lambda i: (i // nbl, 0, 0)),
            pl.BlockSpec((1, 2 * hid), full),
            pl.BlockSpec((1, d_hy), full),
        ],
        out_specs=pl.BlockSpec((tm, d_hy), lambda i: (i, 0)),
        out_shape=jax.ShapeDtypeStruct((n2l, d_hy), F32),
        compiler_params=_params(1),
        name="hyena_filter",
    )(feats, twice(w1p), pair(b1), twice(w2), pair(b2),
      jnp.stack([twice(w3[:, :d_hy]), twice(w3[:, d_hy:])]), pair(freq), deltas_abs.reshape(1, d_hy))


def _fwd_spectra(f1, chunks, twr, twi, w2f, cc):
    n1 = twr.shape[0]
    a = [_dot3(f1, x) for x in chunks]
    b = []
    for aj in a:
        rows = []
        for c in range(cc):
            ar = aj[:n1, c * LANES:(c + 1) * LANES]
            ai = aj[n1:, c * LANES:(c + 1) * LANES]
            rows.append(jnp.concatenate([ar * twr - ai * twi, ar * twi + ai * twr], axis=1))
        b.append(jnp.concatenate(rows, axis=0))
    return [_dot3(bj, w2f) for bj in b]


def _kf_body(f1_ref, x_ref, twr_ref, twi_ref, w2f_ref, o_ref, *, cc, n_sub, scale):
    n1 = twr_ref.shape[0]
    w = cc * LANES
    xs = _fwd_spectra(f1_ref[...], [x_ref[:, j * w:(j + 1) * w] for j in range(n_sub)],
                      twr_ref[...], twi_ref[...], w2f_ref[...], cc)
    for j, xj in enumerate(xs):
        for c in range(cc):
            lanes = slice(j * w + c * LANES, j * w + (c + 1) * LANES)
            o_ref[0, :, lanes] = xj[c * n1:(c + 1) * n1, :LANES] * scale
            o_ref[1, :, lanes] = xj[c * n1:(c + 1) * n1, LANES:] * scale


def _conv_body(f1_ref, finv_ref, z_ref, x0_ref, kf_ref, skip_ref, twr_ref, twi_ref, w2f_ref, w2i_ref,
               o_ref, *, cc, n_sub):
    n1 = twr_ref.shape[0]
    w = cc * LANES
    twr, twi = twr_ref[...], twi_ref[...]
    cols = [slice(j * w, (j + 1) * w) for j in range(n_sub)]
    zs = [z_ref[0, :, c] for c in cols]
    xs = _fwd_spectra(f1_ref[...], zs, twr, twi, w2f_ref[...], cc)
    ys = []
    for xj, cj in zip(xs, cols):
        xr, xi = xj[:, :LANES], xj[:, LANES:]
        kr = jnp.concatenate([kf_ref[0, :, cj][:, c * LANES:(c + 1) * LANES] for c in range(cc)], axis=0)
        ki = jnp.concatenate([kf_ref[1, :, cj][:, c * LANES:(c + 1) * LANES] for c in range(cc)], axis=0)
        ys.append(jnp.concatenate([xr * kr - xi * ki, xr * ki + xi * kr], axis=1))
    w2i = w2i_ref[...]
    cs = [_dot3(yj, w2i) for yj in ys]
    ds = []
    for cj in cs:
        d_re, d_im = [], []
        for c in range(cc):
            cr = cj[c * n1:(c + 1) * n1, :LANES]
            ci = cj[c * n1:(c + 1) * n1, LANES:]
            d_re.append(cr * twr + ci * twi)
            d_im.append(ci * twr - cr * twi)
        ds.append(jnp.concatenate([jnp.concatenate(d_re, axis=1), jnp.concatenate(d_im, axis=1)], axis=0))
    finv = finv_ref[...]
    convs = [_dot3(finv, dj) for dj in ds]
    for conv, z, cj in zip(convs, zs, cols):
        o_ref[0, :, cj] = (x0_ref[0, :, cj] * (conv + z * skip_ref[:, cj])).astype(o_ref.dtype)


def _dft_tables(seq_len):
    n = 2 * seq_len
    n1 = n // LANES
    n1h = n1 // 2
    k1 = np.arange(n1)[:, None]
    ang1 = 2.0 * np.pi * k1 * np.arange(n1)[None, :] / n1
    f1 = np.concatenate([np.cos(ang1), -np.sin(ang1)], axis=0)
    finv = np.concatenate([np.cos(ang1[:n1h]), -np.sin(ang1[:n1h])], axis=1)
    ang2 = 2.0 * np.pi * np.arange(LANES)[:, None] * np.arange(LANES)[None, :] / LANES
    f2r, f2i = np.cos(ang2), -np.sin(ang2)
    w2f = np.block([[f2r, f2i], [-f2i, f2r]])
    w2i = np.block([[f2r, -f2i], [f2i, f2r]])
    angt = 2.0 * np.pi * k1 * np.arange(LANES)[None, :] / n
    f = lambda a: jnp.asarray(a, F32)
    return dict(n=n, n1=n1, n1h=n1h, f1_full=f(f1), f1_half=f(f1[:, :n1h]), finv=f(finv),
                w2f=f(w2f), w2i=f(w2i), twr=f(np.cos(angt)), twi=f(-np.sin(angt)))


def _to_blocked(x, nseq):
    t, c = x.shape
    r = t // (nseq * LANES)
    return x.reshape(nseq, r, LANES, c).transpose(0, 1, 3, 2).reshape(nseq, r, c * LANES)


def _from_blocked(y, c):
    nseq, r, _ = y.shape
    return y.reshape(nseq, r, c, LANES).transpose(0, 1, 3, 2).reshape(nseq * r * LANES, c)


def _filter_features(seq_len, fw):
    t = jnp.linspace(0.0, 1.0, seq_len, dtype=F32)[:, None]
    band = jnp.linspace(1e-4, POS_BANDS - 1, POS_BANDS, dtype=F32)
    ang = (2.0 * math.pi / seq_len) * jnp.arange(seq_len, dtype=F32)[:, None] * band[None, :]
    feats = jnp.concatenate([t, jnp.cos(ang), -jnp.sin(ang)], axis=-1)
    n_feat = feats.shape[1]
    rows = jnp.concatenate([feats, feats[-1:], feats[:0:-1]], axis=0)
    valid = jnp.asarray((np.arange(2 * seq_len) != seq_len).astype(np.float32))[:, None]
    rows = jnp.concatenate([rows, valid], axis=1)
    return jnp.pad(rows, ((0, 0), (0, fw - n_feat - 1))), n_feat


def _hyena_kf(seq_len, tabs, w1, b1, w2, b2, w3, freq):
    d_hy = w3.shape[1] // 2
    feats, n_feat = _filter_features(seq_len, LANES)
    w1p = jnp.pad(w1, ((0, LANES - w1.shape[0]), (0, 0)))
    max_decay = math.log(DECAY_TARGET) / FAST_DECAY_PCT
    min_decay = math.log(DECAY_TARGET) / SLOW_DECAY_PCT
    deltas = jnp.abs(jnp.linspace(min_decay, max_decay, d_hy, dtype=F32))
    kern = _hyena_filter(feats, w1p, b1, w2, b2, w3, freq, deltas, seq_len, n_feat)
    n, n1 = tabs["n"], tabs["n1"]
    cc = FFT_CC
    n_sub = KF_CHAINS
    w = n_sub * cc * LANES
    full = lambda j: (0, 0)
    return pl.pallas_call(
        functools.partial(_kf_body, cc=cc, n_sub=n_sub, scale=1.0 / n),
        grid=(d_hy // (n_sub * cc),),
        in_specs=[pl.BlockSpec((2 * n1, n1), full),
                  pl.BlockSpec((n1, w), lambda j: (0, j)),
                  pl.BlockSpec((n1, LANES), full), pl.BlockSpec((n1, LANES), full),
                  pl.BlockSpec((2 * LANES, 2 * LANES), full)],
        out_specs=pl.BlockSpec((2, n1, w), lambda j: (0, 0, j)),
        out_shape=jax.ShapeDtypeStruct((2, n1, d_hy * LANES), F32),
        compiler_params=_params(1),
        name="hyena_kf",
    )(tabs["f1_full"], _to_blocked(kern, 1)[0], tabs["twr"], tabs["twi"], tabs["w2f"])


def _hyena_longconv(z, x0, kf, skip, nseq, seq_len, tabs):
    c = z.shape[1]
    n1, n1h = tabs["n1"], tabs["n1h"]
    cc = FFT_CC
    n_sub = CONV_CHAINS
    w = n_sub * cc * LANES
    full = lambda b, j: (0, 0)
    rowblk = pl.BlockSpec((1, n1h, w), lambda b, j: (b, 0, j))
    y = pl.pallas_call(
        functools.partial(_conv_body, cc=cc, n_sub=n_sub),
        grid=(nseq, c // (n_sub * cc)),
        in_specs=[pl.BlockSpec((2 * n1, n1h), full),
                  pl.BlockSpec((n1h, 2 * n1), full),
                  rowblk, rowblk,
                  pl.BlockSpec((2, n1, w), lambda b, j: (0, 0, j)),
                  pl.BlockSpec((1, w), lambda b, j: (0, j)),
                  pl.BlockSpec((n1, LANES), full), pl.BlockSpec((n1, LANES), full),
                  pl.BlockSpec((2 * LANES, 2 * LANES), full), pl.BlockSpec((2 * LANES, 2 * LANES), full)],
        out_specs=rowblk,
        out_shape=jax.ShapeDtypeStruct((nseq, n1h, c * LANES), BF16),
        compiler_params=_params(2),
        name="hyena_conv",
    )(tabs["f1_half"], tabs["finv"], _to_blocked(z, nseq), _to_blocked(x0, nseq), kf,
      jnp.repeat(skip.astype(F32), LANES).reshape(1, c * LANES),
      tabs["twr"], tabs["twi"], tabs["w2f"], tabs["w2i"])
    return _from_blocked(y, c)


def _stack_halves(q_cols, lhs_ref, tq):
    lane = lax.broadcasted_iota(jnp.int32, (tq, LANES), 1)
    low = lane < HEAD_DIM
    zero = jnp.zeros((tq, LANES), lhs_ref.dtype)
    for c, q in enumerate(q_cols):
        lhs_ref[(2 * c) * tq:(2 * c + 1) * tq, :] = jnp.where(low, q, zero)
        lhs_ref[(2 * c + 1) * tq:(2 * c + 2) * tq, :] = jnp.where(low, zero, q)


def _consume(s, v, m_scr, acc_scr, const=None):
    tk = s.shape[1]
    m_prev = m_scr[...]
    row_max = jnp.max(s, axis=1, keepdims=True)
    if const is not None:
        row_max = row_max + const
    m_new = jnp.maximum(m_prev, row_max)
    shift = m_new if const is None else m_new - const
    p = jnp.exp2(s - jnp.concatenate([shift] * (tk // LANES), axis=1)).astype(BF16)
    alpha = jnp.exp2(m_prev - m_new)
    v_ext = jnp.concatenate([v, jnp.ones_like(v)], axis=1)
    acc_scr[...] = jnp.concatenate([alpha, alpha], axis=1) * acc_scr[...] + _dot(p, v_ext)
    m_scr[...] = m_new


def _qk(lhs, k):
    return lax.dot_general(lhs, k, (((1,), (1,)), ((), ())), preferred_element_type=F32)


def _key_rows(t, tk):
    return pl.ds(t * tk if isinstance(t, int) else pl.multiple_of(t * tk, tk), tk)


def _pipelined_attention(qi, n_kv, n_special, stack_fn, score_fn, consume_fn, m_scr, acc_scr, s_scr):
    @pl.when(qi == 0)
    def _():
        stack_fn(False)
        s_scr[0][...] = score_fn(0, n_special == n_kv, False)

    m_scr[...] = jnp.full_like(m_scr, NEG_BIG)
    acc_scr[...] = jnp.zeros_like(acc_scr)
    unroll = min(KV_TILES_PER_ITER, n_kv)
    assert n_kv % unroll == 0 and n_special <= unroll

    def run(p0, last):
        for j in range(unroll):
            special = last and j >= unroll - n_special
            if last and j == unroll - 1:
                stack_fn(True)
                s_scr[(j + 1) % 2][...] = score_fn(0, n_special == n_kv, True)
            else:
                s_scr[(j + 1) % 2][...] = score_fn(p0 + j + 1, last and j + 1 >= unroll - n_special, False)
            consume_fn(s_scr[j % 2][...], p0 + j, special)

    def body(i, carry):
        run(i * unroll, False)
        return carry

    lax.fori_loop(0, n_kv // unroll - 1, body, 0)
    run(n_kv - unroll, True)
    if unroll % 2 == 1:
        s_scr[0][...] = s_scr[1][...]


def _attn_scratch(m, tk):
    return [pltpu.VMEM((2, m, LANES), BF16), pltpu.VMEM((m, LANES), F32), pltpu.VMEM((m, 2 * LANES), F32),
            pltpu.VMEM((m, tk), F32), pltpu.VMEM((m, tk), F32)]


def _gqa_body(q_ref, qn_ref, k_ref, v_ref, o_ref, lhs_scr, m_scr, acc_scr, s0_scr, s1_scr, *, tq, tk, n_kv):
    qi = pl.program_id(1)
    cur = qi % 2
    n_col = q_ref.shape[1] // LANES

    def stack(nxt):
        ref = qn_ref if nxt else q_ref
        slot = 1 - cur if nxt else cur
        _stack_halves([ref[:, c * LANES:(c + 1) * LANES] for c in range(n_col)], lhs_scr.at[slot], tq)

    def score(t, special, nxt):
        return _qk(lhs_scr[1 - cur if nxt else cur], k_ref[_key_rows(t, tk), :])

    def consume(s, t, special):
        _consume(s, v_ref[_key_rows(t, tk), :], m_scr, acc_scr)

    _pipelined_attention(qi, n_kv, 0, stack, score, consume, m_scr, acc_scr, (s0_scr, s1_scr))
    o = acc_scr[:, :LANES] / acc_scr[:, LANES:]
    low = lax.broadcasted_iota(jnp.int32, (tq, LANES), 1) < HEAD_DIM
    for c in range(n_col):
        o_ref[:, c * LANES:(c + 1) * LANES] = jnp.where(
            low, o[(2 * c) * tq:(2 * c + 1) * tq], o[(2 * c + 1) * tq:(2 * c + 2) * tq]
        ).astype(o_ref.dtype)


def _gqa(q, k, v, nseq, seq_len):
    dq = q.shape[1]
    tq = min(GQA_TQ, seq_len)
    tk = min(GQA_TK, seq_len)
    nq = seq_len // tq
    assert nq == 1 or nq % 2 == 0
    m = 2 * (dq // LANES) * tq
    return pl.pallas_call(
        functools.partial(_gqa_body, tq=tq, tk=tk, n_kv=seq_len // tk),
        grid=(nseq, nq),
        in_specs=[pl.BlockSpec((tq, dq), lambda b, i: (b * nq + i, 0)),
                  pl.BlockSpec((tq, dq), lambda b, i: (b * nq + jnp.minimum(i + 1, nq - 1), 0)),
                  pl.BlockSpec((seq_len, LANES), lambda b, i: (b, 0)),
                  pl.BlockSpec((seq_len, LANES), lambda b, i: (b, 0))],
        out_specs=pl.BlockSpec((tq, dq), lambda b, i: (b * nq + i, 0)),
        out_shape=jax.ShapeDtypeStruct((nseq * seq_len, dq), BF16),
        scratch_shapes=_attn_scratch(m, tk),
        compiler_params=_params(2),
        name="gqa_attn",
    )(q, q, k, v)


def _bias_span(tq, tk):
    u_lo = -((MAX_DISTANCE + tk - 1 + tq - 1) // tq)
    u_hi = (MAX_DISTANCE + tq - 1 + tq - 1) // tq
    return u_lo, u_hi


def _t5_bucket(rel):
    nb = N_BUCKETS // 2
    max_exact = nb // 2
    ret = jnp.where(rel > 0, nb, 0)
    n = jnp.abs(rel)
    nf = jnp.maximum(n, 1).astype(F32)
    large = max_exact + (jnp.log(nf / max_exact) / math.log(MAX_DISTANCE / max_exact)
                         * (nb - max_exact)).astype(jnp.int32)
    large = jnp.minimum(large, nb - 1)
    return ret + jnp.where(n < max_exact, n, large)


def _bias_body(tab_ref, o_ref, *, tq, tk, u_lo, width):
    h = pl.program_id(0)
    u = pl.program_id(1) + u_lo
    rel = u * tq - (tq - 1) + lax.broadcasted_iota(jnp.int32, (SUBLANES, width), 1)
    bucket = _t5_bucket(rel)
    profile = jnp.zeros((SUBLANES, width), F32)
    for b in range(N_BUCKETS):
        profile = jnp.where(bucket == b, tab_ref[b, h], profile)
    rows = jnp.broadcast_to(profile[0:1, :], (tq, width))
    tile = pltpu.roll(rows, width - (tq - 1), 1, stride=1, stride_axis=0)
    o_ref[0, 0] = tile[:, :tk] * LOG2E


def _bias_tiles(rel_bias, tq, tk):
    u_lo, u_hi = _bias_span(tq, tk)
    n_off = u_hi - u_lo + 1
    n_heads = rel_bias.shape[1]
    width = pl.next_power_of_2(tq + tk - 1)
    return pl.pallas_call(
        functools.partial(_bias_body, tq=tq, tk=tk, u_lo=u_lo, width=width),
        grid=(n_heads, n_off),
        in_specs=[pl.BlockSpec(memory_space=pltpu.SMEM)],
        out_specs=pl.BlockSpec((1, 1, tq, tk), lambda h, u: (h, u, 0, 0)),
        out_shape=jax.ShapeDtypeStruct((n_heads, n_off, tq, tk), F32),
        compiler_params=_params(2),
        name="t5_bias",
    )(rel_bias.astype(F32))


def _near_count(tq, tk, u_lo, u_hi):
    r = tk // tq
    return max((qi + u_hi - 1) // r - (qi + u_lo) // r for qi in range(r))


def _diff_body(lam_ref, far_ref, q_ref, qn_ref, k_ref, v_ref, bias_ref, g_ref, o_ref,
               lhs_scr, m_scr, acc_scr, s0_scr, s1_scr, *, tq, tk, n_kv, n_near, u_lo, u_hi, out_scale):
    h = pl.program_id(1)
    qi = pl.program_id(2)
    cur = qi % 2
    r = tk // tq

    def stack(nxt):
        _stack_halves([(qn_ref if nxt else q_ref)[...]], lhs_scr.at[1 - cur if nxt else cur], tq)

    def first_near(q_idx):
        return jnp.clip((q_idx + u_lo) // r + 1, 0, n_kv - n_near)

    def tile_of(p, near, q_idx):
        t_a = first_near(q_idx)
        if near:
            return t_a + (p - (n_kv - n_near))
        return jnp.where(p < t_a, p, p + n_near)

    def score(p, near, nxt):
        q_idx = qi + 1 if nxt else qi
        t = tile_of(p, near, q_idx)
        s = _qk(lhs_scr[1 - cur if nxt else cur], k_ref[_key_rows(t, tk), :])
        if not near:
            return s
        u = jnp.clip(t * r - q_idx, u_lo, u_hi) - u_lo
        return (s.reshape(2, tq, tk) + bias_ref[0, u][None]).reshape(2 * tq, tk)

    def consume(s, p, near):
        t = tile_of(p, near, qi)
        v = v_ref[_key_rows(t, tk), :]
        if near:
            _consume(s, v, m_scr, acc_scr)
        else:
            _consume(s, v, m_scr, acc_scr, const=jnp.where(t < first_near(qi), far_ref[0, h], far_ref[1, h]))

    _pipelined_attention(qi, n_kv, n_near, stack, score, consume, m_scr, acc_scr, (s0_scr, s1_scr))
    o = acc_scr[:, :LANES] / acc_scr[:, LANES:]
    o = o[:tq] - lam_ref[0] * o[tq:]
    o_ref[...] = (_rms(o, g_ref[...], DIFF_SUBLN_EPS) * out_scale).astype(o_ref.dtype)


def _diff(lam, far, q, k, v, bias, subln_g, nseq, seq_len, lam_init, tq, tk):
    n_heads = q.shape[1] // LANES
    nq = seq_len // tq
    assert nq == 1 or nq % 2 == 0
    u_lo, u_hi = _bias_span(tq, tk)
    n_off = u_hi - u_lo + 1
    kv_spec = pl.BlockSpec((seq_len, LANES), lambda b, h, i: (b, h))
    n_kv = seq_len // tk
    n_near = min(_near_count(tq, tk, u_lo, u_hi), n_kv)
    return pl.pallas_call(
        functools.partial(_diff_body, tq=tq, tk=tk, n_kv=n_kv, n_near=n_near, u_lo=u_lo, u_hi=u_hi,
                          out_scale=1.0 - lam_init),
        grid=(nseq, n_heads, nq),
        in_specs=[pl.BlockSpec(memory_space=pltpu.SMEM),
                  pl.BlockSpec(memory_space=pltpu.SMEM),
                  pl.BlockSpec((tq, LANES), lambda b, h, i: (b * nq + i, h)),
                  pl.BlockSpec((tq, LANES), lambda b, h, i: (b * nq + jnp.minimum(i + 1, nq - 1), h)),
                  kv_spec, kv_spec,
                  pl.BlockSpec((1, n_off, tq, tk), lambda b, h, i: (h, 0, 0, 0),
                               pipeline_mode=pl.Buffered(1)),
                  pl.BlockSpec((1, LANES), lambda b, h, i: (0, 0))],
        out_specs=pl.BlockSpec((tq, LANES), lambda b, h, i: (b * nq + i, h)),
        out_shape=jax.ShapeDtypeStruct((nseq * seq_len, q.shape[1]), BF16),
        scratch_shapes=_attn_scratch(2 * tq, tk),
        compiler_params=_params(3),
        name="diff_attn",
    )(lam, far, q, q, k, v, bias, subln_g.reshape(1, LANES))


def _merge_body(x_ref, yh_ref, yg_ref, yd_ref, gate_ref, wb_ref, wo_ref, o_ref):
    d = x_ref.shape[1]
    merged = (gate_ref[:, 0:d].astype(F32) * _dot(yh_ref[...], wb_ref[0])
              + gate_ref[:, d:2 * d].astype(F32) * _dot(yg_ref[...], wb_ref[1])
              + gate_ref[:, 2 * d:3 * d].astype(F32) * _dot(yd_ref[...], wb_ref[2]))
    o_ref[...] = x_ref[...] + _dot(merged.astype(BF16), wo_ref[...])


def _merge(x, y_hy, y_gqa, y_diff, gates, wb, wo):
    t, d = x.shape
    db = y_hy.shape[1]
    tm = min(MERGE_TM, t)
    row = lambda i: (i, 0)
    return pl.pallas_call(
        _merge_body,
        grid=(t // tm,),
        in_specs=[pl.BlockSpec((tm, d), row),
                  pl.BlockSpec((tm, db), row), pl.BlockSpec((tm, db), row), pl.BlockSpec((tm, db), row),
                  pl.BlockSpec((tm, 3 * d), row),
                  pl.BlockSpec(wb.shape, lambda i: (0, 0, 0)),
                  pl.BlockSpec(wo.shape, lambda i: (0, 0))],
        out_specs=pl.BlockSpec((tm, d), row),
        out_shape=jax.ShapeDtypeStruct((t, d), F32),
        compiler_params=_params(1),
        name="merge",
    )(x, y_hy, y_gqa, y_diff, gates, wb, wo)


def _rope_tables(max_len):
    pos = np.arange(max_len)
    half = HEAD_DIM // 2
    inv = ROPE_THETA ** (-jnp.arange(0, half, 2, dtype=F32) / half)
    row = jnp.asarray(pos // GRID_W, F32)
    col = jnp.asarray(pos % GRID_W, F32)
    ang = jnp.concatenate([row[:, None] * inv, col[:, None] * inv], axis=-1)
    cos, sin = jnp.cos(ang), jnp.sin(ang)
    cos_h = jnp.concatenate([cos, cos], axis=1)
    sin_h = jnp.concatenate([-sin, sin], axis=1)
    return jnp.tile(cos_h, (1, LANES // HEAD_DIM)), jnp.tile(sin_h, (1, LANES // HEAD_DIM))


def _column_layout(d_model, d_hy):
    d_gqa = N_Q_HEADS * HEAD_DIM
    d_kv = N_KV_HEADS * HEAD_DIM
    d_diff = N_DIFF_HEADS * 2 * HEAD_DIM
    widths = [3 * d_hy, d_gqa, d_kv, d_kv, d_diff, d_diff, d_diff, 3 * d_model]
    offs = [0]
    for w in widths:
        offs.append(offs[-1] + w)
    deint = np.concatenate([np.arange(0, HEAD_DIM, 2), np.arange(1, HEAD_DIM, 2)])
    q_heads = [kv * GQA_GROUP + g for g in range(GQA_GROUP) for kv in range(N_KV_HEADS)]
    q_cols = np.concatenate([offs[1] + h * HEAD_DIM + deint for h in q_heads])
    k_cols = np.concatenate([offs[2] + h * HEAD_DIM + deint for h in range(N_KV_HEADS)])
    qkv_cols = np.concatenate([q_cols, k_cols, np.arange(offs[3], offs[4])])
    out_rows = np.concatenate([h * HEAD_DIM + np.arange(HEAD_DIM) for h in q_heads])
    return offs, qkv_cols, deint, out_rows


def kernel(x_prompt, x_sample, ffn1_norm, ffn1_w_in, ffn1_w_out, mix_norm, w_in, hy_conv_w, hy_conv_b, hy_filt_w1, hy_filt_b1, hy_filt_w2, hy_filt_b2, hy_filt_w3, hy_filt_freq, hy_skip, gqa_q_norm, gqa_k_norm, diff_lambda, diff_subln, rel_bias, w_branch, w_out, ffn2_norm, ffn2_w_in, ffn2_w_out, final_norm):
    d = x_prompt.shape[-1]
    depth = w_in.shape[0]
    d_hy = hy_skip.shape[1]
    groups = [(x.shape[0], x.shape[1]) for x in (x_prompt, x_sample)]
    xs = [x_prompt.reshape(-1, d), x_sample.reshape(-1, d)]
    lens = sorted({sl for _, sl in groups})
    assert all(sl % GRID_W == 0 for sl in lens)

    offs, qkv_cols, deint, out_rows = _column_layout(d, d_hy)
    cos_t, sin_t = _rope_tables(max(lens))
    bd = jnp.asarray(np.kron(np.eye(LANES // HEAD_DIM), np.full((HEAD_DIM, HEAD_DIM), 1.0 / HEAD_DIM)), BF16)
    tabs = {sl: _dft_tables(sl) for sl in lens}
    diff_tiles = {sl: (min(DIFF_TQ, sl), min(DIFF_TK, sl)) for sl in lens}
    bias = {tt: _bias_tiles(rel_bias, *tt) for tt in sorted(set(diff_tiles.values()))}
    far_bias = rel_bias[jnp.array([N_BUCKETS // 2 - 1, N_BUCKETS - 1])].astype(F32) * LOG2E

    for l in range(depth):
        w1_in, w1_out = ffn1_w_in[l].astype(BF16), ffn1_w_out[l].astype(BF16)
        w2_in, w2_out = ffn2_w_in[l].astype(BF16), ffn2_w_out[l].astype(BF16)
        w_l = w_in[l].astype(BF16)
        w_qkv = w_in[l][:, offs[1]:offs[4]][:, qkv_cols - offs[1]].astype(BF16)
        qn = jnp.tile(gqa_q_norm[l][deint], N_Q_HEADS).reshape(1, -1)
        kn = jnp.tile(gqa_k_norm[l][deint], N_KV_HEADS).reshape(1, -1)
        wb = jnp.stack([w_branch[l][0], w_branch[l][1][out_rows], w_branch[l][2]]).astype(BF16)
        wo = w_out[l].astype(BF16)
        lp = diff_lambda[l].astype(F32)
        lam_init = 0.8 - 0.6 * math.exp(-0.3 * l)
        lam = (jnp.exp(jnp.sum(lp[0] * lp[1])) - jnp.exp(jnp.sum(lp[2] * lp[3])) + lam_init).reshape(1)
        kfs = {sl: _hyena_kf(sl, tabs[sl], hy_filt_w1[l], hy_filt_b1[l], hy_filt_w2[l], hy_filt_b2[l],
                             hy_filt_w3[l], hy_filt_freq[l]) for sl in lens}
        final_g = final_norm if l == depth - 1 else None

        for g, (nseq, sl) in enumerate(groups):
            x = _ffn(xs[g], ffn1_norm[l], w1_in, w1_out)
            hy, gq, gk, gv, dq, dk, dv, gates = _inproj(x, mix_norm[l], w_l, w_qkv, cos_t, sin_t, bd, qn, kn,
                                                         offs, sl)
            z, x0 = _shortconv(hy, hy_conv_w[l], hy_conv_b[l], sl)
            y_hy = _hyena_longconv(z, x0, kfs[sl], hy_skip[l], nseq, sl, tabs[sl])
            y_gqa = _gqa(gq, gk, gv, nseq, sl)
            tq, tk = diff_tiles[sl]
            y_diff = _diff(lam, far_bias, dq, dk, dv, bias[(tq, tk)], diff_subln[l], nseq, sl, lam_init, tq, tk)
            x = _merge(x, y_hy, y_gqa, y_diff, gates, wb, wo)
            xs[g] = _ffn(x, ffn2_norm[l], w2_in, w2_out, final_g=final_g)

    return xs[0].reshape(x_prompt.shape), xs[1].reshape(x_sample.shape)
```

```python
import functools
import math

import numpy as np
import jax
import jax.numpy as jnp
from jax import lax
from jax.experimental import pallas as pl
from jax.experimental.pallas import tpu as pltpu

F32 = jnp.float32
BF16 = jnp.bfloat16

NORM_EPS = 1e-6
GRID_W = 64
HEAD_DIM = 64
N_Q_HEADS = 8
N_KV_HEADS = 2
GQA_GROUP = N_Q_HEADS // N_KV_HEADS
ROPE_THETA = 10000.0
N_DIFF_HEADS = 4
DIFF_SUBLN_EPS = 1e-5
N_BUCKETS = 32
MAX_DISTANCE = 128
POS_BANDS = 16
DECAY_TARGET = 1e-2
FAST_DECAY_PCT = 0.3
SLOW_DECAY_PCT = 1.5

LANES = 128
SUBLANES = 8
MXU_WIDTH = 256
VMEM_LIMIT_BYTES = 56 * 1024 * 1024

FFN_TM = 512
PROJ_TM = 512
MERGE_TM = 512
CONV_TM = 512
GQA_TQ = 128
GQA_TK = 1024
DIFF_TQ = 512
DIFF_TK = 1024
FFT_CC = 8
CONV_CHAINS = 4
KF_CHAINS = 2
FILT_TM = 512

NEG_BIG = -1e30
LOG2E = math.log2(math.e)
KV_TILES_PER_ITER = 8


def _params(n_axes):
    return pltpu.CompilerParams(dimension_semantics=("arbitrary",) * n_axes,
                                vmem_limit_bytes=VMEM_LIMIT_BYTES)


def _dot(a, b):
    return jnp.dot(a, b, preferred_element_type=F32)


def _split_bf16(x):
    hi = x.astype(BF16)
    return hi, (x - hi.astype(F32)).astype(BF16)


def _dot3(a, b):
    ah, al = _split_bf16(a)
    bh, bl = _split_bf16(b)
    return _dot(ah, bh) + (_dot(ah, bl) + _dot(al, bh))


def _rms(x, g, eps):
    ms = jnp.mean(x * x, axis=-1, keepdims=True)
    return x * lax.rsqrt(ms + eps) * g


def _ffn_body(*refs, chunks, final):
    if final:
        x_ref, g_ref, w_in_ref, w_out_ref, gf_ref, o_ref = refs
    else:
        x_ref, g_ref, w_in_ref, w_out_ref, o_ref = refs
    d_ff = w_out_ref.shape[0]
    x = x_ref[...]
    h = _rms(x, g_ref[...], NORM_EPS).astype(BF16)
    acc = None
    for a, b in chunks:
        gate = _dot(h, w_in_ref[:, a:b])
        up = _dot(h, w_in_ref[:, d_ff + a:d_ff + b])
        part = _dot((gate * jax.nn.sigmoid(gate) * up).astype(BF16), w_out_ref[a:b, :])
        acc = part if acc is None else acc + part
    y = x + 0.5 * acc
    if final:
        y = _rms(y, gf_ref[...], NORM_EPS)
    o_ref[...] = y


def _ffn_chunks(d_ff):
    n_blk = d_ff // MXU_WIDTH
    if d_ff % MXU_WIDTH or n_blk < 2:
        return ((0, d_ff),)
    cut = (n_blk + 1) // 2 * MXU_WIDTH
    return ((0, cut), (cut, d_ff))


def _ffn(x, norm_g, w_in, w_out, final_g=None):
    t, d = x.shape
    d_ff = w_out.shape[0]
    tm = min(FFN_TM, t)
    final = final_g is not None
    const = lambda i: (0, 0)
    in_specs = [
        pl.BlockSpec((tm, d), lambda i: (i, 0)),
        pl.BlockSpec((1, d), const),
        pl.BlockSpec(w_in.shape, const, pipeline_mode=pl.Buffered(1)),
        pl.BlockSpec(w_out.shape, const, pipeline_mode=pl.Buffered(1)),
    ]
    args = [x, norm_g.reshape(1, d), w_in, w_out]
    if final:
        in_specs.append(pl.BlockSpec((1, d), const))
        args.append(final_g.reshape(1, d))
    return pl.pallas_call(
        functools.partial(_ffn_body, chunks=_ffn_chunks(d_ff), final=final),
        grid=(t // tm,),
        in_specs=in_specs,
        out_specs=pl.BlockSpec((tm, d), lambda i: (i, 0)),
        out_shape=jax.ShapeDtypeStruct((t, d), F32),
        compiler_params=_params(1),
        name="ffn",
    )(*args)


def _head_rms(x, bd):
    x2 = x * x
    hi = x2.astype(BF16)
    lo = (x2 - hi.astype(F32)).astype(BF16)
    ms = _dot(hi, bd) + _dot(lo, bd)
    return x * lax.rsqrt(ms + NORM_EPS)


def _rope(x, cos, sin_signed):
    lane = lax.broadcasted_iota(jnp.int32, x.shape, 1)
    partner = jnp.where((lane % HEAD_DIM) < HEAD_DIM // 2,
                        pltpu.roll(x, LANES - HEAD_DIM // 2, 1),
                        pltpu.roll(x, HEAD_DIM // 2, 1))
    return x * cos + partner * sin_signed


def _inproj_body(x_ref, g_ref, w_ref, wqkv_ref, cos_ref, sin_ref, bd_ref, qn_ref, kn_ref,
                 hy_ref, gq_ref, gk_ref, gv_ref, dq_ref, dk_ref, dv_ref, gate_ref, *, offs):
    o_hy, o_gq, o_gk, o_gv, o_dq, o_dk, o_dv, o_gate, o_end = offs
    h = _rms(x_ref[...], g_ref[...], NORM_EPS).astype(BF16)

    def seg(a, b):
        return _dot(h, w_ref[:, a:b])

    hy_ref[...] = seg(o_hy, o_gq)
    cos = cos_ref[...]
    sin = sin_ref[...]
    bd = bd_ref[...]
    scale = HEAD_DIM ** -0.5 * LOG2E
    qkv = _dot(h, wqkv_ref[...])
    wq, wk = o_gk - o_gq, o_gv - o_gk
    for c in range(wq // LANES):
        y = _head_rms(qkv[:, c * LANES:(c + 1) * LANES], bd) * qn_ref[:, c * LANES:(c + 1) * LANES]
        gq_ref[:, c * LANES:(c + 1) * LANES] = (_rope(y, cos, sin) * scale).astype(BF16)
    for c in range(wk // LANES):
        y = _head_rms(qkv[:, wq + c * LANES:wq + (c + 1) * LANES], bd) * kn_ref[:, c * LANES:(c + 1) * LANES]
        gk_ref[:, c * LANES:(c + 1) * LANES] = _rope(y, cos, sin).astype(BF16)
    gv_ref[...] = qkv[:, wq + wk:].astype(BF16)
    dq_ref[...] = (seg(o_dq, o_dk) * scale).astype(BF16)
    dk_ref[...] = seg(o_dk, o_dv).astype(BF16)
    dv_ref[...] = seg(o_dv, o_gate).astype(BF16)
    d = x_ref.shape[1]
    for c in range((o_end - o_gate) // d):
        gate_ref[:, c * d:(c + 1) * d] = jax.nn.sigmoid(seg(o_gate + c * d, o_gate + (c + 1) * d)).astype(BF16)


def _inproj(x, norm_g, w, wqkv, cos_t, sin_t, bd, qn, kn, offs, seq_len):
    t, d = x.shape
    tm = min(PROJ_TM, seq_len)
    widths = [offs[i + 1] - offs[i] for i in range(8)]

    def pos_map(i):
        return (i % (seq_len // tm), 0)

    full = lambda i: (0, 0)
    row = lambda i: (i, 0)
    out_dtypes = [F32] + [BF16] * 7
    return pl.pallas_call(
        functools.partial(_inproj_body, offs=tuple(offs)),
        grid=(t // tm,),
        in_specs=[
            pl.BlockSpec((tm, d), row),
            pl.BlockSpec((1, d), full),
            pl.BlockSpec(w.shape, full, pipeline_mode=pl.Buffered(1)),
            pl.BlockSpec(wqkv.shape, full, pipeline_mode=pl.Buffered(1)),
            pl.BlockSpec((tm, LANES), pos_map),
            pl.BlockSpec((tm, LANES), pos_map),
            pl.BlockSpec(bd.shape, full),
            pl.BlockSpec(qn.shape, full),
            pl.BlockSpec(kn.shape, full),
        ],
        out_specs=[pl.BlockSpec((tm, wd), row) for wd in widths],
        out_shape=[jax.ShapeDtypeStruct((t, wd), dt) for wd, dt in zip(widths, out_dtypes)],
        compiler_params=_params(1),
        name="inproj",
    )(x, norm_g.reshape(1, d), w, wqkv, cos_t, sin_t, bd, qn, kn)


def _shortconv_body(u_ref, prev_ref, next_ref, w_ref, b_ref, z_ref, x0_ref, *, tm, seq_len, d_hy):
    pos = (pl.program_id(0) * tm) % seq_len
    u = u_ref[...]
    rows = lax.broadcasted_iota(jnp.int32, u.shape, 0)
    before = jnp.where(pos == 0, 0.0, prev_ref[SUBLANES - 1:SUBLANES, :])
    after = jnp.where(pos + tm == seq_len, 0.0, next_ref[0:1, :])
    up = jnp.where(rows == 0, before, pltpu.roll(u, 1, 0))
    dn = jnp.where(rows == tm - 1, after, pltpu.roll(u, tm - 1, 0))
    y = up * w_ref[0:1, :] + u * w_ref[1:2, :] + dn * w_ref[2:3, :] + b_ref[...]
    x0_ref[...] = y[:, :d_hy]
    z_ref[...] = y[:, 2 * d_hy:] * y[:, d_hy:2 * d_hy]


def _shortconv(u, w, b, seq_len):
    t, c3 = u.shape
    d_hy = c3 // 3
    tm = min(CONV_TM, seq_len)
    nb8 = t // SUBLANES
    return pl.pallas_call(
        functools.partial(_shortconv_body, tm=tm, seq_len=seq_len, d_hy=d_hy),
        grid=(t // tm,),
        in_specs=[
            pl.BlockSpec((tm, c3), lambda i: (i, 0)),
            pl.BlockSpec((SUBLANES, c3), lambda i: (jnp.maximum(i * (tm // SUBLANES) - 1, 0), 0)),
            pl.BlockSpec((SUBLANES, c3), lambda i: (jnp.minimum((i + 1) * (tm // SUBLANES), nb8 - 1), 0)),
            pl.BlockSpec((3, c3), lambda i: (0, 0)),
            pl.BlockSpec((1, c3), lambda i: (0, 0)),
        ],
        out_specs=[pl.BlockSpec((tm, d_hy), lambda i: (i, 0))] * 2,
        out_shape=[jax.ShapeDtypeStruct((t, d_hy), F32)] * 2,
        compiler_params=_params(1),
        name="shortconv",
    )(u, u, u, w, b.reshape(1, c3))


def _filter_body(f_ref, w1_ref, b1_ref, w2_ref, b2_ref, w3_ref, fr_ref, dl_ref, o_ref, *, n_feat):
    half = f_ref.shape[0] // 2
    d_hy = o_ref.shape[1]
    halves = (f_ref[:half, :], f_ref[half:, :])
    fr = fr_ref[...]
    h = jnp.sin(fr * (_dot3(jnp.concatenate(halves, axis=1), w1_ref[...]) + b1_ref[...]))
    h = jnp.sin(fr * (_dot3(h, w2_ref[...]) + b2_ref[...]))
    h = _dot3(h, w3_ref[0])
    for k, f in enumerate(halves):
        decay = jnp.exp(-f[:, 0:1] * dl_ref[...])
        o_ref[k * half:(k + 1) * half, :] = h[:, k * d_hy:(k + 1) * d_hy] * decay * f[:, n_feat:n_feat + 1]


def _hyena_filter(feats, w1p, b1, w2, b2, w3, freq, deltas_abs, seq_len, n_feat):
    n2l, fw = feats.shape
    d_hy = w3.shape[1] // 2
    hid = w2.shape[0]
    tm = min(FILT_TM, seq_len)
    nbl = seq_len // tm
    full = lambda i: (0, 0)
    twice = lambda m: jnp.kron(jnp.eye(2, dtype=m.dtype), m)
    pair = lambda v: jnp.tile(v, 2).reshape(1, -1)
    return pl.pallas_call(
        functools.partial(_filter_body, n_feat=n_feat),
        grid=(n2l // tm,),
        in_specs=[
            pl.BlockSpec((tm, fw), lambda i: (i, 0)),
            pl.BlockSpec((2 * fw, 2 * hid), full),
            pl.BlockSpec((1, 2 * hid), full),
            pl.BlockSpec((2 * hid, 2 * hid), full),
            pl.BlockSpec((1, 2 * hid), full),
            pl.BlockSpec((1, 2 * hid, 2 * d_hy), lambda i: (i // nbl, 0, 0)),
            pl.BlockSpec((1, 2 * hid), full),
            pl.BlockSpec((1, d_hy), full),
        ],
        out_specs=pl.BlockSpec((tm, d_hy), lambda i: (i, 0)),
        out_shape=jax.ShapeDtypeStruct((n2l, d_hy), F32),
        compiler_params=_params(1),
        name="hyena_filter",
    )(feats, twice(w1p), pair(b1), twice(w2), pair(b2),
      jnp.stack([twice(w3[:, :d_hy]), twice(w3[:, d_hy:])]), pair(freq), deltas_abs.reshape(1, d_hy))


def _fwd_spectra(f1, chunks, twr, twi, w2f, cc):
    n1 = twr.shape[0]
    a = [_dot3(f1, x) for x in chunks]
    b = []
    for aj in a:
        rows = []
        for c in range(cc):
            ar = aj[:n1, c * LANES:(c + 1) * LANES]
            ai = aj[n1:, c * LANES:(c + 1) * LANES]
            rows.append(jnp.concatenate([ar * twr - ai * twi, ar * twi + ai * twr], axis=1))
        b.append(jnp.concatenate(rows, axis=0))
    return [_dot3(bj, w2f) for bj in b]


def _kf_body(f1_ref, x_ref, twr_ref, twi_ref, w2f_ref, o_ref, *, cc, n_sub, scale):
    n1 = twr_ref.shape[0]
    w = cc * LANES
    xs = _fwd_spectra(f1_ref[...], [x_ref[:, j * w:(j + 1) * w] for j in range(n_sub)],
                      twr_ref[...], twi_ref[...], w2f_ref[...], cc)
    for j, xj in enumerate(xs):
        for c in range(cc):
            lanes = slice(j * w + c * LANES, j * w + (c + 1) * LANES)
            o_ref[0, :, lanes] = xj[c * n1:(c + 1) * n1, :LANES] * scale
            o_ref[1, :, lanes] = xj[c * n1:(c + 1) * n1, LANES:] * scale


def _conv_body(f1_ref, finv_ref, z_ref, x0_ref, kf_ref, skip_ref, twr_ref, twi_ref, w2f_ref, w2i_ref,
               o_ref, *, cc, n_sub):
    n1 = twr_ref.shape[0]
    w = cc * LANES
    twr, twi = twr_ref[...], twi_ref[...]
    cols = [slice(j * w, (j + 1) * w) for j in range(n_sub)]
    zs = [z_ref[0, :, c] for c in cols]
    xs = _fwd_spectra(f1_ref[...], zs, twr, twi, w2f_ref[...], cc)
    ys = []
    for xj, cj in zip(xs, cols):
        xr, xi = xj[:, :LANES], xj[:, LANES:]
        kr = jnp.concatenate([kf_ref[0, :, cj][:, c * LANES:(c + 1) * LANES] for c in range(cc)], axis=0)
        ki = jnp.concatenate([kf_ref[1, :, cj][:, c * LANES:(c + 1) * LANES] for c in range(cc)], axis=0)
        ys.append(jnp.concatenate([xr * kr - xi * ki, xr * ki + xi * kr], axis=1))
    w2i = w2i_ref[...]
    cs = [_dot3(yj, w2i) for yj in ys]
    ds = []
    for cj in cs:
        d_re, d_im = [], []
        for c in range(cc):
            cr = cj[c * n1:(c + 1) * n1, :LANES]
            ci = cj[c * n1:(c + 1) * n1, LANES:]
            d_re.append(cr * twr + ci * twi)
            d_im.append(ci * twr - cr * twi)
        ds.append(jnp.concatenate([jnp.concatenate(d_re, axis=1), jnp.concatenate(d_im, axis=1)], axis=0))
    finv = finv_ref[...]
    convs = [_dot3(finv, dj) for dj in ds]
    for conv, z, cj in zip(convs, zs, cols):
        o_ref[0, :, cj] = (x0_ref[0, :, cj] * (conv + z * skip_ref[:, cj])).astype(o_ref.dtype)


def _dft_tables(seq_len):
    n = 2 * seq_len
    n1 = n // LANES
    n1h = n1 // 2
    n1r = n1h + SUBLANES
    k1 = np.arange(n1r)[:, None]
    kept = (k1 <= n1h).astype(np.float64)
    ang1 = 2.0 * np.pi * k1 * np.arange(n1)[None, :] / n1
    f1 = np.concatenate([kept * np.cos(ang1), -kept * np.sin(ang1)], axis=0)
    weight = kept * np.where((k1 == 0) | (k1 == n1h), 1.0, 2.0)
    finv = np.concatenate([(weight * np.cos(ang1[:, :n1h])).T,
                           (-weight * np.sin(ang1[:, :n1h])).T], axis=1)
    ang2 = 2.0 * np.pi * np.arange(LANES)[:, None] * np.arange(LANES)[None, :] / LANES
    f2r, f2i = np.cos(ang2), -np.sin(ang2)
    w2f = np.block([[f2r, f2i], [-f2i, f2r]])
    w2i = np.block([[f2r, -f2i], [f2i, f2r]])
    angt = 2.0 * np.pi * k1 * np.arange(LANES)[None, :] / n
    f = lambda a: jnp.asarray(a, F32)
    return dict(n=n, n1=n1, n1h=n1h, n1r=n1r, f1_full=f(f1), f1_half=f(f1[:, :n1h]), finv=f(finv),
                w2f=f(w2f), w2i=f(w2i), twr=f(np.cos(angt)), twi=f(-np.sin(angt)))


def _to_blocked(x, nseq):
    t, c = x.shape
    r = t // (nseq * LANES)
    return x.reshape(nseq, r, LANES, c).transpose(0, 1, 3, 2).reshape(nseq, r, c * LANES)


def _from_blocked(y, c):
    nseq, r, _ = y.shape
    return y.reshape(nseq, r, c, LANES).transpose(0, 1, 3, 2).reshape(nseq * r * LANES, c)


def _filter_features(seq_len, fw):
    t = jnp.linspace(0.0, 1.0, seq_len, dtype=F32)[:, None]
    band = jnp.linspace(1e-4, POS_BANDS - 1, POS_BANDS, dtype=F32)
    ang = (2.0 * math.pi / seq_len) * jnp.arange(seq_len, dtype=F32)[:, None] * band[None, :]
    feats = jnp.concatenate([t, jnp.cos(ang), -jnp.sin(ang)], axis=-1)
    n_feat = feats.shape[1]
    rows = jnp.concatenate([feats, feats[-1:], feats[:0:-1]], axis=0)
    valid = jnp.asarray((np.arange(2 * seq_len) != seq_len).astype(np.float32))[:, None]
    rows = jnp.concatenate([rows, valid], axis=1)
    return jnp.pad(rows, ((0, 0), (0, fw - n_feat - 1))), n_feat


def _hyena_kf(seq_len, tabs, w1, b1, w2, b2, w3, freq):
    d_hy = w3.shape[1] // 2
    feats, n_feat = _filter_features(seq_len, LANES)
    w1p = jnp.pad(w1, ((0, LANES - w1.shape[0]), (0, 0)))
    max_decay = math.log(DECAY_TARGET) / FAST_DECAY_PCT
    min_decay = math.log(DECAY_TARGET) / SLOW_DECAY_PCT
    deltas = jnp.abs(jnp.linspace(min_decay, max_decay, d_hy, dtype=F32))
    kern = _hyena_filter(feats, w1p, b1, w2, b2, w3, freq, deltas, seq_len, n_feat)
    n, n1, n1r = tabs["n"], tabs["n1"], tabs["n1r"]
    cc = FFT_CC
    n_sub = KF_CHAINS
    w = n_sub * cc * LANES
    full = lambda j: (0, 0)
    return pl.pallas_call(
        functools.partial(_kf_body, cc=cc, n_sub=n_sub, scale=1.0 / n),
        grid=(d_hy // (n_sub * cc),),
        in_specs=[pl.BlockSpec((2 * n1r, n1), full),
                  pl.BlockSpec((n1, w), lambda j: (0, j)),
                  pl.BlockSpec((n1r, LANES), full), pl.BlockSpec((n1r, LANES), full),
                  pl.BlockSpec((2 * LANES, 2 * LANES), full)],
        out_specs=pl.BlockSpec((2, n1r, w), lambda j: (0, 0, j)),
        out_shape=jax.ShapeDtypeStruct((2, n1r, d_hy * LANES), F32),
        compiler_params=_params(1),
        name="hyena_kf",
    )(tabs["f1_full"], _to_blocked(kern, 1)[0], tabs["twr"], tabs["twi"], tabs["w2f"])


def _hyena_longconv(z, x0, kf, skip, nseq, seq_len, tabs):
    c = z.shape[1]
    n1h, n1r = tabs["n1h"], tabs["n1r"]
    cc = FFT_CC
    n_sub = CONV_CHAINS
    w = n_sub * cc * LANES
    full = lambda b, j: (0, 0)
    rowblk = pl.BlockSpec((1, n1h, w), lambda b, j: (b, 0, j))
    y = pl.pallas_call(
        functools.partial(_conv_body, cc=cc, n_sub=n_sub),
        grid=(nseq, c // (n_sub * cc)),
        in_specs=[pl.BlockSpec((2 * n1r, n1h), full),
                  pl.BlockSpec((n1h, 2 * n1r), full),
                  rowblk, rowblk,
                  pl.BlockSpec((2, n1r, w), lambda b, j: (0, 0, j)),
                  pl.BlockSpec((1, w), lambda b, j: (0, j)),
                  pl.BlockSpec((n1r, LANES), full), pl.BlockSpec((n1r, LANES), full),
                  pl.BlockSpec((2 * LANES, 2 * LANES), full), pl.BlockSpec((2 * LANES, 2 * LANES), full)],
        out_specs=rowblk,
        out_shape=jax.ShapeDtypeStruct((nseq, n1h, c * LANES), BF16),
        compiler_params=_params(2),
        name="hyena_conv",
    )(tabs["f1_half"], tabs["finv"], _to_blocked(z, nseq), _to_blocked(x0, nseq), kf,
      jnp.repeat(skip.astype(F32), LANES).reshape(1, c * LANES),
      tabs["twr"], tabs["twi"], tabs["w2f"], tabs["w2i"])
    return _from_blocked(y, c)


def _stack_halves(q_cols, lhs_ref, tq):
    lane = lax.broadcasted_iota(jnp.int32, (tq, LANES), 1)
    low = lane < HEAD_DIM
    zero = jnp.zeros((tq, LANES), lhs_ref.dtype)
    for c, q in enumerate(q_cols):
        lhs_ref[(2 * c) * tq:(2 * c + 1) * tq, :] = jnp.where(low, q, zero)
        lhs_ref[(2 * c + 1) * tq:(2 * c + 2) * tq, :] = jnp.where(low, zero, q)


def _consume(s, v, m_scr, acc_scr, const=None):
    tk = s.shape[1]
    m_prev = m_scr[...]
    row_max = jnp.max(s, axis=1, keepdims=True)
    if const is not None:
        row_max = row_max + const
    m_new = jnp.maximum(m_prev, row_max)
    shift = m_new if const is None else m_new - const
    p = jnp.exp2(s - jnp.concatenate([shift] * (tk // LANES), axis=1)).astype(BF16)
    alpha = jnp.exp2(m_prev - m_new)
    v_ext = jnp.concatenate([v, jnp.ones_like(v)], axis=1)
    acc_scr[...] = jnp.concatenate([alpha, alpha], axis=1) * acc_scr[...] + _dot(p, v_ext)
    m_scr[...] = m_new


def _qk(lhs, k):
    return lax.dot_general(lhs, k, (((1,), (1,)), ((), ())), preferred_element_type=F32)


def _key_rows(t, tk):
    return pl.ds(t * tk if isinstance(t, int) else pl.multiple_of(t * tk, tk), tk)


def _pipelined_attention(qi, n_kv, n_special, stack_fn, score_fn, consume_fn, m_scr, acc_scr, s_scr):
    @pl.when(qi == 0)
    def _():
        stack_fn(False)
        s_scr[0][...] = score_fn(0, n_special == n_kv, False)

    m_scr[...] = jnp.full_like(m_scr, NEG_BIG)
    acc_scr[...] = jnp.zeros_like(acc_scr)
    unroll = min(KV_TILES_PER_ITER, n_kv)
    assert n_kv % unroll == 0 and n_special <= unroll

    def run(p0, last):
        for j in range(unroll):
            special = last and j >= unroll - n_special
            if last and j == unroll - 1:
                stack_fn(True)
                s_scr[(j + 1) % 2][...] = score_fn(0, n_special == n_kv, True)
            else:
                s_scr[(j + 1) % 2][...] = score_fn(p0 + j + 1, last and j + 1 >= unroll - n_special, False)
            consume_fn(s_scr[j % 2][...], p0 + j, special)

    def body(i, carry):
        run(i * unroll, False)
        return carry

    lax.fori_loop(0, n_kv // unroll - 1, body, 0)
    run(n_kv - unroll, True)
    if unroll % 2 == 1:
        s_scr[0][...] = s_scr[1][...]


def _attn_scratch(m, tk):
    return [pltpu.VMEM((2, m, LANES), BF16), pltpu.VMEM((m, LANES), F32), pltpu.VMEM((m, 2 * LANES), F32),
            pltpu.VMEM((m, tk), F32), pltpu.VMEM((m, tk), F32)]


def _gqa_body(q_ref, qn_ref, k_ref, v_ref, o_ref, lhs_scr, m_scr, acc_scr, s0_scr, s1_scr, *, tq, tk, n_kv):
    qi = pl.program_id(1)
    cur = qi % 2
    n_col = q_ref.shape[1] // LANES

    def stack(nxt):
        ref = qn_ref if nxt else q_ref
        slot = 1 - cur if nxt else cur
        _stack_halves([ref[:, c * LANES:(c + 1) * LANES] for c in range(n_col)], lhs_scr.at[slot], tq)

    def score(t, special, nxt):
        return _qk(lhs_scr[1 - cur if nxt else cur], k_ref[_key_rows(t, tk), :])

    def consume(s, t, special):
        _consume(s, v_ref[_key_rows(t, tk), :], m_scr, acc_scr)

    _pipelined_attention(qi, n_kv, 0, stack, score, consume, m_scr, acc_scr, (s0_scr, s1_scr))
    o = acc_scr[:, :LANES] / acc_scr[:, LANES:]
    low = lax.broadcasted_iota(jnp.int32, (tq, LANES), 1) < HEAD_DIM
    for c in range(n_col):
        o_ref[:, c * LANES:(c + 1) * LANES] = jnp.where(
            low, o[(2 * c) * tq:(2 * c + 1) * tq], o[(2 * c + 1) * tq:(2 * c + 2) * tq]
        ).astype(o_ref.dtype)


def _gqa(q, k, v, nseq, seq_len):
    dq = q.shape[1]
    tq = min(GQA_TQ, seq_len)
    tk = min(GQA_TK, seq_len)
    nq = seq_len // tq
    assert nq == 1 or nq % 2 == 0
    m = 2 * (dq // LANES) * tq
    return pl.pallas_call(
        functools.partial(_gqa_body, tq=tq, tk=tk, n_kv=seq_len // tk),
        grid=(nseq, nq),
        in_specs=[pl.BlockSpec((tq, dq), lambda b, i: (b * nq + i, 0)),
                  pl.BlockSpec((tq, dq), lambda b, i: (b * nq + jnp.minimum(i + 1, nq - 1), 0)),
                  pl.BlockSpec((seq_len, LANES), lambda b, i: (b, 0)),
                  pl.BlockSpec((seq_len, LANES), lambda b, i: (b, 0))],
        out_specs=pl.BlockSpec((tq, dq), lambda b, i: (b * nq + i, 0)),
        out_shape=jax.ShapeDtypeStruct((nseq * seq_len, dq), BF16),
        scratch_shapes=_attn_scratch(m, tk),
        compiler_params=_params(2),
        name="gqa_attn",
    )(q, q, k, v)


def _bias_span(tq, tk):
    u_lo = -((MAX_DISTANCE + tk - 1 + tq - 1) // tq)
    u_hi = (MAX_DISTANCE + tq - 1 + tq - 1) // tq
    return u_lo, u_hi


def _t5_bucket(rel):
    nb = N_BUCKETS // 2
    max_exact = nb // 2
    ret = jnp.where(rel > 0, nb, 0)
    n = jnp.abs(rel)
    nf = jnp.maximum(n, 1).astype(F32)
    large = max_exact + (jnp.log(nf / max_exact) / math.log(MAX_DISTANCE / max_exact)
                         * (nb - max_exact)).astype(jnp.int32)
    large = jnp.minimum(large, nb - 1)
    return ret + jnp.where(n < max_exact, n, large)


def _bias_body(tab_ref, o_ref, *, tq, tk, u_lo, width):
    h = pl.program_id(0)
    u = pl.program_id(1) + u_lo
    rel = u * tq - (tq - 1) + lax.broadcasted_iota(jnp.int32, (SUBLANES, width), 1)
    bucket = _t5_bucket(rel)
    profile = jnp.zeros((SUBLANES, width), F32)
    for b in range(N_BUCKETS):
        profile = jnp.where(bucket == b, tab_ref[b, h], profile)
    rows = jnp.broadcast_to(profile[0:1, :], (tq, width))
    tile = pltpu.roll(rows, width - (tq - 1), 1, stride=1, stride_axis=0)
    o_ref[0, 0] = tile[:, :tk] * LOG2E


def _bias_tiles(rel_bias, tq, tk):
    u_lo, u_hi = _bias_span(tq, tk)
    n_off = u_hi - u_lo + 1
    n_heads = rel_bias.shape[1]
    width = pl.next_power_of_2(tq + tk - 1)
    return pl.pallas_call(
        functools.partial(_bias_body, tq=tq, tk=tk, u_lo=u_lo, width=width),
        grid=(n_heads, n_off),
        in_specs=[pl.BlockSpec(memory_space=pltpu.SMEM)],
        out_specs=pl.BlockSpec((1, 1, tq, tk), lambda h, u: (h, u, 0, 0)),
        out_shape=jax.ShapeDtypeStruct((n_heads, n_off, tq, tk), F32),
        compiler_params=_params(2),
        name="t5_bias",
    )(rel_bias.astype(F32))


def _near_count(tq, tk, u_lo, u_hi):
    r = tk // tq
    return max((qi + u_hi - 1) // r - (qi + u_lo) // r for qi in range(r))


def _diff_body(lam_ref, far_ref, q_ref, qn_ref, k_ref, v_ref, bias_ref, g_ref, o_ref,
               lhs_scr, m_scr, acc_scr, s0_scr, s1_scr, *, tq, tk, n_kv, n_near, u_lo, u_hi, out_scale):
    h = pl.program_id(0)
    qi = pl.program_id(2)
    cur = qi % 2
    r = tk // tq

    def stack(nxt):
        _stack_halves([(qn_ref if nxt else q_ref)[...]], lhs_scr.at[1 - cur if nxt else cur], tq)

    def first_near(q_idx):
        return jnp.clip((q_idx + u_lo) // r + 1, 0, n_kv - n_near)

    def tile_of(p, near, q_idx):
        t_a = first_near(q_idx)
        if near:
            return t_a + (p - (n_kv - n_near))
        return jnp.where(p < t_a, p, p + n_near)

    def score(p, near, nxt):
        q_idx = qi + 1 if nxt else qi
        t = tile_of(p, near, q_idx)
        s = _qk(lhs_scr[1 - cur if nxt else cur], k_ref[_key_rows(t, tk), :])
        if not near:
            return s
        u = jnp.clip(t * r - q_idx, u_lo, u_hi) - u_lo
        return (s.reshape(2, tq, tk) + bias_ref[0, u][None]).reshape(2 * tq, tk)

    def consume(s, p, near):
        t = tile_of(p, near, qi)
        v = v_ref[_key_rows(t, tk), :]
        if near:
            _consume(s, v, m_scr, acc_scr)
        else:
            _consume(s, v, m_scr, acc_scr, const=jnp.where(t < first_near(qi), far_ref[0, h], far_ref[1, h]))

    _pipelined_attention(qi, n_kv, n_near, stack, score, consume, m_scr, acc_scr, (s0_scr, s1_scr))
    o = acc_scr[:, :LANES] / acc_scr[:, LANES:]
    o = o[:tq] - lam_ref[0] * o[tq:]
    o_ref[...] = (_rms(o, g_ref[...], DIFF_SUBLN_EPS) * out_scale).astype(o_ref.dtype)


def _diff(lam, far, q, k, v, bias, subln_g, nseq, seq_len, lam_init, tq, tk):
    n_heads = q.shape[1] // LANES
    nq = seq_len // tq
    assert nq == 1 or nq % 2 == 0
    u_lo, u_hi = _bias_span(tq, tk)
    n_off = u_hi - u_lo + 1
    kv_spec = pl.BlockSpec((seq_len, LANES), lambda h, b, i: (b, h))
    n_kv = seq_len // tk
    n_near = min(_near_count(tq, tk, u_lo, u_hi), n_kv)
    return pl.pallas_call(
        functools.partial(_diff_body, tq=tq, tk=tk, n_kv=n_kv, n_near=n_near, u_lo=u_lo, u_hi=u_hi,
                          out_scale=1.0 - lam_init),
        grid=(n_heads, nseq, nq),
        in_specs=[pl.BlockSpec(memory_space=pltpu.SMEM),
                  pl.BlockSpec(memory_space=pltpu.SMEM),
                  pl.BlockSpec((tq, LANES), lambda h, b, i: (b * nq + i, h)),
                  pl.BlockSpec((tq, LANES), lambda h, b, i: (b * nq + jnp.minimum(i + 1, nq - 1), h)),
                  kv_spec, kv_spec,
                  pl.BlockSpec((1, n_off, tq, tk), lambda h, b, i: (h, 0, 0, 0),
                               pipeline_mode=pl.Buffered(1)),
                  pl.BlockSpec((1, LANES), lambda h, b, i: (0, 0))],
        out_specs=pl.BlockSpec((tq, LANES), lambda h, b, i: (b * nq + i, h)),
        out_shape=jax.ShapeDtypeStruct((nseq * seq_len, q.shape[1]), BF16),
        scratch_shapes=_attn_scratch(2 * tq, tk),
        compiler_params=_params(3),
        name="diff_attn",
    )(lam, far, q, q, k, v, bias, subln_g.reshape(1, LANES))


def _merge_body(x_ref, yh_ref, yg_ref, yd_ref, gate_ref, wb_ref, wo_ref, o_ref):
    d = x_ref.shape[1]
    merged = (gate_ref[:, 0:d].astype(F32) * _dot(yh_ref[...], wb_ref[0])
              + gate_ref[:, d:2 * d].astype(F32) * _dot(yg_ref[...], wb_ref[1])
              + gate_ref[:, 2 * d:3 * d].astype(F32) * _dot(yd_ref[...], wb_ref[2]))
    o_ref[...] = x_ref[...] + _dot(merged.astype(BF16), wo_ref[...])


def _merge(x, y_hy, y_gqa, y_diff, gates, wb, wo):
    t, d = x.shape
    db = y_hy.shape[1]
    tm = min(MERGE_TM, t)
    row = lambda i: (i, 0)
    return pl.pallas_call(
        _merge_body,
        grid=(t // tm,),
        in_specs=[pl.BlockSpec((tm, d), row),
                  pl.BlockSpec((tm, db), row), pl.BlockSpec((tm, db), row), pl.BlockSpec((tm, db), row),
                  pl.BlockSpec((tm, 3 * d), row),
                  pl.BlockSpec(wb.shape, lambda i: (0, 0, 0)),
                  pl.BlockSpec(wo.shape, lambda i: (0, 0))],
        out_specs=pl.BlockSpec((tm, d), row),
        out_shape=jax.ShapeDtypeStruct((t, d), F32),
        compiler_params=_params(1),
        name="merge",
    )(x, y_hy, y_gqa, y_diff, gates, wb, wo)


def _rope_tables(max_len):
    pos = np.arange(max_len)
    half = HEAD_DIM // 2
    inv = ROPE_THETA ** (-jnp.arange(0, half, 2, dtype=F32) / half)
    row = jnp.asarray(pos // GRID_W, F32)
    col = jnp.asarray(pos % GRID_W, F32)
    ang = jnp.concatenate([row[:, None] * inv, col[:, None] * inv], axis=-1)
    cos, sin = jnp.cos(ang), jnp.sin(ang)
    cos_h = jnp.concatenate([cos, cos], axis=1)
    sin_h = jnp.concatenate([-sin, sin], axis=1)
    return jnp.tile(cos_h, (1, LANES // HEAD_DIM)), jnp.tile(sin_h, (1, LANES // HEAD_DIM))


def _column_layout(d_model, d_hy):
    d_gqa = N_Q_HEADS * HEAD_DIM
    d_kv = N_KV_HEADS * HEAD_DIM
    d_diff = N_DIFF_HEADS * 2 * HEAD_DIM
    widths = [3 * d_hy, d_gqa, d_kv, d_kv, d_diff, d_diff, d_diff, 3 * d_model]
    offs = [0]
    for w in widths:
        offs.append(offs[-1] + w)
    deint = np.concatenate([np.arange(0, HEAD_DIM, 2), np.arange(1, HEAD_DIM, 2)])
    q_heads = [kv * GQA_GROUP + g for g in range(GQA_GROUP) for kv in range(N_KV_HEADS)]
    q_cols = np.concatenate([offs[1] + h * HEAD_DIM + deint for h in q_heads])
    k_cols = np.concatenate([offs[2] + h * HEAD_DIM + deint for h in range(N_KV_HEADS)])
    qkv_cols = np.concatenate([q_cols, k_cols, np.arange(offs[3], offs[4])])
    out_rows = np.concatenate([h * HEAD_DIM + np.arange(HEAD_DIM) for h in q_heads])
    return offs, qkv_cols, deint, out_rows


def kernel(x_prompt, x_sample, ffn1_norm, ffn1_w_in, ffn1_w_out, mix_norm, w_in, hy_conv_w, hy_conv_b, hy_filt_w1, hy_filt_b1, hy_filt_w2, hy_filt_b2, hy_filt_w3, hy_filt_freq, hy_skip, gqa_q_norm, gqa_k_norm, diff_lambda, diff_subln, rel_bias, w_branch, w_out, ffn2_norm, ffn2_w_in, ffn2_w_out, final_norm):
    d = x_prompt.shape[-1]
    depth = w_in.shape[0]
    d_hy = hy_skip.shape[1]
    groups = [(x.shape[0], x.shape[1]) for x in (x_prompt, x_sample)]
    xs = [x_prompt.reshape(-1, d), x_sample.reshape(-1, d)]
    lens = sorted({sl for _, sl in groups})
    assert all(sl % GRID_W == 0 for sl in lens)

    offs, qkv_cols, deint, out_rows = _column_layout(d, d_hy)
    cos_t, sin_t = _rope_tables(max(lens))
    bd = jnp.asarray(np.kron(np.eye(LANES // HEAD_DIM), np.full((HEAD_DIM, HEAD_DIM), 1.0 / HEAD_DIM)), BF16)
    tabs = {sl: _dft_tables(sl) for sl in lens}
    diff_tiles = {sl: (min(DIFF_TQ, sl), min(DIFF_TK, sl)) for sl in lens}
    bias = {tt: _bias_tiles(rel_bias, *tt) for tt in sorted(set(diff_tiles.values()))}
    far_bias = rel_bias[jnp.array([N_BUCKETS // 2 - 1, N_BUCKETS - 1])].astype(F32) * LOG2E

    for l in range(depth):
        w1_in, w1_out = ffn1_w_in[l].astype(BF16), ffn1_w_out[l].astype(BF16)
        w2_in, w2_out = ffn2_w_in[l].astype(BF16), ffn2_w_out[l].astype(BF16)
        w_l = w_in[l].astype(BF16)
        w_qkv = w_in[l][:, offs[1]:offs[4]][:, qkv_cols - offs[1]].astype(BF16)
        qn = jnp.tile(gqa_q_norm[l][deint], N_Q_HEADS).reshape(1, -1)
        kn = jnp.tile(gqa_k_norm[l][deint], N_KV_HEADS).reshape(1, -1)
        wb = jnp.stack([w_branch[l][0], w_branch[l][1][out_rows], w_branch[l][2]]).astype(BF16)
        wo = w_out[l].astype(BF16)
        lp = diff_lambda[l].astype(F32)
        lam_init = 0.8 - 0.6 * math.exp(-0.3 * l)
        lam = (jnp.exp(jnp.sum(lp[0] * lp[1])) - jnp.exp(jnp.sum(lp[2] * lp[3])) + lam_init).reshape(1)
        kfs = {sl: _hyena_kf(sl, tabs[sl], hy_filt_w1[l], hy_filt_b1[l], hy_filt_w2[l], hy_filt_b2[l],
                             hy_filt_w3[l], hy_filt_freq[l]) for sl in lens}
        final_g = final_norm if l == depth - 1 else None

        for g, (nseq, sl) in enumerate(groups):
            x = _ffn(xs[g], ffn1_norm[l], w1_in, w1_out)
            hy, gq, gk, gv, dq, dk, dv, gates = _inproj(x, mix_norm[l], w_l, w_qkv, cos_t, sin_t, bd, qn, kn,
                                                         offs, sl)
            z, x0 = _shortconv(hy, hy_conv_w[l], hy_conv_b[l], sl)
            y_hy = _hyena_longconv(z, x0, kfs[sl], hy_skip[l], nseq, sl, tabs[sl])
            y_gqa = _gqa(gq, gk, gv, nseq, sl)
            tq, tk = diff_tiles[sl]
            y_diff = _diff(lam, far_bias, dq, dk, dv, bias[(tq, tk)], diff_subln[l], nseq, sl, lam_init, tq, tk)
            x = _merge(x, y_hy, y_gqa, y_diff, gates, wb, wo)
            xs[g] = _ffn(x, ffn2_norm[l], w2_in, w2_out, final_g=final_g)

    return xs[0].reshape(x_prompt.shape), xs[1].reshape(x_sample.shape)
```

```python
import functools
import math

import numpy as np
import jax
import jax.numpy as jnp
from jax import lax
from jax.experimental import pallas as pl
from jax.experimental.pallas import tpu as pltpu

F32 = jnp.float32
BF16 = jnp.bfloat16

NORM_EPS = 1e-6
GRID_W = 64
HEAD_DIM = 64
N_Q_HEADS = 8
N_KV_HEADS = 2
GQA_GROUP = N_Q_HEADS // N_KV_HEADS
ROPE_THETA = 10000.0
N_DIFF_HEADS = 4
DIFF_SUBLN_EPS = 1e-5
N_BUCKETS = 32
MAX_DISTANCE = 128
POS_BANDS = 16
DECAY_TARGET = 1e-2
FAST_DECAY_PCT = 0.3
SLOW_DECAY_PCT = 1.5

LANES = 128
SUBLANES = 8
MXU_WIDTH = 256
VMEM_LIMIT_BYTES = 56 * 1024 * 1024

FFN_TM = 512
PROJ_TM = 512
MERGE_TM = 512
GQA_TQ = 128
GQA_TK = 1024
DIFF_TQ = 512
DIFF_TK = 1024
FFT_CC = 8
CONV_CHAINS = 4
KF_CHAINS = 2
FILT_TM = 512

NEG_BIG = -1e30
LOG2E = math.log2(math.e)
KV_TILES_PER_ITER = 8


def _params(n_axes):
    return pltpu.CompilerParams(dimension_semantics=("arbitrary",) * n_axes,
                                vmem_limit_bytes=VMEM_LIMIT_BYTES)


def _dot(a, b):
    return jnp.dot(a, b, preferred_element_type=F32)


def _split_bf16(x):
    hi = x.astype(BF16)
    return hi, (x - hi.astype(F32)).astype(BF16)


def _dot3(a, b):
    ah, al = _split_bf16(a)
    bh, bl = _split_bf16(b)
    return _dot(ah, bh) + (_dot(ah, bl) + _dot(al, bh))


def _rms(x, g, eps):
    ms = jnp.mean(x * x, axis=-1, keepdims=True)
    return x * lax.rsqrt(ms + eps) * g


def _ffn_body(*refs, chunks, final):
    if final:
        x_ref, g_ref, w_in_ref, w_out_ref, gf_ref, o_ref = refs
    else:
        x_ref, g_ref, w_in_ref, w_out_ref, o_ref = refs
    d_ff = w_out_ref.shape[0]
    x = x_ref[...]
    h = _rms(x, g_ref[...], NORM_EPS).astype(BF16)
    acc = None
    for a, b in chunks:
        gate = _dot(h, w_in_ref[:, a:b])
        up = _dot(h, w_in_ref[:, d_ff + a:d_ff + b])
        part = _dot((gate * jax.nn.sigmoid(gate) * up).astype(BF16), w_out_ref[a:b, :])
        acc = part if acc is None else acc + part
    y = x + 0.5 * acc
    if final:
        y = _rms(y, gf_ref[...], NORM_EPS)
    o_ref[...] = y


def _ffn_chunks(d_ff):
    n_blk = d_ff // MXU_WIDTH
    if d_ff % MXU_WIDTH or n_blk < 2:
        return ((0, d_ff),)
    cut = (n_blk + 1) // 2 * MXU_WIDTH
    return ((0, cut), (cut, d_ff))


def _ffn(x, norm_g, w_in, w_out, final_g=None):
    t, d = x.shape
    d_ff = w_out.shape[0]
    tm = min(FFN_TM, t)
    final = final_g is not None
    const = lambda i: (0, 0)
    in_specs = [
        pl.BlockSpec((tm, d), lambda i: (i, 0)),
        pl.BlockSpec((1, d), const),
        pl.BlockSpec(w_in.shape, const, pipeline_mode=pl.Buffered(1)),
        pl.BlockSpec(w_out.shape, const, pipeline_mode=pl.Buffered(1)),
    ]
    args = [x, norm_g.reshape(1, d), w_in, w_out]
    if final:
        in_specs.append(pl.BlockSpec((1, d), const))
        args.append(final_g.reshape(1, d))
    return pl.pallas_call(
        functools.partial(_ffn_body, chunks=_ffn_chunks(d_ff), final=final),
        grid=(t // tm,),
        in_specs=in_specs,
        out_specs=pl.BlockSpec((tm, d), lambda i: (i, 0)),
        out_shape=jax.ShapeDtypeStruct((t, d), F32),
        compiler_params=_params(1),
        name="ffn",
    )(*args)


def _head_rms(x, bd):
    x2 = x * x
    hi = x2.astype(BF16)
    lo = (x2 - hi.astype(F32)).astype(BF16)
    ms = _dot(hi, bd) + _dot(lo, bd)
    return x * lax.rsqrt(ms + NORM_EPS)


def _rope(x, cos, sin_signed):
    lane = lax.broadcasted_iota(jnp.int32, x.shape, 1)
    partner = jnp.where((lane % HEAD_DIM) < HEAD_DIM // 2,
                        pltpu.roll(x, LANES - HEAD_DIM // 2, 1),
                        pltpu.roll(x, HEAD_DIM // 2, 1))
    return x * cos + partner * sin_signed


def _short_conv_gate(u, before, after, w_ref, b_ref, z_ref, x0_ref):
    tm, d_hy = u.shape[0], u.shape[1] // 3
    rows = lax.broadcasted_iota(jnp.int32, u.shape, 0)
    up = jnp.where(rows == 0, before, pltpu.roll(u, 1, 0))
    dn = jnp.where(rows == tm - 1, after, pltpu.roll(u, tm - 1, 0))
    y = up * w_ref[0:1, :] + u * w_ref[1:2, :] + dn * w_ref[2:3, :] + b_ref[...]
    x0_ref[...] = y[:, :d_hy]
    z_ref[...] = y[:, 2 * d_hy:] * y[:, d_hy:2 * d_hy]


def _inproj_body(x_ref, xp_ref, xn_ref, g_ref, w_ref, wqkv_ref, cw_ref, cb_ref, cos_ref, sin_ref, bd_ref,
                 qn_ref, kn_ref, z_ref, x0_ref, gq_ref, gk_ref, gv_ref, dq_ref, dk_ref, dv_ref, gate_ref,
                 *, offs, seq_len):
    o_hy, o_gq, o_gk, o_gv, o_dq, o_dk, o_dv, o_gate, o_end = offs
    g = g_ref[...]
    h = _rms(x_ref[...], g, NORM_EPS).astype(BF16)

    def seg(a, b):
        return _dot(h, w_ref[:, a:b])

    tm, halo = x_ref.shape[0], xp_ref.shape[0]
    pos = (pl.program_id(0) * tm) % seq_len
    h_halo = [_rms(r[...], g, NORM_EPS).astype(BF16) for r in (xp_ref, xn_ref)]
    u = _dot(jnp.concatenate([h] + h_halo, axis=0), w_ref[:, o_hy:o_gq])
    before = jnp.where(pos == 0, 0.0, u[tm + halo - 1:tm + halo, :])
    after = jnp.where(pos + tm == seq_len, 0.0, u[tm + halo:tm + halo + 1, :])
    _short_conv_gate(u[:tm], before, after, cw_ref, cb_ref, z_ref, x0_ref)

    cos = cos_ref[...]
    sin = sin_ref[...]
    bd = bd_ref[...]
    scale = HEAD_DIM ** -0.5 * LOG2E
    qkv = _dot(h, wqkv_ref[...])
    wq, wk = o_gk - o_gq, o_gv - o_gk
    for c in range(wq // LANES):
        y = _head_rms(qkv[:, c * LANES:(c + 1) * LANES], bd) * qn_ref[:, c * LANES:(c + 1) * LANES]
        gq_ref[:, c * LANES:(c + 1) * LANES] = (_rope(y, cos, sin) * scale).astype(BF16)
    for c in range(wk // LANES):
        y = _head_rms(qkv[:, wq + c * LANES:wq + (c + 1) * LANES], bd) * kn_ref[:, c * LANES:(c + 1) * LANES]
        gk_ref[:, c * LANES:(c + 1) * LANES] = _rope(y, cos, sin).astype(BF16)
    gv_ref[...] = qkv[:, wq + wk:].astype(BF16)
    dq_ref[...] = (seg(o_dq, o_dk) * scale).astype(BF16)
    dk_ref[...] = seg(o_dk, o_dv).astype(BF16)
    dv_ref[...] = seg(o_dv, o_gate).astype(BF16)
    d = x_ref.shape[1]
    for c in range((o_end - o_gate) // d):
        gate_ref[:, c * d:(c + 1) * d] = jax.nn.sigmoid(seg(o_gate + c * d, o_gate + (c + 1) * d)).astype(BF16)


def _inproj(x, norm_g, w, wqkv, conv_w, conv_b, cos_t, sin_t, bd, qn, kn, offs, seq_len):
    t, d = x.shape
    tm = min(PROJ_TM, seq_len)
    d_hy = (offs[1] - offs[0]) // 3
    widths = [d_hy, d_hy] + [offs[i + 1] - offs[i] for i in range(1, 8)]
    halo = 2 * SUBLANES
    n_halo = t // halo

    def pos_map(i):
        return (i % (seq_len // tm), 0)

    full = lambda i: (0, 0)
    row = lambda i: (i, 0)
    out_dtypes = [F32, F32] + [BF16] * 7
    return pl.pallas_call(
        functools.partial(_inproj_body, offs=tuple(offs), seq_len=seq_len),
        grid=(t // tm,),
        in_specs=[
            pl.BlockSpec((tm, d), row),
            pl.BlockSpec((halo, d), lambda i: (jnp.maximum(i * (tm // halo) - 1, 0), 0)),
            pl.BlockSpec((halo, d), lambda i: (jnp.minimum((i + 1) * (tm // halo), n_halo - 1), 0)),
            pl.BlockSpec((1, d), full),
            pl.BlockSpec(w.shape, full, pipeline_mode=pl.Buffered(1)),
            pl.BlockSpec(wqkv.shape, full, pipeline_mode=pl.Buffered(1)),
            pl.BlockSpec(conv_w.shape, full),
            pl.BlockSpec((1, conv_b.shape[0]), full),
            pl.BlockSpec((tm, LANES), pos_map),
            pl.BlockSpec((tm, LANES), pos_map),
            pl.BlockSpec(bd.shape, full),
            pl.BlockSpec(qn.shape, full),
            pl.BlockSpec(kn.shape, full),
        ],
        out_specs=[pl.BlockSpec((tm, wd), row) for wd in widths],
        out_shape=[jax.ShapeDtypeStruct((t, wd), dt) for wd, dt in zip(widths, out_dtypes)],
        compiler_params=_params(1),
        name="inproj",
    )(x, x, x, norm_g.reshape(1, d), w, wqkv, conv_w, conv_b.reshape(1, -1), cos_t, sin_t, bd, qn, kn)


def _filter_body(f_ref, w1_ref, b1_ref, w2_ref, b2_ref, w3_ref, fr_ref, dl_ref, o_ref, *, n_feat):
    half = f_ref.shape[0] // 2
    d_hy = o_ref.shape[1]
    halves = (f_ref[:half, :], f_ref[half:, :])
    fr = fr_ref[...]
    h = jnp.sin(fr * (_dot3(jnp.concatenate(halves, axis=1), w1_ref[...]) + b1_ref[...]))
    h = jnp.sin(fr * (_dot3(h, w2_ref[...]) + b2_ref[...]))
    h = _dot3(h, w3_ref[0])
    for k, f in enumerate(halves):
        decay = jnp.exp(-f[:, 0:1] * dl_ref[...])
        o_ref[k * half:(k + 1) * half, :] = h[:, k * d_hy:(k + 1) * d_hy] * decay * f[:, n_feat:n_feat + 1]


def _hyena_filter(feats, w1p, b1, w2, b2, w3, freq, deltas_abs, seq_len, n_feat):
    n2l, fw = feats.shape
    d_hy = w3.shape[1] // 2
    hid = w2.shape[0]
    tm = min(FILT_TM, seq_len)
    nbl = seq_len // tm
    full = lambda i: (0, 0)
    twice = lambda m: jnp.kron(jnp.eye(2, dtype=m.dtype), m)
    pair = lambda v: jnp.tile(v, 2).reshape(1, -1)
    return pl.pallas_call(
        functools.partial(_filter_body, n_feat=n_feat),
        grid=(n2l // tm,),
        in_specs=[
            pl.BlockSpec((tm, fw), lambda i: (i, 0)),
            pl.BlockSpec((2 * fw, 2 * hid), full),
            pl.BlockSpec((1, 2 * hid), full),
            pl.BlockSpec((2 * hid, 2 * hid), full),
            pl.BlockSpec((1, 2 * hid), full),
            pl.BlockSpec((1, 2 * hid, 2 * d_hy), lambda i: (i // nbl, 0, 0)),
            pl.BlockSpec((1, 2 * hid), full),
            pl.BlockSpec((1, d_hy), full),
        ],
        out_specs=pl.BlockSpec((tm, d_hy), lambda i: (i, 0)),
        out_shape=jax.ShapeDtypeStruct((n2l, d_hy), F32),
        compiler_params=_params(1),
        name="hyena_filter",
    )(feats, twice(w1p), pair(b1), twice(w2), pair(b2),
      jnp.stack([twice(w3[:, :d_hy]), twice(w3[:, d_hy:])]), pair(freq), deltas_abs.reshape(1, d_hy))


def _fwd_spectra(f1, chunks, twr, twi, w2f, cc):
    n1 = twr.shape[0]
    a = [_dot3(f1, x) for x in chunks]
    b = []
    for aj in a:
        rows = []
        for c in range(cc):
            ar = aj[:n1, c * LANES:(c + 1) * LANES]
            ai = aj[n1:, c * LANES:(c + 1) * LANES]
            rows.append(jnp.concatenate([ar * twr - ai * twi, ar * twi + ai * twr], axis=1))
        b.append(jnp.concatenate(rows, axis=0))
    return [_dot3(bj, w2f) for bj in b]


def _kf_body(f1_ref, x_ref, twr_ref, twi_ref, w2f_ref, o_ref, *, cc, n_sub, scale):
    n1 = twr_ref.shape[0]
    w = cc * LANES
    xs = _fwd_spectra(f1_ref[...], [x_ref[:, j * w:(j + 1) * w] for j in range(n_sub)],
                      twr_ref[...], twi_ref[...], w2f_ref[...], cc)
    for j, xj in enumerate(xs):
        for c in range(cc):
            lanes = slice(j * w + c * LANES, j * w + (c + 1) * LANES)
            o_ref[0, :, lanes] = xj[c * n1:(c + 1) * n1, :LANES] * scale
            o_ref[1, :, lanes] = xj[c * n1:(c + 1) * n1, LANES:] * scale


def _conv_body(f1_ref, finv_ref, z_ref, x0_ref, kf_ref, skip_ref, twr_ref, twi_ref, w2f_ref, w2i_ref,
               o_ref, *, cc, n_sub):
    n1 = twr_ref.shape[0]
    w = cc * LANES
    twr, twi = twr_ref[...], twi_ref[...]
    cols = [slice(j * w, (j + 1) * w) for j in range(n_sub)]
    zs = [z_ref[0, :, c] for c in cols]
    xs = _fwd_spectra(f1_ref[...], zs, twr, twi, w2f_ref[...], cc)
    ys = []
    for xj, cj in zip(xs, cols):
        xr, xi = xj[:, :LANES], xj[:, LANES:]
        kr = jnp.concatenate([kf_ref[0, :, cj][:, c * LANES:(c + 1) * LANES] for c in range(cc)], axis=0)
        ki = jnp.concatenate([kf_ref[1, :, cj][:, c * LANES:(c + 1) * LANES] for c in range(cc)], axis=0)
        ys.append(jnp.concatenate([xr * kr - xi * ki, xr * ki + xi * kr], axis=1))
    w2i = w2i_ref[...]
    cs = [_dot3(yj, w2i) for yj in ys]
    ds = []
    for cj in cs:
        d_re, d_im = [], []
        for c in range(cc):
            cr = cj[c * n1:(c + 1) * n1, :LANES]
            ci = cj[c * n1:(c + 1) * n1, LANES:]
            d_re.append(cr * twr + ci * twi)
            d_im.append(ci * twr - cr * twi)
        ds.append(jnp.concatenate([jnp.concatenate(d_re, axis=1), jnp.concatenate(d_im, axis=1)], axis=0))
    finv = finv_ref[...]
    convs = [_dot3(finv, dj) for dj in ds]
    for conv, z, cj in zip(convs, zs, cols):
        o_ref[0, :, cj] = (x0_ref[0, :, cj] * (conv + z * skip_ref[:, cj])).astype(o_ref.dtype)


def _dft_tables(seq_len):
    n = 2 * seq_len
    n1 = n // LANES
    n1h = n1 // 2
    n1r = n1h + SUBLANES
    k1 = np.arange(n1r)[:, None]
    kept = (k1 <= n1h).astype(np.float64)
    ang1 = 2.0 * np.pi * k1 * np.arange(n1)[None, :] / n1
    f1 = np.concatenate([kept * np.cos(ang1), -kept * np.sin(ang1)], axis=0)
    weight = kept * np.where((k1 == 0) | (k1 == n1h), 1.0, 2.0)
    finv = np.concatenate([(weight * np.cos(ang1[:, :n1h])).T,
                           (-weight * np.sin(ang1[:, :n1h])).T], axis=1)
    ang2 = 2.0 * np.pi * np.arange(LANES)[:, None] * np.arange(LANES)[None, :] / LANES
    f2r, f2i = np.cos(ang2), -np.sin(ang2)
    w2f = np.block([[f2r, f2i], [-f2i, f2r]])
    w2i = np.block([[f2r, -f2i], [f2i, f2r]])
    angt = 2.0 * np.pi * k1 * np.arange(LANES)[None, :] / n
    f = lambda a: jnp.asarray(a, F32)
    return dict(n=n, n1=n1, n1h=n1h, n1r=n1r, f1_full=f(f1), f1_half=f(f1[:, :n1h]), finv=f(finv),
                w2f=f(w2f), w2i=f(w2i), twr=f(np.cos(angt)), twi=f(-np.sin(angt)))


def _to_blocked(x, nseq):
    t, c = x.shape
    r = t // (nseq * LANES)
    return x.reshape(nseq, r, LANES, c).transpose(0, 1, 3, 2).reshape(nseq, r, c * LANES)


def _from_blocked(y, c):
    nseq, r, _ = y.shape
    return y.reshape(nseq, r, c, LANES).transpose(0, 1, 3, 2).reshape(nseq * r * LANES, c)


def _filter_features(seq_len, fw):
    t = jnp.linspace(0.0, 1.0, seq_len, dtype=F32)[:, None]
    band = jnp.linspace(1e-4, POS_BANDS - 1, POS_BANDS, dtype=F32)
    ang = (2.0 * math.pi / seq_len) * jnp.arange(seq_len, dtype=F32)[:, None] * band[None, :]
    feats = jnp.concatenate([t, jnp.cos(ang), -jnp.sin(ang)], axis=-1)
    n_feat = feats.shape[1]
    rows = jnp.concatenate([feats, feats[-1:], feats[:0:-1]], axis=0)
    valid = jnp.asarray((np.arange(2 * seq_len) != seq_len).astype(np.float32))[:, None]
    rows = jnp.concatenate([rows, valid], axis=1)
    return jnp.pad(rows, ((0, 0), (0, fw - n_feat - 1))), n_feat


def _hyena_kf(seq_len, tabs, w1, b1, w2, b2, w3, freq):
    d_hy = w3.shape[1] // 2
    feats, n_feat = _filter_features(seq_len, LANES)
    w1p = jnp.pad(w1, ((0, LANES - w1.shape[0]), (0, 0)))
    max_decay = math.log(DECAY_TARGET) / FAST_DECAY_PCT
    min_decay = math.log(DECAY_TARGET) / SLOW_DECAY_PCT
    deltas = jnp.abs(jnp.linspace(min_decay, max_decay, d_hy, dtype=F32))
    kern = _hyena_filter(feats, w1p, b1, w2, b2, w3, freq, deltas, seq_len, n_feat)
    n, n1, n1r = tabs["n"], tabs["n1"], tabs["n1r"]
    cc = FFT_CC
    n_sub = KF_CHAINS
    w = n_sub * cc * LANES
    full = lambda j: (0, 0)
    return pl.pallas_call(
        functools.partial(_kf_body, cc=cc, n_sub=n_sub, scale=1.0 / n),
        grid=(d_hy // (n_sub * cc),),
        in_specs=[pl.BlockSpec((2 * n1r, n1), full),
                  pl.BlockSpec((n1, w), lambda j: (0, j)),
                  pl.BlockSpec((n1r, LANES), full), pl.BlockSpec((n1r, LANES), full),
                  pl.BlockSpec((2 * LANES, 2 * LANES), full)],
        out_specs=pl.BlockSpec((2, n1r, w), lambda j: (0, 0, j)),
        out_shape=jax.ShapeDtypeStruct((2, n1r, d_hy * LANES), F32),
        compiler_params=_params(1),
        name="hyena_kf",
    )(tabs["f1_full"], _to_blocked(kern, 1)[0], tabs["twr"], tabs["twi"], tabs["w2f"])


def _hyena_longconv(z, x0, kf, skip, nseq, seq_len, tabs):
    c = z.shape[1]
    n1h, n1r = tabs["n1h"], tabs["n1r"]
    cc = FFT_CC
    n_sub = CONV_CHAINS
    w = n_sub * cc * LANES
    full = lambda b, j: (0, 0)
    rowblk = pl.BlockSpec((1, n1h, w), lambda b, j: (b, 0, j))
    y = pl.pallas_call(
        functools.partial(_conv_body, cc=cc, n_sub=n_sub),
        grid=(nseq, c // (n_sub * cc)),
        in_specs=[pl.BlockSpec((2 * n1r, n1h), full),
                  pl.BlockSpec((n1h, 2 * n1r), full),
                  rowblk, rowblk,
                  pl.BlockSpec((2, n1r, w), lambda b, j: (0, 0, j)),
                  pl.BlockSpec((1, w), lambda b, j: (0, j)),
                  pl.BlockSpec((n1r, LANES), full), pl.BlockSpec((n1r, LANES), full),
                  pl.BlockSpec((2 * LANES, 2 * LANES), full), pl.BlockSpec((2 * LANES, 2 * LANES), full)],
        out_specs=rowblk,
        out_shape=jax.ShapeDtypeStruct((nseq, n1h, c * LANES), BF16),
        compiler_params=_params(2),
        name="hyena_conv",
    )(tabs["f1_half"], tabs["finv"], _to_blocked(z, nseq), _to_blocked(x0, nseq), kf,
      jnp.repeat(skip.astype(F32), LANES).reshape(1, c * LANES),
      tabs["twr"], tabs["twi"], tabs["w2f"], tabs["w2i"])
    return _from_blocked(y, c)


def _stack_halves(q_cols, lhs_ref, tq):
    lane = lax.broadcasted_iota(jnp.int32, (tq, LANES), 1)
    low = lane < HEAD_DIM
    zero = jnp.zeros((tq, LANES), lhs_ref.dtype)
    for c, q in enumerate(q_cols):
        lhs_ref[(2 * c) * tq:(2 * c + 1) * tq, :] = jnp.where(low, q, zero)
        lhs_ref[(2 * c + 1) * tq:(2 * c + 2) * tq, :] = jnp.where(low, zero, q)


def _consume(s, v, m_scr, acc_scr, const=None):
    tk = s.shape[1]
    m_prev = m_scr[...]
    row_max = jnp.max(s, axis=1, keepdims=True)
    if const is not None:
        row_max = row_max + const
    m_new = jnp.maximum(m_prev, row_max)
    shift = m_new if const is None else m_new - const
    p = jnp.exp2(s - jnp.concatenate([shift] * (tk // LANES), axis=1)).astype(BF16)
    alpha = jnp.exp2(m_prev - m_new)
    v_ext = jnp.concatenate([v, jnp.ones_like(v)], axis=1)
    acc_scr[...] = jnp.concatenate([alpha, alpha], axis=1) * acc_scr[...] + _dot(p, v_ext)
    m_scr[...] = m_new


def _qk(lhs, k):
    return lax.dot_general(lhs, k, (((1,), (1,)), ((), ())), preferred_element_type=F32)


def _key_rows(t, tk):
    return pl.ds(t * tk if isinstance(t, int) else pl.multiple_of(t * tk, tk), tk)


def _pipelined_attention(qi, n_kv, n_special, stack_fn, score_fn, consume_fn, m_scr, acc_scr, s_scr):
    @pl.when(qi == 0)
    def _():
        stack_fn(False)
        s_scr[0][...] = score_fn(0, n_special == n_kv, False)

    m_scr[...] = jnp.full_like(m_scr, NEG_BIG)
    acc_scr[...] = jnp.zeros_like(acc_scr)
    unroll = min(KV_TILES_PER_ITER, n_kv)
    assert n_kv % unroll == 0 and n_special <= unroll

    def run(p0, last):
        for j in range(unroll):
            special = last and j >= unroll - n_special
            if last and j == unroll - 1:
                stack_fn(True)
                s_scr[(j + 1) % 2][...] = score_fn(0, n_special == n_kv, True)
            else:
                s_scr[(j + 1) % 2][...] = score_fn(p0 + j + 1, last and j + 1 >= unroll - n_special, False)
            consume_fn(s_scr[j % 2][...], p0 + j, special)

    def body(i, carry):
        run(i * unroll, False)
        return carry

    lax.fori_loop(0, n_kv // unroll - 1, body, 0)
    run(n_kv - unroll, True)
    if unroll % 2 == 1:
        s_scr[0][...] = s_scr[1][...]


def _attn_scratch(m, tk):
    return [pltpu.VMEM((2, m, LANES), BF16), pltpu.VMEM((m, LANES), F32), pltpu.VMEM((m, 2 * LANES), F32),
            pltpu.VMEM((m, tk), F32), pltpu.VMEM((m, tk), F32)]


def _gqa_body(q_ref, qn_ref, k_ref, v_ref, o_ref, lhs_scr, m_scr, acc_scr, s0_scr, s1_scr, *, tq, tk, n_kv):
    qi = pl.program_id(1)
    cur = qi % 2
    n_col = q_ref.shape[1] // LANES

    def stack(nxt):
        ref = qn_ref if nxt else q_ref
        slot = 1 - cur if nxt else cur
        _stack_halves([ref[:, c * LANES:(c + 1) * LANES] for c in range(n_col)], lhs_scr.at[slot], tq)

    def score(t, special, nxt):
        return _qk(lhs_scr[1 - cur if nxt else cur], k_ref[_key_rows(t, tk), :])

    def consume(s, t, special):
        _consume(s, v_ref[_key_rows(t, tk), :], m_scr, acc_scr)

    _pipelined_attention(qi, n_kv, 0, stack, score, consume, m_scr, acc_scr, (s0_scr, s1_scr))
    o = acc_scr[:, :LANES] / acc_scr[:, LANES:]
    low = lax.broadcasted_iota(jnp.int32, (tq, LANES), 1) < HEAD_DIM
    for c in range(n_col):
        o_ref[:, c * LANES:(c + 1) * LANES] = jnp.where(
            low, o[(2 * c) * tq:(2 * c + 1) * tq], o[(2 * c + 1) * tq:(2 * c + 2) * tq]
        ).astype(o_ref.dtype)


def _gqa(q, k, v, nseq, seq_len):
    dq = q.shape[1]
    tq = min(GQA_TQ, seq_len)
    tk = min(GQA_TK, seq_len)
    nq = seq_len // tq
    assert nq == 1 or nq % 2 == 0
    m = 2 * (dq // LANES) * tq
    return pl.pallas_call(
        functools.partial(_gqa_body, tq=tq, tk=tk, n_kv=seq_len // tk),
        grid=(nseq, nq),
        in_specs=[pl.BlockSpec((tq, dq), lambda b, i: (b * nq + i, 0)),
                  pl.BlockSpec((tq, dq), lambda b, i: (b * nq + jnp.minimum(i + 1, nq - 1), 0)),
                  pl.BlockSpec((seq_len, LANES), lambda b, i: (b, 0)),
                  pl.BlockSpec((seq_len, LANES), lambda b, i: (b, 0))],
        out_specs=pl.BlockSpec((tq, dq), lambda b, i: (b * nq + i, 0)),
        out_shape=jax.ShapeDtypeStruct((nseq * seq_len, dq), BF16),
        scratch_shapes=_attn_scratch(m, tk),
        compiler_params=_params(2),
        name="gqa_attn",
    )(q, q, k, v)


def _bias_span(tq, tk):
    u_lo = -((MAX_DISTANCE + tk - 1 + tq - 1) // tq)
    u_hi = (MAX_DISTANCE + tq - 1 + tq - 1) // tq
    return u_lo, u_hi


def _t5_bucket(rel):
    nb = N_BUCKETS // 2
    max_exact = nb // 2
    ret = jnp.where(rel > 0, nb, 0)
    n = jnp.abs(rel)
    nf = jnp.maximum(n, 1).astype(F32)
    large = max_exact + (jnp.log(nf / max_exact) / math.log(MAX_DISTANCE / max_exact)
                         * (nb - max_exact)).astype(jnp.int32)
    large = jnp.minimum(large, nb - 1)
    return ret + jnp.where(n < max_exact, n, large)


def _bias_body(tab_ref, o_ref, *, tq, tk, u_lo, width):
    h = pl.program_id(0)
    u = pl.program_id(1) + u_lo
    rel = u * tq - (tq - 1) + lax.broadcasted_iota(jnp.int32, (SUBLANES, width), 1)
    bucket = _t5_bucket(rel)
    profile = jnp.zeros((SUBLANES, width), F32)
    for b in range(N_BUCKETS):
        profile = jnp.where(bucket == b, tab_ref[b, h], profile)
    rows = jnp.broadcast_to(profile[0:1, :], (tq, width))
    tile = pltpu.roll(rows, width - (tq - 1), 1, stride=1, stride_axis=0)
    o_ref[0, 0] = tile[:, :tk] * LOG2E


def _bias_tiles(rel_bias, tq, tk):
    u_lo, u_hi = _bias_span(tq, tk)
    n_off = u_hi - u_lo + 1
    n_heads = rel_bias.shape[1]
    width = pl.next_power_of_2(tq + tk - 1)
    return pl.pallas_call(
        functools.partial(_bias_body, tq=tq, tk=tk, u_lo=u_lo, width=width),
        grid=(n_heads, n_off),
        in_specs=[pl.BlockSpec(memory_space=pltpu.SMEM)],
        out_specs=pl.BlockSpec((1, 1, tq, tk), lambda h, u: (h, u, 0, 0)),
        out_shape=jax.ShapeDtypeStruct((n_heads, n_off, tq, tk), F32),
        compiler_params=_params(2),
        name="t5_bias",
    )(rel_bias.astype(F32))


def _near_count(tq, tk, u_lo, u_hi):
    r = tk // tq
    return max((qi + u_hi - 1) // r - (qi + u_lo) // r for qi in range(r))


def _diff_body(lam_ref, far_ref, q_ref, qn_ref, k_ref, v_ref, bias_ref, g_ref, o_ref,
               lhs_scr, m_scr, acc_scr, s0_scr, s1_scr, *, tq, tk, n_kv, n_near, u_lo, u_hi, out_scale):
    h = pl.program_id(0)
    qi = pl.program_id(2)
    cur = qi % 2
    r = tk // tq

    def stack(nxt):
        _stack_halves([(qn_ref if nxt else q_ref)[...]], lhs_scr.at[1 - cur if nxt else cur], tq)

    def first_near(q_idx):
        return jnp.clip((q_idx + u_lo) // r + 1, 0, n_kv - n_near)

    def tile_of(p, near, q_idx):
        t_a = first_near(q_idx)
        if near:
            return t_a + (p - (n_kv - n_near))
        return jnp.where(p < t_a, p, p + n_near)

    def score(p, near, nxt):
        q_idx = qi + 1 if nxt else qi
        t = tile_of(p, near, q_idx)
        s = _qk(lhs_scr[1 - cur if nxt else cur], k_ref[_key_rows(t, tk), :])
        if not near:
            return s
        u = jnp.clip(t * r - q_idx, u_lo, u_hi) - u_lo
        return (s.reshape(2, tq, tk) + bias_ref[0, u][None]).reshape(2 * tq, tk)

    def consume(s, p, near):
        t = tile_of(p, near, qi)
        v = v_ref[_key_rows(t, tk), :]
        if near:
            _consume(s, v, m_scr, acc_scr)
        else:
            _consume(s, v, m_scr, acc_scr, const=jnp.where(t < first_near(qi), far_ref[0, h], far_ref[1, h]))

    _pipelined_attention(qi, n_kv, n_near, stack, score, consume, m_scr, acc_scr, (s0_scr, s1_scr))
    o = acc_scr[:, :LANES] / acc_scr[:, LANES:]
    o = o[:tq] - lam_ref[0] * o[tq:]
    o_ref[...] = (_rms(o, g_ref[...], DIFF_SUBLN_EPS) * out_scale).astype(o_ref.dtype)


def _diff(lam, far, q, k, v, bias, subln_g, nseq, seq_len, lam_init, tq, tk):
    n_heads = q.shape[1] // LANES
    nq = seq_len // tq
    assert nq == 1 or nq % 2 == 0
    u_lo, u_hi = _bias_span(tq, tk)
    n_off = u_hi - u_lo + 1
    kv_spec = pl.BlockSpec((seq_len, LANES), lambda h, b, i: (b, h))
    n_kv = seq_len // tk
    n_near = min(_near_count(tq, tk, u_lo, u_hi), n_kv)
    return pl.pallas_call(
        functools.partial(_diff_body, tq=tq, tk=tk, n_kv=n_kv, n_near=n_near, u_lo=u_lo, u_hi=u_hi,
                          out_scale=1.0 - lam_init),
        grid=(n_heads, nseq, nq),
        in_specs=[pl.BlockSpec(memory_space=pltpu.SMEM),
                  pl.BlockSpec(memory_space=pltpu.SMEM),
                  pl.BlockSpec((tq, LANES), lambda h, b, i: (b * nq + i, h)),
                  pl.BlockSpec((tq, LANES), lambda h, b, i: (b * nq + jnp.minimum(i + 1, nq - 1), h)),
                  kv_spec, kv_spec,
                  pl.BlockSpec((1, n_off, tq, tk), lambda h, b, i: (h, 0, 0, 0),
                               pipeline_mode=pl.Buffered(1)),
                  pl.BlockSpec((1, LANES), lambda h, b, i: (0, 0))],
        out_specs=pl.BlockSpec((tq, LANES), lambda h, b, i: (b * nq + i, h)),
        out_shape=jax.ShapeDtypeStruct((nseq * seq_len, q.shape[1]), BF16),
        scratch_shapes=_attn_scratch(2 * tq, tk),
        compiler_params=_params(3),
        name="diff_attn",
    )(lam, far, q, q, k, v, bias, subln_g.reshape(1, LANES))


def _merge_body(x_ref, yh_ref, yg_ref, yd_ref, gate_ref, wb_ref, wo_ref, o_ref):
    d = x_ref.shape[1]
    merged = (gate_ref[:, 0:d].astype(F32) * _dot(yh_ref[...], wb_ref[0])
              + gate_ref[:, d:2 * d].astype(F32) * _dot(yg_ref[...], wb_ref[1])
              + gate_ref[:, 2 * d:3 * d].astype(F32) * _dot(yd_ref[...], wb_ref[2]))
    o_ref[...] = x_ref[...] + _dot(merged.astype(BF16), wo_ref[...])


def _merge(x, y_hy, y_gqa, y_diff, gates, wb, wo):
    t, d = x.shape
    db = y_hy.shape[1]
    tm = min(MERGE_TM, t)
    row = lambda i: (i, 0)
    return pl.pallas_call(
        _merge_body,
        grid=(t // tm,),
        in_specs=[pl.BlockSpec((tm, d), row),
                  pl.BlockSpec((tm, db), row), pl.BlockSpec((tm, db), row), pl.BlockSpec((tm, db), row),
                  pl.BlockSpec((tm, 3 * d), row),
                  pl.BlockSpec(wb.shape, lambda i: (0, 0, 0)),
                  pl.BlockSpec(wo.shape, lambda i: (0, 0))],
        out_specs=pl.BlockSpec((tm, d), row),
        out_shape=jax.ShapeDtypeStruct((t, d), F32),
        compiler_params=_params(1),
        name="merge",
    )(x, y_hy, y_gqa, y_diff, gates, wb, wo)


def _rope_tables(max_len):
    pos = np.arange(max_len)
    half = HEAD_DIM // 2
    inv = ROPE_THETA ** (-jnp.arange(0, half, 2, dtype=F32) / half)
    row = jnp.asarray(pos // GRID_W, F32)
    col = jnp.asarray(pos % GRID_W, F32)
    ang = jnp.concatenate([row[:, None] * inv, col[:, None] * inv], axis=-1)
    cos, sin = jnp.cos(ang), jnp.sin(ang)
    cos_h = jnp.concatenate([cos, cos], axis=1)
    sin_h = jnp.concatenate([-sin, sin], axis=1)
    return jnp.tile(cos_h, (1, LANES // HEAD_DIM)), jnp.tile(sin_h, (1, LANES // HEAD_DIM))


def _column_layout(d_model, d_hy):
    d_gqa = N_Q_HEADS * HEAD_DIM
    d_kv = N_KV_HEADS * HEAD_DIM
    d_diff = N_DIFF_HEADS * 2 * HEAD_DIM
    widths = [3 * d_hy, d_gqa, d_kv, d_kv, d_diff, d_diff, d_diff, 3 * d_model]
    offs = [0]
    for w in widths:
        offs.append(offs[-1] + w)
    deint = np.concatenate([np.arange(0, HEAD_DIM, 2), np.arange(1, HEAD_DIM, 2)])
    q_heads = [kv * GQA_GROUP + g for g in range(GQA_GROUP) for kv in range(N_KV_HEADS)]
    q_cols = np.concatenate([offs[1] + h * HEAD_DIM + deint for h in q_heads])
    k_cols = np.concatenate([offs[2] + h * HEAD_DIM + deint for h in range(N_KV_HEADS)])
    qkv_cols = np.concatenate([q_cols, k_cols, np.arange(offs[3], offs[4])])
    out_rows = np.concatenate([h * HEAD_DIM + np.arange(HEAD_DIM) for h in q_heads])
    return offs, qkv_cols, deint, out_rows


def kernel(x_prompt, x_sample, ffn1_norm, ffn1_w_in, ffn1_w_out, mix_norm, w_in, hy_conv_w, hy_conv_b, hy_filt_w1, hy_filt_b1, hy_filt_w2, hy_filt_b2, hy_filt_w3, hy_filt_freq, hy_skip, gqa_q_norm, gqa_k_norm, diff_lambda, diff_subln, rel_bias, w_branch, w_out, ffn2_norm, ffn2_w_in, ffn2_w_out, final_norm):
    d = x_prompt.shape[-1]
    depth = w_in.shape[0]
    d_hy = hy_skip.shape[1]
    groups = [(x.shape[0], x.shape[1]) for x in (x_prompt, x_sample)]
    xs = [x_prompt.reshape(-1, d), x_sample.reshape(-1, d)]
    lens = sorted({sl for _, sl in groups})
    assert all(sl % GRID_W == 0 for sl in lens)

    offs, qkv_cols, deint, out_rows = _column_layout(d, d_hy)
    cos_t, sin_t = _rope_tables(max(lens))
    bd = jnp.asarray(np.kron(np.eye(LANES // HEAD_DIM), np.full((HEAD_DIM, HEAD_DIM), 1.0 / HEAD_DIM)), BF16)
    tabs = {sl: _dft_tables(sl) for sl in lens}
    diff_tiles = {sl: (min(DIFF_TQ, sl), min(DIFF_TK, sl)) for sl in lens}
    bias = {tt: _bias_tiles(rel_bias, *tt) for tt in sorted(set(diff_tiles.values()))}
    far_bias = rel_bias[jnp.array([N_BUCKETS // 2 - 1, N_BUCKETS - 1])].astype(F32) * LOG2E

    for l in range(depth):
        w1_in, w1_out = ffn1_w_in[l].astype(BF16), ffn1_w_out[l].astype(BF16)
        w2_in, w2_out = ffn2_w_in[l].astype(BF16), ffn2_w_out[l].astype(BF16)
        w_l = w_in[l].astype(BF16)
        w_qkv = w_in[l][:, offs[1]:offs[4]][:, qkv_cols - offs[1]].astype(BF16)
        qn = jnp.tile(gqa_q_norm[l][deint], N_Q_HEADS).reshape(1, -1)
        kn = jnp.tile(gqa_k_norm[l][deint], N_KV_HEADS).reshape(1, -1)
        wb = jnp.stack([w_branch[l][0], w_branch[l][1][out_rows], w_branch[l][2]]).astype(BF16)
        wo = w_out[l].astype(BF16)
        lp = diff_lambda[l].astype(F32)
        lam_init = 0.8 - 0.6 * math.exp(-0.3 * l)
        lam = (jnp.exp(jnp.sum(lp[0] * lp[1])) - jnp.exp(jnp.sum(lp[2] * lp[3])) + lam_init).reshape(1)
        kfs = {sl: _hyena_kf(sl, tabs[sl], hy_filt_w1[l], hy_filt_b1[l], hy_filt_w2[l], hy_filt_b2[l],
                             hy_filt_w3[l], hy_filt_freq[l]) for sl in lens}
        final_g = final_norm if l == depth - 1 else None

        for g, (nseq, sl) in enumerate(groups):
            x = _ffn(xs[g], ffn1_norm[l], w1_in, w1_out)
            z, x0, gq, gk, gv, dq, dk, dv, gates = _inproj(x, mix_norm[l], w_l, w_qkv, hy_conv_w[l],
                                                            hy_conv_b[l], cos_t, sin_t, bd, qn, kn, offs, sl)
            y_hy = _hyena_longconv(z, x0, kfs[sl], hy_skip[l], nseq, sl, tabs[sl])
            y_gqa = _gqa(gq, gk, gv, nseq, sl)
            tq, tk = diff_tiles[sl]
            y_diff = _diff(lam, far_bias, dq, dk, dv, bias[(tq, tk)], diff_subln[l], nseq, sl, lam_init, tq, tk)
            x = _merge(x, y_hy, y_gqa, y_diff, gates, wb, wo)
            xs[g] = _ffn(x, ffn2_norm[l], w2_in, w2_out, final_g=final_g)

    return xs[0].reshape(x_prompt.shape), xs[1].reshape(x_sample.shape)
```

```python
import functools
import math

import numpy as np
import jax
import jax.numpy as jnp
from jax import lax
from jax.experimental import pallas as pl
from jax.experimental.pallas import tpu as pltpu

F32 = jnp.float32
BF16 = jnp.bfloat16

NORM_EPS = 1e-6
GRID_W = 64
HEAD_DIM = 64
N_Q_HEADS = 8
N_KV_HEADS = 2
GQA_GROUP = N_Q_HEADS // N_KV_HEADS
ROPE_THETA = 10000.0
N_DIFF_HEADS = 4
DIFF_SUBLN_EPS = 1e-5
N_BUCKETS = 32
MAX_DISTANCE = 128
POS_BANDS = 16
DECAY_TARGET = 1e-2
FAST_DECAY_PCT = 0.3
SLOW_DECAY_PCT = 1.5

LANES = 128
SUBLANES = 8
MXU_WIDTH = 256
VMEM_LIMIT_BYTES = 56 * 1024 * 1024

FFN_TM = 512
PROJ_TM = 512
GQA_TQ = 128
GQA_TK = 1024
DIFF_TQ = 512
DIFF_TK = 1024
FFT_CC = 8
CONV_CHAINS = 4
KF_CHAINS = 2
FILT_TM = 512

NEG_BIG = -1e30
LOG2E = math.log2(math.e)
KV_TILES_PER_ITER = 8


def _params(n_axes):
    return pltpu.CompilerParams(dimension_semantics=("arbitrary",) * n_axes,
                                vmem_limit_bytes=VMEM_LIMIT_BYTES)


def _dot(a, b):
    return jnp.dot(a, b, preferred_element_type=F32)


def _split_bf16(x):
    hi = x.astype(BF16)
    return hi, (x - hi.astype(F32)).astype(BF16)


def _dot3(a, b):
    ah, al = _split_bf16(a)
    bh, bl = _split_bf16(b)
    return _dot(ah, bh) + (_dot(ah, bl) + _dot(al, bh))


def _rms(x, g, eps):
    ms = jnp.mean(x * x, axis=-1, keepdims=True)
    return x * lax.rsqrt(ms + eps) * g


def _merged_residual(x_ref, yh_ref, yg_ref, yd_ref, gate_ref, wb_ref, wo_ref):
    d = x_ref.shape[1]
    merged = (gate_ref[:, 0:d].astype(F32) * _dot(yh_ref[...], wb_ref[0])
              + gate_ref[:, d:2 * d].astype(F32) * _dot(yg_ref[...], wb_ref[1])
              + gate_ref[:, 2 * d:3 * d].astype(F32) * _dot(yd_ref[...], wb_ref[2]))
    return x_ref[...] + _dot(merged.astype(BF16), wo_ref[...])


def _ffn_body(*refs, chunks, final, merge):
    refs = list(refs)
    o_ref = refs.pop()
    if merge:
        x = _merged_residual(*refs[:7])
        refs = refs[7:]
    else:
        x = refs.pop(0)[...]
    g_ref, w_in_ref, w_out_ref = refs[:3]
    d_ff = w_out_ref.shape[0]
    h = _rms(x, g_ref[...], NORM_EPS).astype(BF16)
    acc = None
    for a, b in chunks:
        gate = _dot(h, w_in_ref[:, a:b])
        up = _dot(h, w_in_ref[:, d_ff + a:d_ff + b])
        part = _dot((gate * jax.nn.sigmoid(gate) * up).astype(BF16), w_out_ref[a:b, :])
        acc = part if acc is None else acc + part
    y = x + 0.5 * acc
    if final:
        y = _rms(y, refs[3][...], NORM_EPS)
    o_ref[...] = y


def _ffn_chunks(d_ff):
    n_blk = d_ff // MXU_WIDTH
    if d_ff % MXU_WIDTH or n_blk < 2:
        return ((0, d_ff),)
    cut = (n_blk + 1) // 2 * MXU_WIDTH
    return ((0, cut), (cut, d_ff))


def _ffn(x, norm_g, w_in, w_out, final_g=None, merge=None):
    t, d = x.shape
    d_ff = w_out.shape[0]
    tm = min(FFN_TM, t)
    final = final_g is not None
    const = lambda i: (0, 0)
    row = lambda i: (i, 0)
    resident = lambda a: pl.BlockSpec(a.shape, lambda i: (0,) * a.ndim, pipeline_mode=pl.Buffered(1))
    in_specs = [pl.BlockSpec((tm, d), row)]
    args = [x]
    if merge is not None:
        y_hy, y_gqa, y_diff, gates, wb, wo = merge
        in_specs += [pl.BlockSpec((tm, y.shape[1]), row) for y in (y_hy, y_gqa, y_diff, gates)]
        in_specs += [resident(wb), resident(wo)]
        args += [y_hy, y_gqa, y_diff, gates, wb, wo]
    in_specs += [pl.BlockSpec((1, d), const), resident(w_in), resident(w_out)]
    args += [norm_g.reshape(1, d), w_in, w_out]
    if final:
        in_specs.append(pl.BlockSpec((1, d), const))
        args.append(final_g.reshape(1, d))
    return pl.pallas_call(
        functools.partial(_ffn_body, chunks=_ffn_chunks(d_ff), final=final, merge=merge is not None),
        grid=(t // tm,),
        in_specs=in_specs,
        out_specs=pl.BlockSpec((tm, d), row),
        out_shape=jax.ShapeDtypeStruct((t, d), F32),
        compiler_params=_params(1),
        name="merge_ffn" if merge is not None else "ffn",
    )(*args)


def _head_rms(x, bd):
    x2 = x * x
    hi = x2.astype(BF16)
    lo = (x2 - hi.astype(F32)).astype(BF16)
    ms = _dot(hi, bd) + _dot(lo, bd)
    return x * lax.rsqrt(ms + NORM_EPS)


def _rope(x, cos, sin_signed):
    lane = lax.broadcasted_iota(jnp.int32, x.shape, 1)
    partner = jnp.where((lane % HEAD_DIM) < HEAD_DIM // 2,
                        pltpu.roll(x, LANES - HEAD_DIM // 2, 1),
                        pltpu.roll(x, HEAD_DIM // 2, 1))
    return x * cos + partner * sin_signed


def _short_conv_gate(u, before, after, w_ref, b_ref, z_ref, x0_ref):
    tm, d_hy = u.shape[0], u.shape[1] // 3
    rows = lax.broadcasted_iota(jnp.int32, u.shape, 0)
    up = jnp.where(rows == 0, before, pltpu.roll(u, 1, 0))
    dn = jnp.where(rows == tm - 1, after, pltpu.roll(u, tm - 1, 0))
    y = up * w_ref[0:1, :] + u * w_ref[1:2, :] + dn * w_ref[2:3, :] + b_ref[...]
    x0_ref[...] = y[:, :d_hy]
    z_ref[...] = y[:, 2 * d_hy:] * y[:, d_hy:2 * d_hy]


def _inproj_body(x_ref, xp_ref, xn_ref, g_ref, w_ref, wqkv_ref, cw_ref, cb_ref, cos_ref, sin_ref, bd_ref,
                 qn_ref, kn_ref, z_ref, x0_ref, gq_ref, gk_ref, gv_ref, dq_ref, dk_ref, dv_ref, gate_ref,
                 *, offs, seq_len):
    o_hy, o_gq, o_gk, o_gv, o_dq, o_dk, o_dv, o_gate, o_end = offs
    g = g_ref[...]
    h = _rms(x_ref[...], g, NORM_EPS).astype(BF16)

    def seg(a, b):
        return _dot(h, w_ref[:, a:b])

    tm, halo = x_ref.shape[0], xp_ref.shape[0]
    pos = (pl.program_id(0) * tm) % seq_len
    h_halo = [_rms(r[...], g, NORM_EPS).astype(BF16) for r in (xp_ref, xn_ref)]
    u = _dot(jnp.concatenate([h] + h_halo, axis=0), w_ref[:, o_hy:o_gq])
    before = jnp.where(pos == 0, 0.0, u[tm + halo - 1:tm + halo, :])
    after = jnp.where(pos + tm == seq_len, 0.0, u[tm + halo:tm + halo + 1, :])
    _short_conv_gate(u[:tm], before, after, cw_ref, cb_ref, z_ref, x0_ref)

    cos = cos_ref[...]
    sin = sin_ref[...]
    bd = bd_ref[...]
    scale = HEAD_DIM ** -0.5 * LOG2E
    qkv = _dot(h, wqkv_ref[...])
    wq, wk = o_gk - o_gq, o_gv - o_gk
    for c in range(wq // LANES):
        y = _head_rms(qkv[:, c * LANES:(c + 1) * LANES], bd) * qn_ref[:, c * LANES:(c + 1) * LANES]
        gq_ref[:, c * LANES:(c + 1) * LANES] = (_rope(y, cos, sin) * scale).astype(BF16)
    for c in range(wk // LANES):
        y = _head_rms(qkv[:, wq + c * LANES:wq + (c + 1) * LANES], bd) * kn_ref[:, c * LANES:(c + 1) * LANES]
        gk_ref[:, c * LANES:(c + 1) * LANES] = _rope(y, cos, sin).astype(BF16)
    gv_ref[...] = qkv[:, wq + wk:].astype(BF16)
    dq_ref[...] = (seg(o_dq, o_dk) * scale).astype(BF16)
    dk_ref[...] = seg(o_dk, o_dv).astype(BF16)
    dv_ref[...] = seg(o_dv, o_gate).astype(BF16)
    d = x_ref.shape[1]
    for c in range((o_end - o_gate) // d):
        gate_ref[:, c * d:(c + 1) * d] = jax.nn.sigmoid(seg(o_gate + c * d, o_gate + (c + 1) * d)).astype(BF16)


def _inproj(x, norm_g, w, wqkv, conv_w, conv_b, cos_t, sin_t, bd, qn, kn, offs, seq_len):
    t, d = x.shape
    tm = min(PROJ_TM, seq_len)
    d_hy = (offs[1] - offs[0]) // 3
    widths = [d_hy, d_hy] + [offs[i + 1] - offs[i] for i in range(1, 8)]
    halo = 2 * SUBLANES
    n_halo = t // halo

    def pos_map(i):
        return (i % (seq_len // tm), 0)

    full = lambda i: (0, 0)
    row = lambda i: (i, 0)
    out_dtypes = [F32, F32] + [BF16] * 7
    return pl.pallas_call(
        functools.partial(_inproj_body, offs=tuple(offs), seq_len=seq_len),
        grid=(t // tm,),
        in_specs=[
            pl.BlockSpec((tm, d), row),
            pl.BlockSpec((halo, d), lambda i: (jnp.maximum(i * (tm // halo) - 1, 0), 0)),
            pl.BlockSpec((halo, d), lambda i: (jnp.minimum((i + 1) * (tm // halo), n_halo - 1), 0)),
            pl.BlockSpec((1, d), full),
            pl.BlockSpec(w.shape, full, pipeline_mode=pl.Buffered(1)),
            pl.BlockSpec(wqkv.shape, full, pipeline_mode=pl.Buffered(1)),
            pl.BlockSpec(conv_w.shape, full),
            pl.BlockSpec((1, conv_b.shape[0]), full),
            pl.BlockSpec((tm, LANES), pos_map),
            pl.BlockSpec((tm, LANES), pos_map),
            pl.BlockSpec(bd.shape, full),
            pl.BlockSpec(qn.shape, full),
            pl.BlockSpec(kn.shape, full),
        ],
        out_specs=[pl.BlockSpec((tm, wd), row) for wd in widths],
        out_shape=[jax.ShapeDtypeStruct((t, wd), dt) for wd, dt in zip(widths, out_dtypes)],
        compiler_params=_params(1),
        name="inproj",
    )(x, x, x, norm_g.reshape(1, d), w, wqkv, conv_w, conv_b.reshape(1, -1), cos_t, sin_t, bd, qn, kn)


def _filter_body(f_ref, w1_ref, b1_ref, w2_ref, b2_ref, w3_ref, fr_ref, dl_ref, fwd_ref, bwd_ref):
    half = f_ref.shape[0] // 2
    d_hy = fwd_ref.shape[1]
    halves = (f_ref[:half, :], f_ref[half:, :])
    fr = fr_ref[...]
    h = jnp.sin(fr * (_dot3(jnp.concatenate(halves, axis=1), w1_ref[...]) + b1_ref[...]))
    h = jnp.sin(fr * (_dot3(h, w2_ref[...]) + b2_ref[...]))
    h = _dot3(h, w3_ref[...])
    for k, f in enumerate(halves):
        rows = slice(k * half, (k + 1) * half)
        decay = jnp.exp(-f[:, 0:1] * dl_ref[...])
        fwd_ref[rows, :] = h[:, 2 * k * d_hy:(2 * k + 1) * d_hy] * decay
        bwd_ref[rows, :] = h[:, (2 * k + 1) * d_hy:(2 * k + 2) * d_hy] * decay


def _hyena_filter(feats, w1p, b1, w2, b2, w3, freq, deltas_abs):
    seq_len, fw = feats.shape
    d_hy = w3.shape[1] // 2
    hid = w2.shape[0]
    tm = min(FILT_TM, seq_len)
    full = lambda i: (0, 0)
    twice = lambda m: jnp.kron(jnp.eye(2, dtype=m.dtype), m)
    pair = lambda v: jnp.tile(v, 2).reshape(1, -1)
    out = pl.BlockSpec((tm, d_hy), lambda i: (i, 0))
    return pl.pallas_call(
        _filter_body,
        grid=(seq_len // tm,),
        in_specs=[
            pl.BlockSpec((tm, fw), lambda i: (i, 0)),
            pl.BlockSpec((2 * fw, 2 * hid), full),
            pl.BlockSpec((1, 2 * hid), full),
            pl.BlockSpec((2 * hid, 2 * hid), full),
            pl.BlockSpec((1, 2 * hid), full),
            pl.BlockSpec((2 * hid, 4 * d_hy), full),
            pl.BlockSpec((1, 2 * hid), full),
            pl.BlockSpec((1, d_hy), full),
        ],
        out_specs=[out, out],
        out_shape=[jax.ShapeDtypeStruct((seq_len, d_hy), F32)] * 2,
        compiler_params=_params(1),
        name="hyena_filter",
    )(feats, twice(w1p), pair(b1), twice(w2), pair(b2), twice(w3), pair(freq), deltas_abs.reshape(1, d_hy))


def _fwd_spectra(f1, chunks, twr, twi, w2f, cc):
    n1 = twr.shape[0]
    a = [_dot3(f1, x) for x in chunks]
    b = []
    for aj in a:
        rows = []
        for c in range(cc):
            ar = aj[:n1, c * LANES:(c + 1) * LANES]
            ai = aj[n1:, c * LANES:(c + 1) * LANES]
            rows.append(jnp.concatenate([ar * twr - ai * twi, ar * twi + ai * twr], axis=1))
        b.append(jnp.concatenate(rows, axis=0))
    return [_dot3(bj, w2f) for bj in b]


def _kf_body(f1_ref, x_ref, twr_ref, twi_ref, w2f_ref, o_ref, *, cc, n_sub, scale):
    n1 = twr_ref.shape[0]
    w = cc * LANES
    xs = _fwd_spectra(f1_ref[...], [x_ref[:, j * w:(j + 1) * w] for j in range(n_sub)],
                      twr_ref[...], twi_ref[...], w2f_ref[...], cc)
    for j, xj in enumerate(xs):
        for c in range(cc):
            lanes = slice(j * w + c * LANES, j * w + (c + 1) * LANES)
            o_ref[0, :, lanes] = xj[c * n1:(c + 1) * n1, :LANES] * scale
            o_ref[1, :, lanes] = xj[c * n1:(c + 1) * n1, LANES:] * scale


def _conv_body(f1_ref, finv_ref, z_ref, x0_ref, kf_ref, skip_ref, twr_ref, twi_ref, w2f_ref, w2i_ref,
               o_ref, *, cc, n_sub):
    n1 = twr_ref.shape[0]
    w = cc * LANES
    twr, twi = twr_ref[...], twi_ref[...]
    cols = [slice(j * w, (j + 1) * w) for j in range(n_sub)]
    zs = [z_ref[0, :, c] for c in cols]
    xs = _fwd_spectra(f1_ref[...], zs, twr, twi, w2f_ref[...], cc)
    ys = []
    for xj, cj in zip(xs, cols):
        xr, xi = xj[:, :LANES], xj[:, LANES:]
        kr = jnp.concatenate([kf_ref[0, :, cj][:, c * LANES:(c + 1) * LANES] for c in range(cc)], axis=0)
        ki = jnp.concatenate([kf_ref[1, :, cj][:, c * LANES:(c + 1) * LANES] for c in range(cc)], axis=0)
        ys.append(jnp.concatenate([xr * kr - xi * ki, xr * ki + xi * kr], axis=1))
    w2i = w2i_ref[...]
    cs = [_dot3(yj, w2i) for yj in ys]
    ds = []
    for cj in cs:
        d_re, d_im = [], []
        for c in range(cc):
            cr = cj[c * n1:(c + 1) * n1, :LANES]
            ci = cj[c * n1:(c + 1) * n1, LANES:]
            d_re.append(cr * twr + ci * twi)
            d_im.append(ci * twr - cr * twi)
        ds.append(jnp.concatenate([jnp.concatenate(d_re, axis=1), jnp.concatenate(d_im, axis=1)], axis=0))
    finv = finv_ref[...]
    convs = [_dot3(finv, dj) for dj in ds]
    for conv, z, cj in zip(convs, zs, cols):
        o_ref[0, :, cj] = (x0_ref[0, :, cj] * (conv + z * skip_ref[:, cj])).astype(o_ref.dtype)


def _dft_tables(seq_len):
    n = 2 * seq_len
    n1 = n // LANES
    n1h = n1 // 2
    n1r = n1h + SUBLANES
    k1 = np.arange(n1r)[:, None]
    kept = (k1 <= n1h).astype(np.float64)
    ang1 = 2.0 * np.pi * k1 * np.arange(n1)[None, :] / n1
    f1 = np.concatenate([kept * np.cos(ang1), -kept * np.sin(ang1)], axis=0)
    weight = kept * np.where((k1 == 0) | (k1 == n1h), 1.0, 2.0)
    finv = np.concatenate([(weight * np.cos(ang1[:, :n1h])).T,
                           (-weight * np.sin(ang1[:, :n1h])).T], axis=1)
    ang2 = 2.0 * np.pi * np.arange(LANES)[:, None] * np.arange(LANES)[None, :] / LANES
    f2r, f2i = np.cos(ang2), -np.sin(ang2)
    w2f = np.block([[f2r, f2i], [-f2i, f2r]])
    w2i = np.block([[f2r, -f2i], [f2i, f2r]])
    angt = 2.0 * np.pi * k1 * np.arange(LANES)[None, :] / n
    f = lambda a: jnp.asarray(a, F32)
    return dict(n=n, n1=n1, n1h=n1h, n1r=n1r, f1_full=f(f1), f1_half=f(f1[:, :n1h]), finv=f(finv),
                w2f=f(w2f), w2i=f(w2i), twr=f(np.cos(angt)), twi=f(-np.sin(angt)))


def _to_blocked(x, nseq):
    t, c = x.shape
    r = t // (nseq * LANES)
    return x.reshape(nseq, r, LANES, c).transpose(0, 1, 3, 2).reshape(nseq, r, c * LANES)


def _from_blocked(y, c):
    nseq, r, _ = y.shape
    return y.reshape(nseq, r, c, LANES).transpose(0, 1, 3, 2).reshape(nseq * r * LANES, c)


def _filter_features(seq_len, fw):
    t = jnp.linspace(0.0, 1.0, seq_len, dtype=F32)[:, None]
    band = jnp.linspace(1e-4, POS_BANDS - 1, POS_BANDS, dtype=F32)
    ang = (2.0 * math.pi / seq_len) * jnp.arange(seq_len, dtype=F32)[:, None] * band[None, :]
    feats = jnp.concatenate([t, jnp.cos(ang), -jnp.sin(ang)], axis=-1)
    return jnp.pad(feats, ((0, 0), (0, fw - feats.shape[1])))


def _hyena_kf(seq_len, tabs, w1, b1, w2, b2, w3, freq):
    d_hy = w3.shape[1] // 2
    feats = _filter_features(seq_len, LANES)
    w1p = jnp.pad(w1, ((0, LANES - w1.shape[0]), (0, 0)))
    max_decay = math.log(DECAY_TARGET) / FAST_DECAY_PCT
    min_decay = math.log(DECAY_TARGET) / SLOW_DECAY_PCT
    deltas = jnp.abs(jnp.linspace(min_decay, max_decay, d_hy, dtype=F32))
    fwd, bwd = _hyena_filter(feats, w1p, b1, w2, b2, w3, freq, deltas)
    kern = jnp.concatenate([fwd, jnp.zeros((1, d_hy), F32), bwd[:0:-1]], axis=0)
    n, n1, n1r = tabs["n"], tabs["n1"], tabs["n1r"]
    cc = FFT_CC
    n_sub = KF_CHAINS
    w = n_sub * cc * LANES
    full = lambda j: (0, 0)
    return pl.pallas_call(
        functools.partial(_kf_body, cc=cc, n_sub=n_sub, scale=1.0 / n),
        grid=(d_hy // (n_sub * cc),),
        in_specs=[pl.BlockSpec((2 * n1r, n1), full),
                  pl.BlockSpec((n1, w), lambda j: (0, j)),
                  pl.BlockSpec((n1r, LANES), full), pl.BlockSpec((n1r, LANES), full),
                  pl.BlockSpec((2 * LANES, 2 * LANES), full)],
        out_specs=pl.BlockSpec((2, n1r, w), lambda j: (0, 0, j)),
        out_shape=jax.ShapeDtypeStruct((2, n1r, d_hy * LANES), F32),
        compiler_params=_params(1),
        name="hyena_kf",
    )(tabs["f1_full"], _to_blocked(kern, 1)[0], tabs["twr"], tabs["twi"], tabs["w2f"])


def _hyena_longconv(z, x0, kf, skip, nseq, seq_len, tabs):
    c = z.shape[1]
    n1h, n1r = tabs["n1h"], tabs["n1r"]
    cc = FFT_CC
    n_sub = CONV_CHAINS
    w = n_sub * cc * LANES
    full = lambda b, j: (0, 0)
    rowblk = pl.BlockSpec((1, n1h, w), lambda b, j: (b, 0, j))
    y = pl.pallas_call(
        functools.partial(_conv_body, cc=cc, n_sub=n_sub),
        grid=(nseq, c // (n_sub * cc)),
        in_specs=[pl.BlockSpec((2 * n1r, n1h), full),
                  pl.BlockSpec((n1h, 2 * n1r), full),
                  rowblk, rowblk,
                  pl.BlockSpec((2, n1r, w), lambda b, j: (0, 0, j)),
                  pl.BlockSpec((1, w), lambda b, j: (0, j)),
                  pl.BlockSpec((n1r, LANES), full), pl.BlockSpec((n1r, LANES), full),
                  pl.BlockSpec((2 * LANES, 2 * LANES), full), pl.BlockSpec((2 * LANES, 2 * LANES), full)],
        out_specs=rowblk,
        out_shape=jax.ShapeDtypeStruct((nseq, n1h, c * LANES), BF16),
        compiler_params=_params(2),
        name="hyena_conv",
    )(tabs["f1_half"], tabs["finv"], _to_blocked(z, nseq), _to_blocked(x0, nseq), kf,
      jnp.repeat(skip.astype(F32), LANES).reshape(1, c * LANES),
      tabs["twr"], tabs["twi"], tabs["w2f"], tabs["w2i"])
    return _from_blocked(y, c)


def _stack_halves(q_cols, lhs_ref, tq):
    lane = lax.broadcasted_iota(jnp.int32, (tq, LANES), 1)
    low = lane < HEAD_DIM
    zero = jnp.zeros((tq, LANES), lhs_ref.dtype)
    for c, q in enumerate(q_cols):
        lhs_ref[(2 * c) * tq:(2 * c + 1) * tq, :] = jnp.where(low, q, zero)
        lhs_ref[(2 * c + 1) * tq:(2 * c + 2) * tq, :] = jnp.where(low, zero, q)


def _consume(s, v, m_scr, acc_scr, const=None):
    tk = s.shape[1]
    m_prev = m_scr[...]
    row_max = jnp.max(s, axis=1, keepdims=True)
    if const is not None:
        row_max = row_max + const
    m_new = jnp.maximum(m_prev, row_max)
    shift = m_new if const is None else m_new - const
    p = jnp.exp2(s - jnp.concatenate([shift] * (tk // LANES), axis=1)).astype(BF16)
    alpha = jnp.exp2(m_prev - m_new)
    v_ext = jnp.concatenate([v, jnp.ones_like(v)], axis=1)
    acc_scr[...] = jnp.concatenate([alpha, alpha], axis=1) * acc_scr[...] + _dot(p, v_ext)
    m_scr[...] = m_new


def _qk(lhs, k):
    return lax.dot_general(lhs, k, (((1,), (1,)), ((), ())), preferred_element_type=F32)


def _key_rows(t, tk):
    return pl.ds(t * tk if isinstance(t, int) else pl.multiple_of(t * tk, tk), tk)


def _pipelined_attention(qi, n_kv, n_special, stack_fn, score_fn, consume_fn, m_scr, acc_scr, s_scr):
    @pl.when(qi == 0)
    def _():
        stack_fn(False)
        s_scr[0][...] = score_fn(0, n_special == n_kv, False)

    m_scr[...] = jnp.full_like(m_scr, NEG_BIG)
    acc_scr[...] = jnp.zeros_like(acc_scr)
    unroll = min(KV_TILES_PER_ITER, n_kv)
    assert n_kv % unroll == 0 and n_special <= unroll

    def run(p0, last):
        for j in range(unroll):
            special = last and j >= unroll - n_special
            if last and j == unroll - 1:
                stack_fn(True)
                s_scr[(j + 1) % 2][...] = score_fn(0, n_special == n_kv, True)
            else:
                s_scr[(j + 1) % 2][...] = score_fn(p0 + j + 1, last and j + 1 >= unroll - n_special, False)
            consume_fn(s_scr[j % 2][...], p0 + j, special)

    def body(i, carry):
        run(i * unroll, False)
        return carry

    lax.fori_loop(0, n_kv // unroll - 1, body, 0)
    run(n_kv - unroll, True)
    if unroll % 2 == 1:
        s_scr[0][...] = s_scr[1][...]


def _attn_scratch(m, tk):
    return [pltpu.VMEM((2, m, LANES), BF16), pltpu.VMEM((m, LANES), F32), pltpu.VMEM((m, 2 * LANES), F32),
            pltpu.VMEM((m, tk), F32), pltpu.VMEM((m, tk), F32)]


def _gqa_body(q_ref, qn_ref, k_ref, v_ref, o_ref, lhs_scr, m_scr, acc_scr, s0_scr, s1_scr, *, tq, tk, n_kv):
    qi = pl.program_id(1)
    cur = qi % 2
    n_col = q_ref.shape[1] // LANES

    def stack(nxt):
        ref = qn_ref if nxt else q_ref
        slot = 1 - cur if nxt else cur
        _stack_halves([ref[:, c * LANES:(c + 1) * LANES] for c in range(n_col)], lhs_scr.at[slot], tq)

    def score(t, special, nxt):
        return _qk(lhs_scr[1 - cur if nxt else cur], k_ref[_key_rows(t, tk), :])

    def consume(s, t, special):
        _consume(s, v_ref[_key_rows(t, tk), :], m_scr, acc_scr)

    _pipelined_attention(qi, n_kv, 0, stack, score, consume, m_scr, acc_scr, (s0_scr, s1_scr))
    o = acc_scr[:, :LANES] / acc_scr[:, LANES:]
    low = lax.broadcasted_iota(jnp.int32, (tq, LANES), 1) < HEAD_DIM
    for c in range(n_col):
        o_ref[:, c * LANES:(c + 1) * LANES] = jnp.where(
            low, o[(2 * c) * tq:(2 * c + 1) * tq], o[(2 * c + 1) * tq:(2 * c + 2) * tq]
        ).astype(o_ref.dtype)


def _gqa(q, k, v, nseq, seq_len):
    dq = q.shape[1]
    tq = min(GQA_TQ, seq_len)
    tk = min(GQA_TK, seq_len)
    nq = seq_len // tq
    assert nq == 1 or nq % 2 == 0
    m = 2 * (dq // LANES) * tq
    return pl.pallas_call(
        functools.partial(_gqa_body, tq=tq, tk=tk, n_kv=seq_len // tk),
        grid=(nseq, nq),
        in_specs=[pl.BlockSpec((tq, dq), lambda b, i: (b * nq + i, 0)),
                  pl.BlockSpec((tq, dq), lambda b, i: (b * nq + jnp.minimum(i + 1, nq - 1), 0)),
                  pl.BlockSpec((seq_len, LANES), lambda b, i: (b, 0)),
                  pl.BlockSpec((seq_len, LANES), lambda b, i: (b, 0))],
        out_specs=pl.BlockSpec((tq, dq), lambda b, i: (b * nq + i, 0)),
        out_shape=jax.ShapeDtypeStruct((nseq * seq_len, dq), BF16),
        scratch_shapes=_attn_scratch(m, tk),
        compiler_params=_params(2),
        name="gqa_attn",
    )(q, q, k, v)


def _bias_span(tq, tk):
    u_lo = -((MAX_DISTANCE + tk - 1 + tq - 1) // tq)
    u_hi = (MAX_DISTANCE + tq - 1 + tq - 1) // tq
    return u_lo, u_hi


def _t5_bucket(rel):
    nb = N_BUCKETS // 2
    max_exact = nb // 2
    ret = jnp.where(rel > 0, nb, 0)
    n = jnp.abs(rel)
    nf = jnp.maximum(n, 1).astype(F32)
    large = max_exact + (jnp.log(nf / max_exact) / math.log(MAX_DISTANCE / max_exact)
                         * (nb - max_exact)).astype(jnp.int32)
    large = jnp.minimum(large, nb - 1)
    return ret + jnp.where(n < max_exact, n, large)


def _bias_body(tab_ref, o_ref, *, tq, tk, u_lo, width):
    h = pl.program_id(0)
    u = pl.program_id(1) + u_lo
    rel = u * tq - (tq - 1) + lax.broadcasted_iota(jnp.int32, (SUBLANES, width), 1)
    bucket = _t5_bucket(rel)
    profile = jnp.zeros((SUBLANES, width), F32)
    for b in range(N_BUCKETS):
        profile = jnp.where(bucket == b, tab_ref[b, h], profile)
    rows = jnp.broadcast_to(profile[0:1, :], (tq, width))
    tile = pltpu.roll(rows, width - (tq - 1), 1, stride=1, stride_axis=0)
    o_ref[0, 0] = tile[:, :tk] * LOG2E


def _bias_tiles(rel_bias, tq, tk):
    u_lo, u_hi = _bias_span(tq, tk)
    n_off = u_hi - u_lo + 1
    n_heads = rel_bias.shape[1]
    width = pl.next_power_of_2(tq + tk - 1)
    return pl.pallas_call(
        functools.partial(_bias_body, tq=tq, tk=tk, u_lo=u_lo, width=width),
        grid=(n_heads, n_off),
        in_specs=[pl.BlockSpec(memory_space=pltpu.SMEM)],
        out_specs=pl.BlockSpec((1, 1, tq, tk), lambda h, u: (h, u, 0, 0)),
        out_shape=jax.ShapeDtypeStruct((n_heads, n_off, tq, tk), F32),
        compiler_params=_params(2),
        name="t5_bias",
    )(rel_bias.astype(F32))


def _near_count(tq, tk, u_lo, u_hi):
    r = tk // tq
    return max((qi + u_hi - 1) // r - (qi + u_lo) // r for qi in range(r))


def _diff_body(lam_ref, far_ref, q_ref, qn_ref, k_ref, v_ref, bias_ref, g_ref, o_ref,
               lhs_scr, m_scr, acc_scr, s0_scr, s1_scr, *, tq, tk, n_kv, n_near, u_lo, u_hi, out_scale):
    h = pl.program_id(0)
    qi = pl.program_id(2)
    cur = qi % 2
    r = tk // tq

    def stack(nxt):
        _stack_halves([(qn_ref if nxt else q_ref)[...]], lhs_scr.at[1 - cur if nxt else cur], tq)

    def first_near(q_idx):
        return jnp.clip((q_idx + u_lo) // r + 1, 0, n_kv - n_near)

    def tile_of(p, near, q_idx):
        t_a = first_near(q_idx)
        if near:
            return t_a + (p - (n_kv - n_near))
        return jnp.where(p < t_a, p, p + n_near)

    def score(p, near, nxt):
        q_idx = qi + 1 if nxt else qi
        t = tile_of(p, near, q_idx)
        s = _qk(lhs_scr[1 - cur if nxt else cur], k_ref[_key_rows(t, tk), :])
        if not near:
            return s
        u = jnp.clip(t * r - q_idx, u_lo, u_hi) - u_lo
        return (s.reshape(2, tq, tk) + bias_ref[0, u][None]).reshape(2 * tq, tk)

    def consume(s, p, near):
        t = tile_of(p, near, qi)
        v = v_ref[_key_rows(t, tk), :]
        if near:
            _consume(s, v, m_scr, acc_scr)
        else:
            _consume(s, v, m_scr, acc_scr, const=jnp.where(t < first_near(qi), far_ref[0, h], far_ref[1, h]))

    _pipelined_attention(qi, n_kv, n_near, stack, score, consume, m_scr, acc_scr, (s0_scr, s1_scr))
    o = acc_scr[:, :LANES] / acc_scr[:, LANES:]
    o = o[:tq] - lam_ref[0] * o[tq:]
    o_ref[...] = (_rms(o, g_ref[...], DIFF_SUBLN_EPS) * out_scale).astype(o_ref.dtype)


def _diff(lam, far, q, k, v, bias, subln_g, nseq, seq_len, lam_init, tq, tk):
    n_heads = q.shape[1] // LANES
    nq = seq_len // tq
    assert nq == 1 or nq % 2 == 0
    u_lo, u_hi = _bias_span(tq, tk)
    n_off = u_hi - u_lo + 1
    kv_spec = pl.BlockSpec((seq_len, LANES), lambda h, b, i: (b, h))
    n_kv = seq_len // tk
    n_near = min(_near_count(tq, tk, u_lo, u_hi), n_kv)
    return pl.pallas_call(
        functools.partial(_diff_body, tq=tq, tk=tk, n_kv=n_kv, n_near=n_near, u_lo=u_lo, u_hi=u_hi,
                          out_scale=1.0 - lam_init),
        grid=(n_heads, nseq, nq),
        in_specs=[pl.BlockSpec(memory_space=pltpu.SMEM),
                  pl.BlockSpec(memory_space=pltpu.SMEM),
                  pl.BlockSpec((tq, LANES), lambda h, b, i: (b * nq + i, h)),
                  pl.BlockSpec((tq, LANES), lambda h, b, i: (b * nq + jnp.minimum(i + 1, nq - 1), h)),
                  kv_spec, kv_spec,
                  pl.BlockSpec((1, n_off, tq, tk), lambda h, b, i: (h, 0, 0, 0),
                               pipeline_mode=pl.Buffered(1)),
                  pl.BlockSpec((1, LANES), lambda h, b, i: (0, 0))],
        out_specs=pl.BlockSpec((tq, LANES), lambda h, b, i: (b * nq + i, h)),
        out_shape=jax.ShapeDtypeStruct((nseq * seq_len, q.shape[1]), BF16),
        scratch_shapes=_attn_scratch(2 * tq, tk),
        compiler_params=_params(3),
        name="diff_attn",
    )(lam, far, q, q, k, v, bias, subln_g.reshape(1, LANES))


def _rope_tables(max_len):
    pos = np.arange(max_len)
    half = HEAD_DIM // 2
    inv = ROPE_THETA ** (-jnp.arange(0, half, 2, dtype=F32) / half)
    row = jnp.asarray(pos // GRID_W, F32)
    col = jnp.asarray(pos % GRID_W, F32)
    ang = jnp.concatenate([row[:, None] * inv, col[:, None] * inv], axis=-1)
    cos, sin = jnp.cos(ang), jnp.sin(ang)
    cos_h = jnp.concatenate([cos, cos], axis=1)
    sin_h = jnp.concatenate([-sin, sin], axis=1)
    return jnp.tile(cos_h, (1, LANES // HEAD_DIM)), jnp.tile(sin_h, (1, LANES // HEAD_DIM))


def _column_layout(d_model, d_hy):
    d_gqa = N_Q_HEADS * HEAD_DIM
    d_kv = N_KV_HEADS * HEAD_DIM
    d_diff = N_DIFF_HEADS * 2 * HEAD_DIM
    widths = [3 * d_hy, d_gqa, d_kv, d_kv, d_diff, d_diff, d_diff, 3 * d_model]
    offs = [0]
    for w in widths:
        offs.append(offs[-1] + w)
    deint = np.concatenate([np.arange(0, HEAD_DIM, 2), np.arange(1, HEAD_DIM, 2)])
    q_heads = [kv * GQA_GROUP + g for g in range(GQA_GROUP) for kv in range(N_KV_HEADS)]
    q_cols = np.concatenate([offs[1] + h * HEAD_DIM + deint for h in q_heads])
    k_cols = np.concatenate([offs[2] + h * HEAD_DIM + deint for h in range(N_KV_HEADS)])
    qkv_cols = np.concatenate([q_cols, k_cols, np.arange(offs[3], offs[4])])
    out_rows = np.concatenate([h * HEAD_DIM + np.arange(HEAD_DIM) for h in q_heads])
    return offs, qkv_cols, deint, out_rows


def kernel(x_prompt, x_sample, ffn1_norm, ffn1_w_in, ffn1_w_out, mix_norm, w_in, hy_conv_w, hy_conv_b, hy_filt_w1, hy_filt_b1, hy_filt_w2, hy_filt_b2, hy_filt_w3, hy_filt_freq, hy_skip, gqa_q_norm, gqa_k_norm, diff_lambda, diff_subln, rel_bias, w_branch, w_out, ffn2_norm, ffn2_w_in, ffn2_w_out, final_norm):
    d = x_prompt.shape[-1]
    depth = w_in.shape[0]
    d_hy = hy_skip.shape[1]
    groups = [(x.shape[0], x.shape[1]) for x in (x_prompt, x_sample)]
    xs = [x_prompt.reshape(-1, d), x_sample.reshape(-1, d)]
    lens = sorted({sl for _, sl in groups})
    assert all(sl % GRID_W == 0 for sl in lens)

    offs, qkv_cols, deint, out_rows = _column_layout(d, d_hy)
    cos_t, sin_t = _rope_tables(max(lens))
    bd = jnp.asarray(np.kron(np.eye(LANES // HEAD_DIM), np.full((HEAD_DIM, HEAD_DIM), 1.0 / HEAD_DIM)), BF16)
    tabs = {sl: _dft_tables(sl) for sl in lens}
    diff_tiles = {sl: (min(DIFF_TQ, sl), min(DIFF_TK, sl)) for sl in lens}
    bias = {tt: _bias_tiles(rel_bias, *tt) for tt in sorted(set(diff_tiles.values()))}
    far_bias = rel_bias[jnp.array([N_BUCKETS // 2 - 1, N_BUCKETS - 1])].astype(F32) * LOG2E

    for l in range(depth):
        w1_in, w1_out = ffn1_w_in[l].astype(BF16), ffn1_w_out[l].astype(BF16)
        w2_in, w2_out = ffn2_w_in[l].astype(BF16), ffn2_w_out[l].astype(BF16)
        w_l = w_in[l].astype(BF16)
        w_qkv = w_in[l][:, offs[1]:offs[4]][:, qkv_cols - offs[1]].astype(BF16)
        qn = jnp.tile(gqa_q_norm[l][deint], N_Q_HEADS).reshape(1, -1)
        kn = jnp.tile(gqa_k_norm[l][deint], N_KV_HEADS).reshape(1, -1)
        wb = jnp.stack([w_branch[l][0], w_branch[l][1][out_rows], w_branch[l][2]]).astype(BF16)
        wo = w_out[l].astype(BF16)
        lp = diff_lambda[l].astype(F32)
        lam_init = 0.8 - 0.6 * math.exp(-0.3 * l)
        lam = (jnp.exp(jnp.sum(lp[0] * lp[1])) - jnp.exp(jnp.sum(lp[2] * lp[3])) + lam_init).reshape(1)
        kfs = {sl: _hyena_kf(sl, tabs[sl], hy_filt_w1[l], hy_filt_b1[l], hy_filt_w2[l], hy_filt_b2[l],
                             hy_filt_w3[l], hy_filt_freq[l]) for sl in lens}
        final_g = final_norm if l == depth - 1 else None

        for g, (nseq, sl) in enumerate(groups):
            x = _ffn(xs[g], ffn1_norm[l], w1_in, w1_out)
            z, x0, gq, gk, gv, dq, dk, dv, gates = _inproj(x, mix_norm[l], w_l, w_qkv, hy_conv_w[l],
                                                            hy_conv_b[l], cos_t, sin_t, bd, qn, kn, offs, sl)
            y_hy = _hyena_longconv(z, x0, kfs[sl], hy_skip[l], nseq, sl, tabs[sl])
            y_gqa = _gqa(gq, gk, gv, nseq, sl)
            tq, tk = diff_tiles[sl]
            y_diff = _diff(lam, far_bias, dq, dk, dv, bias[(tq, tk)], diff_subln[l], nseq, sl, lam_init, tq, tk)
            xs[g] = _ffn(x, ffn2_norm[l], w2_in, w2_out, final_g=final_g,
                         merge=(y_hy, y_gqa, y_diff, gates, wb, wo))

    return xs[0].reshape(x_prompt.shape), xs[1].reshape(x_sample.shape)
```

```python
import functools
import math

import numpy as np
import jax
import jax.numpy as jnp
from jax import lax
from jax.experimental import pallas as pl
from jax.experimental.pallas import tpu as pltpu

F32 = jnp.float32
BF16 = jnp.bfloat16

NORM_EPS = 1e-6
GRID_W = 64
HEAD_DIM = 64
N_Q_HEADS = 8
N_KV_HEADS = 2
GQA_GROUP = N_Q_HEADS // N_KV_HEADS
ROPE_THETA = 10000.0
N_DIFF_HEADS = 4
DIFF_SUBLN_EPS = 1e-5
N_BUCKETS = 32
MAX_DISTANCE = 128
POS_BANDS = 16
DECAY_TARGET = 1e-2
FAST_DECAY_PCT = 0.3
SLOW_DECAY_PCT = 1.5

LANES = 128
SUBLANES = 8
MXU_WIDTH = 256
VMEM_LIMIT_BYTES = 56 * 1024 * 1024

FFN_TM = 512
PROJ_TM = 512
GQA_TQ = 128
GQA_TK = 1024
DIFF_TQ = 512
DIFF_TK = 1024
FFT_CC = 8
CONV_CHAINS = 4
KF_CHAINS = 2
FILT_TM = 512

NEG_BIG = -1e30
LOG2E = math.log2(math.e)
KV_TILES_PER_ITER = 8


def _params(n_axes):
    return pltpu.CompilerParams(dimension_semantics=("arbitrary",) * n_axes,
                                vmem_limit_bytes=VMEM_LIMIT_BYTES)


def _dot(a, b):
    return jnp.dot(a, b, preferred_element_type=F32)


def _split_bf16(x):
    hi = x.astype(BF16)
    return hi, (x - hi.astype(F32)).astype(BF16)


def _dot3(a, b):
    ah, al = _split_bf16(a)
    bh, bl = _split_bf16(b)
    return _dot(ah, bh) + (_dot(ah, bl) + _dot(al, bh))


def _rms(x, g, eps):
    ms = jnp.mean(x * x, axis=-1, keepdims=True)
    return x * lax.rsqrt(ms + eps) * g


def _merged_residual(x_ref, yh_ref, yg_ref, yd_ref, gate_ref, wb_ref, wo_ref):
    d = x_ref.shape[1]
    merged = (gate_ref[:, 0:d].astype(F32) * _dot(yh_ref[...], wb_ref[0])
              + gate_ref[:, d:2 * d].astype(F32) * _dot(yg_ref[...], wb_ref[1])
              + gate_ref[:, 2 * d:3 * d].astype(F32) * _dot(yd_ref[...], wb_ref[2]))
    return x_ref[...] + _dot(merged.astype(BF16), wo_ref[...])


def _ffn_body(*refs, chunks, final, merge):
    refs = list(refs)
    o_ref = refs.pop()
    if merge:
        x = _merged_residual(*refs[:7])
        refs = refs[7:]
    else:
        x = refs.pop(0)[...]
    g_ref, w_in_ref, w_out_ref = refs[:3]
    d_ff = w_out_ref.shape[0]
    h = _rms(x, g_ref[...], NORM_EPS).astype(BF16)
    acc = None
    for a, b in chunks:
        gate = _dot(h, w_in_ref[:, a:b])
        up = _dot(h, w_in_ref[:, d_ff + a:d_ff + b])
        part = _dot((gate * jax.nn.sigmoid(gate) * up).astype(BF16), w_out_ref[a:b, :])
        acc = part if acc is None else acc + part
    y = x + 0.5 * acc
    if final:
        y = _rms(y, refs[3][...], NORM_EPS)
    o_ref[...] = y


def _ffn_chunks(d_ff):
    n_blk = d_ff // MXU_WIDTH
    if d_ff % MXU_WIDTH or n_blk < 2:
        return ((0, d_ff),)
    cut = (n_blk + 1) // 2 * MXU_WIDTH
    return ((0, cut), (cut, d_ff))


def _ffn(x, norm_g, w_in, w_out, final_g=None, merge=None):
    t, d = x.shape
    d_ff = w_out.shape[0]
    tm = min(FFN_TM, t)
    final = final_g is not None
    const = lambda i: (0, 0)
    row = lambda i: (i, 0)
    resident = lambda a: pl.BlockSpec(a.shape, lambda i: (0,) * a.ndim, pipeline_mode=pl.Buffered(1))
    in_specs = [pl.BlockSpec((tm, d), row)]
    args = [x]
    if merge is not None:
        y_hy, y_gqa, y_diff, gates, wb, wo = merge
        in_specs += [pl.BlockSpec((tm, y.shape[1]), row) for y in (y_hy, y_gqa, y_diff, gates)]
        in_specs += [resident(wb), resident(wo)]
        args += [y_hy, y_gqa, y_diff, gates, wb, wo]
    in_specs += [pl.BlockSpec((1, d), const), resident(w_in), resident(w_out)]
    args += [norm_g.reshape(1, d), w_in, w_out]
    if final:
        in_specs.append(pl.BlockSpec((1, d), const))
        args.append(final_g.reshape(1, d))
    return pl.pallas_call(
        functools.partial(_ffn_body, chunks=_ffn_chunks(d_ff), final=final, merge=merge is not None),
        grid=(t // tm,),
        in_specs=in_specs,
        out_specs=pl.BlockSpec((tm, d), row),
        out_shape=jax.ShapeDtypeStruct((t, d), F32),
        compiler_params=_params(1),
        name="merge_ffn" if merge is not None else "ffn",
    )(*args)


def _head_rms(x, bd):
    x2 = x * x
    hi = x2.astype(BF16)
    lo = (x2 - hi.astype(F32)).astype(BF16)
    ms = _dot(hi, bd) + _dot(lo, bd)
    return x * lax.rsqrt(ms + NORM_EPS)


def _rope(x, cos, sin_signed):
    lane = lax.broadcasted_iota(jnp.int32, x.shape, 1)
    partner = jnp.where((lane % HEAD_DIM) < HEAD_DIM // 2,
                        pltpu.roll(x, LANES - HEAD_DIM // 2, 1),
                        pltpu.roll(x, HEAD_DIM // 2, 1))
    return x * cos + partner * sin_signed


def _short_conv_gate(u, before, after, w_ref, b_ref, z_ref, x0_ref):
    tm, d_hy = u.shape[0], u.shape[1] // 3
    rows = lax.broadcasted_iota(jnp.int32, u.shape, 0)
    up = jnp.where(rows == 0, before, pltpu.roll(u, 1, 0))
    dn = jnp.where(rows == tm - 1, after, pltpu.roll(u, tm - 1, 0))
    y = up * w_ref[0:1, :] + u * w_ref[1:2, :] + dn * w_ref[2:3, :] + b_ref[...]
    x0_ref[...] = y[:, :d_hy]
    z_ref[...] = y[:, 2 * d_hy:] * y[:, d_hy:2 * d_hy]


def _inproj_body(x_ref, xp_ref, xn_ref, g_ref, w_ref, wqkv_ref, cw_ref, cb_ref, cos_ref, sin_ref, bd_ref,
                 qn_ref, kn_ref, z_ref, x0_ref, gq_ref, gk_ref, gv_ref, dq_ref, dk_ref, dv_ref, gate_ref,
                 *, offs, seq_len):
    o_hy, o_gq, o_gk, o_gv, o_dq, o_dk, o_dv, o_gate, o_end = offs
    g = g_ref[...]
    h = _rms(x_ref[...], g, NORM_EPS).astype(BF16)

    def seg(a, b):
        return _dot(h, w_ref[:, a:b])

    tm, halo = x_ref.shape[0], xp_ref.shape[0]
    pos = (pl.program_id(0) * tm) % seq_len
    h_halo = [_rms(r[...], g, NORM_EPS).astype(BF16) for r in (xp_ref, xn_ref)]
    u = _dot(jnp.concatenate([h] + h_halo, axis=0), w_ref[:, o_hy:o_gq])
    before = jnp.where(pos == 0, 0.0, u[tm + halo - 1:tm + halo, :])
    after = jnp.where(pos + tm == seq_len, 0.0, u[tm + halo:tm + halo + 1, :])
    _short_conv_gate(u[:tm], before, after, cw_ref, cb_ref, z_ref, x0_ref)

    cos = cos_ref[...]
    sin = sin_ref[...]
    bd = bd_ref[...]
    scale = HEAD_DIM ** -0.5 * LOG2E
    qkv = _dot(h, wqkv_ref[...])
    wq, wk = o_gk - o_gq, o_gv - o_gk
    for c in range(wq // LANES):
        y = _head_rms(qkv[:, c * LANES:(c + 1) * LANES], bd) * qn_ref[:, c * LANES:(c + 1) * LANES]
        gq_ref[:, c * LANES:(c + 1) * LANES] = (_rope(y, cos, sin) * scale).astype(BF16)
    for c in range(wk // LANES):
        y = _head_rms(qkv[:, wq + c * LANES:wq + (c + 1) * LANES], bd) * kn_ref[:, c * LANES:(c + 1) * LANES]
        gk_ref[:, c * LANES:(c + 1) * LANES] = _rope(y, cos, sin).astype(BF16)
    gv_ref[...] = qkv[:, wq + wk:].astype(BF16)
    dq_ref[...] = (seg(o_dq, o_dk) * scale).astype(BF16)
    dk_ref[...] = seg(o_dk, o_dv).astype(BF16)
    dv_ref[...] = seg(o_dv, o_gate).astype(BF16)
    d = x_ref.shape[1]
    for c in range((o_end - o_gate) // d):
        gate_ref[:, c * d:(c + 1) * d] = jax.nn.sigmoid(seg(o_gate + c * d, o_gate + (c + 1) * d)).astype(BF16)


def _inproj(x, norm_g, w, wqkv, conv_w, conv_b, cos_t, sin_t, bd, qn, kn, offs, seq_len):
    t, d = x.shape
    tm = min(PROJ_TM, seq_len)
    d_hy = (offs[1] - offs[0]) // 3
    widths = [d_hy, d_hy] + [offs[i + 1] - offs[i] for i in range(1, 8)]
    halo = 2 * SUBLANES
    n_halo = t // halo

    def pos_map(i):
        return (i % (seq_len // tm), 0)

    full = lambda i: (0, 0)
    row = lambda i: (i, 0)
    out_dtypes = [F32, F32] + [BF16] * 7
    return pl.pallas_call(
        functools.partial(_inproj_body, offs=tuple(offs), seq_len=seq_len),
        grid=(t // tm,),
        in_specs=[
            pl.BlockSpec((tm, d), row),
            pl.BlockSpec((halo, d), lambda i: (jnp.maximum(i * (tm // halo) - 1, 0), 0)),
            pl.BlockSpec((halo, d), lambda i: (jnp.minimum((i + 1) * (tm // halo), n_halo - 1), 0)),
            pl.BlockSpec((1, d), full),
            pl.BlockSpec(w.shape, full, pipeline_mode=pl.Buffered(1)),
            pl.BlockSpec(wqkv.shape, full, pipeline_mode=pl.Buffered(1)),
            pl.BlockSpec(conv_w.shape, full),
            pl.BlockSpec((1, conv_b.shape[0]), full),
            pl.BlockSpec((tm, LANES), pos_map),
            pl.BlockSpec((tm, LANES), pos_map),
            pl.BlockSpec(bd.shape, full),
            pl.BlockSpec(qn.shape, full),
            pl.BlockSpec(kn.shape, full),
        ],
        out_specs=[pl.BlockSpec((tm, wd), row) for wd in widths],
        out_shape=[jax.ShapeDtypeStruct((t, wd), dt) for wd, dt in zip(widths, out_dtypes)],
        compiler_params=_params(1),
        name="inproj",
    )(x, x, x, norm_g.reshape(1, d), w, wqkv, conv_w, conv_b.reshape(1, -1), cos_t, sin_t, bd, qn, kn)


def _filter_body(f_ref, w1_ref, b1_ref, w2_ref, b2_ref, w3_ref, fr_ref, dl_ref, o_ref, *, n_feat):
    half = f_ref.shape[0] // 2
    d_hy = o_ref.shape[1]
    halves = (f_ref[:half, :], f_ref[half:, :])
    fr = fr_ref[...]
    h = jnp.sin(fr * (_dot3(jnp.concatenate(halves, axis=1), w1_ref[...]) + b1_ref[...]))
    h = jnp.sin(fr * (_dot3(h, w2_ref[...]) + b2_ref[...]))
    h = _dot3(h, w3_ref[0])
    for k, f in enumerate(halves):
        decay = jnp.exp(-f[:, 0:1] * dl_ref[...])
        o_ref[k * half:(k + 1) * half, :] = h[:, k * d_hy:(k + 1) * d_hy] * decay * f[:, n_feat:n_feat + 1]


def _hyena_filter(feats, w1p, b1, w2, b2, w3, freq, deltas_abs, seq_len, n_feat):
    n2l, fw = feats.shape
    d_hy = w3.shape[1] // 2
    hid = w2.shape[0]
    tm = min(FILT_TM, seq_len)
    nbl = seq_len // tm
    full = lambda i: (0, 0)
    twice = lambda m: jnp.kron(jnp.eye(2, dtype=m.dtype), m)
    pair = lambda v: jnp.tile(v, 2).reshape(1, -1)
    return pl.pallas_call(
        functools.partial(_filter_body, n_feat=n_feat),
        grid=(n2l // tm,),
        in_specs=[
            pl.BlockSpec((tm, fw), lambda i: (i, 0)),
            pl.BlockSpec((2 * fw, 2 * hid), full),
            pl.BlockSpec((1, 2 * hid), full),
            pl.BlockSpec((2 * hid, 2 * hid), full),
            pl.BlockSpec((1, 2 * hid), full),
            pl.BlockSpec((1, 2 * hid, 2 * d_hy), lambda i: (i // nbl, 0, 0)),
            pl.BlockSpec((1, 2 * hid), full),
            pl.BlockSpec((1, d_hy), full),
        ],
        out_specs=pl.BlockSpec((tm, d_hy), lambda i: (i, 0)),
        out_shape=jax.ShapeDtypeStruct((n2l, d_hy), F32),
        compiler_params=_params(1),
        name="hyena_filter",
    )(feats, twice(w1p), pair(b1), twice(w2), pair(b2),
      jnp.stack([twice(w3[:, :d_hy]), twice(w3[:, d_hy:])]), pair(freq), deltas_abs.reshape(1, d_hy))


def _fwd_spectra(f1, chunks, twr, twi, w2f, cc):
    n1 = twr.shape[0]
    a = [_dot3(f1, x) for x in chunks]
    b = []
    for aj in a:
        rows = []
        for c in range(cc):
            ar = aj[:n1, c * LANES:(c + 1) * LANES]
            ai = aj[n1:, c * LANES:(c + 1) * LANES]
            rows.append(jnp.concatenate([ar * twr - ai * twi, ar * twi + ai * twr], axis=1))
        b.append(jnp.concatenate(rows, axis=0))
    return [_dot3(bj, w2f) for bj in b]


def _kf_body(f1_ref, x_ref, twr_ref, twi_ref, w2f_ref, o_ref, *, cc, n_sub, scale):
    n1 = twr_ref.shape[0]
    w = cc * LANES
    xs = _fwd_spectra(f1_ref[...], [x_ref[:, j * w:(j + 1) * w] for j in range(n_sub)],
                      twr_ref[...], twi_ref[...], w2f_ref[...], cc)
    for j, xj in enumerate(xs):
        for c in range(cc):
            lanes = slice(j * w + c * LANES, j * w + (c + 1) * LANES)
            o_ref[0, :, lanes] = xj[c * n1:(c + 1) * n1, :LANES] * scale
            o_ref[1, :, lanes] = xj[c * n1:(c + 1) * n1, LANES:] * scale


def _conv_body(f1_ref, finv_ref, z_ref, x0_ref, kf_ref, skip_ref, twr_ref, twi_ref, w2f_ref, w2i_ref,
               o_ref, *, cc, n_sub):
    n1 = twr_ref.shape[0]
    w = cc * LANES
    twr, twi = twr_ref[...], twi_ref[...]
    cols = [slice(j * w, (j + 1) * w) for j in range(n_sub)]
    zs = [z_ref[0, :, c] for c in cols]
    xs = _fwd_spectra(f1_ref[...], zs, twr, twi, w2f_ref[...], cc)
    ys = []
    for xj, cj in zip(xs, cols):
        xr, xi = xj[:, :LANES], xj[:, LANES:]
        kr = jnp.concatenate([kf_ref[0, :, cj][:, c * LANES:(c + 1) * LANES] for c in range(cc)], axis=0)
        ki = jnp.concatenate([kf_ref[1, :, cj][:, c * LANES:(c + 1) * LANES] for c in range(cc)], axis=0)
        ys.append(jnp.concatenate([xr * kr - xi * ki, xr * ki + xi * kr], axis=1))
    w2i = w2i_ref[...]
    cs = [_dot3(yj, w2i) for yj in ys]
    ds = []
    for cj in cs:
        d_re, d_im = [], []
        for c in range(cc):
            cr = cj[c * n1:(c + 1) * n1, :LANES]
            ci = cj[c * n1:(c + 1) * n1, LANES:]
            d_re.append(cr * twr + ci * twi)
            d_im.append(ci * twr - cr * twi)
        ds.append(jnp.concatenate([jnp.concatenate(d_re, axis=1), jnp.concatenate(d_im, axis=1)], axis=0))
    finv = finv_ref[...]
    convs = [_dot3(finv, dj) for dj in ds]
    for conv, z, cj in zip(convs, zs, cols):
        o_ref[0, :, cj] = (x0_ref[0, :, cj] * (conv + z * skip_ref[:, cj])).astype(o_ref.dtype)


def _dft_tables(seq_len):
    n = 2 * seq_len
    n1 = n // LANES
    n1h = n1 // 2
    n1r = n1h + SUBLANES
    k1 = np.arange(n1r)[:, None]
    kept = (k1 <= n1h).astype(np.float64)
    ang1 = 2.0 * np.pi * k1 * np.arange(n1)[None, :] / n1
    f1 = np.concatenate([kept * np.cos(ang1), -kept * np.sin(ang1)], axis=0)
    weight = kept * np.where((k1 == 0) | (k1 == n1h), 1.0, 2.0)
    finv = np.concatenate([(weight * np.cos(ang1[:, :n1h])).T,
                           (-weight * np.sin(ang1[:, :n1h])).T], axis=1)
    ang2 = 2.0 * np.pi * np.arange(LANES)[:, None] * np.arange(LANES)[None, :] / LANES
    f2r, f2i = np.cos(ang2), -np.sin(ang2)
    w2f = np.block([[f2r, f2i], [-f2i, f2r]])
    w2i = np.block([[f2r, -f2i], [f2i, f2r]])
    angt = 2.0 * np.pi * k1 * np.arange(LANES)[None, :] / n
    f = lambda a: jnp.asarray(a, F32)
    return dict(n=n, n1=n1, n1h=n1h, n1r=n1r, f1_full=f(f1), f1_half=f(f1[:, :n1h]), finv=f(finv),
                w2f=f(w2f), w2i=f(w2i), twr=f(np.cos(angt)), twi=f(-np.sin(angt)))


def _to_blocked(x, nseq):
    t, c = x.shape
    r = t // (nseq * LANES)
    return x.reshape(nseq, r, LANES, c).transpose(0, 1, 3, 2).reshape(nseq, r, c * LANES)


def _from_blocked(y, c):
    nseq, r, _ = y.shape
    return y.reshape(nseq, r, c, LANES).transpose(0, 1, 3, 2).reshape(nseq * r * LANES, c)


def _filter_features(seq_len, fw):
    t = jnp.linspace(0.0, 1.0, seq_len, dtype=F32)[:, None]
    band = jnp.linspace(1e-4, POS_BANDS - 1, POS_BANDS, dtype=F32)
    ang = (2.0 * math.pi / seq_len) * jnp.arange(seq_len, dtype=F32)[:, None] * band[None, :]
    feats = jnp.concatenate([t, jnp.cos(ang), -jnp.sin(ang)], axis=-1)
    n_feat = feats.shape[1]
    rows = jnp.concatenate([feats, feats[-1:], feats[:0:-1]], axis=0)
    valid = jnp.asarray((np.arange(2 * seq_len) != seq_len).astype(np.float32))[:, None]
    rows = jnp.concatenate([rows, valid], axis=1)
    return jnp.pad(rows, ((0, 0), (0, fw - n_feat - 1))), n_feat


def _hyena_kf(seq_len, tabs, w1, b1, w2, b2, w3, freq):
    d_hy = w3.shape[1] // 2
    feats, n_feat = _filter_features(seq_len, LANES)
    w1p = jnp.pad(w1, ((0, LANES - w1.shape[0]), (0, 0)))
    max_decay = math.log(DECAY_TARGET) / FAST_DECAY_PCT
    min_decay = math.log(DECAY_TARGET) / SLOW_DECAY_PCT
    deltas = jnp.abs(jnp.linspace(min_decay, max_decay, d_hy, dtype=F32))
    kern = _hyena_filter(feats, w1p, b1, w2, b2, w3, freq, deltas, seq_len, n_feat)
    n, n1, n1r = tabs["n"], tabs["n1"], tabs["n1r"]
    cc = FFT_CC
    n_sub = KF_CHAINS
    w = n_sub * cc * LANES
    full = lambda j: (0, 0)
    return pl.pallas_call(
        functools.partial(_kf_body, cc=cc, n_sub=n_sub, scale=1.0 / n),
        grid=(d_hy // (n_sub * cc),),
        in_specs=[pl.BlockSpec((2 * n1r, n1), full),
                  pl.BlockSpec((n1, w), lambda j: (0, j)),
                  pl.BlockSpec((n1r, LANES), full), pl.BlockSpec((n1r, LANES), full),
                  pl.BlockSpec((2 * LANES, 2 * LANES), full)],
        out_specs=pl.BlockSpec((2, n1r, w), lambda j: (0, 0, j)),
        out_shape=jax.ShapeDtypeStruct((2, n1r, d_hy * LANES), F32),
        compiler_params=_params(1),
        name="hyena_kf",
    )(tabs["f1_full"], _to_blocked(kern, 1)[0], tabs["twr"], tabs["twi"], tabs["w2f"])


def _hyena_longconv(z, x0, kf, skip, nseq, seq_len, tabs):
    c = z.shape[1]
    n1h, n1r = tabs["n1h"], tabs["n1r"]
    cc = FFT_CC
    n_sub = CONV_CHAINS
    w = n_sub * cc * LANES
    full = lambda b, j: (0, 0)
    rowblk = pl.BlockSpec((1, n1h, w), lambda b, j: (b, 0, j))
    y = pl.pallas_call(
        functools.partial(_conv_body, cc=cc, n_sub=n_sub),
        grid=(nseq, c // (n_sub * cc)),
        in_specs=[pl.BlockSpec((2 * n1r, n1h), full),
                  pl.BlockSpec((n1h, 2 * n1r), full),
                  rowblk, rowblk,
                  pl.BlockSpec((2, n1r, w), lambda b, j: (0, 0, j)),
                  pl.BlockSpec((1, w), lambda b, j: (0, j)),
                  pl.BlockSpec((n1r, LANES), full), pl.BlockSpec((n1r, LANES), full),
                  pl.BlockSpec((2 * LANES, 2 * LANES), full), pl.BlockSpec((2 * LANES, 2 * LANES), full)],
        out_specs=rowblk,
        out_shape=jax.ShapeDtypeStruct((nseq, n1h, c * LANES), BF16),
        compiler_params=_params(2),
        name="hyena_conv",
    )(tabs["f1_half"], tabs["finv"], _to_blocked(z, nseq), _to_blocked(x0, nseq), kf,
      jnp.repeat(skip.astype(F32), LANES).reshape(1, c * LANES),
      tabs["twr"], tabs["twi"], tabs["w2f"], tabs["w2i"])
    return _from_blocked(y, c)


def _stack_halves(q_cols, lhs_ref, tq):
    lane = lax.broadcasted_iota(jnp.int32, (tq, LANES), 1)
    low = lane < HEAD_DIM
    zero = jnp.zeros((tq, LANES), lhs_ref.dtype)
    for c, q in enumerate(q_cols):
        lhs_ref[(2 * c) * tq:(2 * c + 1) * tq, :] = jnp.where(low, q, zero)
        lhs_ref[(2 * c + 1) * tq:(2 * c + 2) * tq, :] = jnp.where(low, zero, q)


def _consume(s, v, m_scr, acc_scr, const=None):
    tk = s.shape[1]
    m_prev = m_scr[...]
    row_max = jnp.max(s, axis=1, keepdims=True)
    if const is not None:
        row_max = row_max + const
    m_new = jnp.maximum(m_prev, row_max)
    shift = m_new if const is None else m_new - const
    p = jnp.exp2(s - jnp.concatenate([shift] * (tk // LANES), axis=1)).astype(BF16)
    alpha = jnp.exp2(m_prev - m_new)
    v_ext = jnp.concatenate([v, jnp.ones_like(v)], axis=1)
    acc_scr[...] = jnp.concatenate([alpha, alpha], axis=1) * acc_scr[...] + _dot(p, v_ext)
    m_scr[...] = m_new


def _qk(lhs, k):
    return lax.dot_general(lhs, k, (((1,), (1,)), ((), ())), preferred_element_type=F32)


def _key_rows(t, tk):
    return pl.ds(t * tk if isinstance(t, int) else pl.multiple_of(t * tk, tk), tk)


def _pipelined_attention(qi, n_kv, n_special, stack_fn, score_fn, consume_fn, m_scr, acc_scr, s_scr):
    @pl.when(qi == 0)
    def _():
        stack_fn(False)
        s_scr[0][...] = score_fn(0, n_special == n_kv, False)

    m_scr[...] = jnp.full_like(m_scr, NEG_BIG)
    acc_scr[...] = jnp.zeros_like(acc_scr)
    unroll = min(KV_TILES_PER_ITER, n_kv)
    assert n_kv % unroll == 0 and n_special <= unroll

    def run(p0, last):
        for j in range(unroll):
            special = last and j >= unroll - n_special
            if last and j == unroll - 1:
                stack_fn(True)
                s_scr[(j + 1) % 2][...] = score_fn(0, n_special == n_kv, True)
            else:
                s_scr[(j + 1) % 2][...] = score_fn(p0 + j + 1, last and j + 1 >= unroll - n_special, False)
            consume_fn(s_scr[j % 2][...], p0 + j, special)

    def body(i, carry):
        run(i * unroll, False)
        return carry

    lax.fori_loop(0, n_kv // unroll - 1, body, 0)
    run(n_kv - unroll, True)
    if unroll % 2 == 1:
        s_scr[0][...] = s_scr[1][...]


def _attn_scratch(m, tk):
    return [pltpu.VMEM((2, m, LANES), BF16), pltpu.VMEM((m, LANES), F32), pltpu.VMEM((m, 2 * LANES), F32),
            pltpu.VMEM((m, tk), F32), pltpu.VMEM((m, tk), F32)]


def _gqa_body(q_ref, qn_ref, k_ref, v_ref, o_ref, lhs_scr, m_scr, acc_scr, s0_scr, s1_scr, *, tq, tk, n_kv):
    qi = pl.program_id(1)
    cur = qi % 2
    n_col = q_ref.shape[1] // LANES

    def stack(nxt):
        ref = qn_ref if nxt else q_ref
        slot = 1 - cur if nxt else cur
        _stack_halves([ref[:, c * LANES:(c + 1) * LANES] for c in range(n_col)], lhs_scr.at[slot], tq)

    def score(t, special, nxt):
        return _qk(lhs_scr[1 - cur if nxt else cur], k_ref[_key_rows(t, tk), :])

    def consume(s, t, special):
        _consume(s, v_ref[_key_rows(t, tk), :], m_scr, acc_scr)

    _pipelined_attention(qi, n_kv, 0, stack, score, consume, m_scr, acc_scr, (s0_scr, s1_scr))
    o = acc_scr[:, :LANES] / acc_scr[:, LANES:]
    low = lax.broadcasted_iota(jnp.int32, (tq, LANES), 1) < HEAD_DIM
    for c in range(n_col):
        o_ref[:, c * LANES:(c + 1) * LANES] = jnp.where(
            low, o[(2 * c) * tq:(2 * c + 1) * tq], o[(2 * c + 1) * tq:(2 * c + 2) * tq]
        ).astype(o_ref.dtype)


def _gqa(q, k, v, nseq, seq_len):
    dq = q.shape[1]
    tq = min(GQA_TQ, seq_len)
    tk = min(GQA_TK, seq_len)
    nq = seq_len // tq
    assert nq == 1 or nq % 2 == 0
    m = 2 * (dq // LANES) * tq
    return pl.pallas_call(
        functools.partial(_gqa_body, tq=tq, tk=tk, n_kv=seq_len // tk),
        grid=(nseq, nq),
        in_specs=[pl.BlockSpec((tq, dq), lambda b, i: (b * nq + i, 0)),
                  pl.BlockSpec((tq, dq), lambda b, i: (b * nq + jnp.minimum(i + 1, nq - 1), 0)),
                  pl.BlockSpec((seq_len, LANES), lambda b, i: (b, 0)),
                  pl.BlockSpec((seq_len, LANES), lambda b, i: (b, 0))],
        out_specs=pl.BlockSpec((tq, dq), lambda b, i: (b * nq + i, 0)),
        out_shape=jax.ShapeDtypeStruct((nseq * seq_len, dq), BF16),
        scratch_shapes=_attn_scratch(m, tk),
        compiler_params=_params(2),
        name="gqa_attn",
    )(q, q, k, v)


def _bias_span(tq, tk):
    u_lo = -((MAX_DISTANCE + tk - 1 + tq - 1) // tq)
    u_hi = (MAX_DISTANCE + tq - 1 + tq - 1) // tq
    return u_lo, u_hi


def _t5_bucket(rel):
    nb = N_BUCKETS // 2
    max_exact = nb // 2
    ret = jnp.where(rel > 0, nb, 0)
    n = jnp.abs(rel)
    nf = jnp.maximum(n, 1).astype(F32)
    large = max_exact + (jnp.log(nf / max_exact) / math.log(MAX_DISTANCE / max_exact)
                         * (nb - max_exact)).astype(jnp.int32)
    large = jnp.minimum(large, nb - 1)
    return ret + jnp.where(n < max_exact, n, large)


def _bias_body(tab_ref, o_ref, *, tq, tk, u_lo, width):
    h = pl.program_id(0)
    u = pl.program_id(1) + u_lo
    rel = u * tq - (tq - 1) + lax.broadcasted_iota(jnp.int32, (SUBLANES, width), 1)
    bucket = _t5_bucket(rel)
    profile = jnp.zeros((SUBLANES, width), F32)
    for b in range(N_BUCKETS):
        profile = jnp.where(bucket == b, tab_ref[b, h], profile)
    rows = jnp.broadcast_to(profile[0:1, :], (tq, width))
    tile = pltpu.roll(rows, width - (tq - 1), 1, stride=1, stride_axis=0)
    o_ref[0, 0] = tile[:, :tk] * LOG2E


def _bias_tiles(rel_bias, tq, tk):
    u_lo, u_hi = _bias_span(tq, tk)
    n_off = u_hi - u_lo + 1
    n_heads = rel_bias.shape[1]
    width = pl.next_power_of_2(tq + tk - 1)
    return pl.pallas_call(
        functools.partial(_bias_body, tq=tq, tk=tk, u_lo=u_lo, width=width),
        grid=(n_heads, n_off),
        in_specs=[pl.BlockSpec(memory_space=pltpu.SMEM)],
        out_specs=pl.BlockSpec((1, 1, tq, tk), lambda h, u: (h, u, 0, 0)),
        out_shape=jax.ShapeDtypeStruct((n_heads, n_off, tq, tk), F32),
        compiler_params=_params(2),
        name="t5_bias",
    )(rel_bias.astype(F32))


def _near_count(tq, tk, u_lo, u_hi):
    r = tk // tq
    return max((qi + u_hi - 1) // r - (qi + u_lo) // r for qi in range(r))


def _diff_body(lam_ref, far_ref, q_ref, qn_ref, k_ref, v_ref, bias_ref, g_ref, o_ref,
               lhs_scr, m_scr, acc_scr, s0_scr, s1_scr, *, tq, tk, n_kv, n_near, u_lo, u_hi, out_scale):
    h = pl.program_id(0)
    qi = pl.program_id(2)
    cur = qi % 2
    r = tk // tq

    def stack(nxt):
        _stack_halves([(qn_ref if nxt else q_ref)[...]], lhs_scr.at[1 - cur if nxt else cur], tq)

    def first_near(q_idx):
        return jnp.clip((q_idx + u_lo) // r + 1, 0, n_kv - n_near)

    def tile_of(p, near, q_idx):
        t_a = first_near(q_idx)
        if near:
            return t_a + (p - (n_kv - n_near))
        return jnp.where(p < t_a, p, p + n_near)

    def score(p, near, nxt):
        q_idx = qi + 1 if nxt else qi
        t = tile_of(p, near, q_idx)
        s = _qk(lhs_scr[1 - cur if nxt else cur], k_ref[_key_rows(t, tk), :])
        if not near:
            return s
        u = jnp.clip(t * r - q_idx, u_lo, u_hi) - u_lo
        return (s.reshape(2, tq, tk) + bias_ref[0, u][None]).reshape(2 * tq, tk)

    def consume(s, p, near):
        t = tile_of(p, near, qi)
        v = v_ref[_key_rows(t, tk), :]
        if near:
            _consume(s, v, m_scr, acc_scr)
        else:
            _consume(s, v, m_scr, acc_scr, const=jnp.where(t < first_near(qi), far_ref[0, h], far_ref[1, h]))

    _pipelined_attention(qi, n_kv, n_near, stack, score, consume, m_scr, acc_scr, (s0_scr, s1_scr))
    o = acc_scr[:, :LANES] / acc_scr[:, LANES:]
    o = o[:tq] - lam_ref[0] * o[tq:]
    o_ref[...] = (_rms(o, g_ref[...], DIFF_SUBLN_EPS) * out_scale).astype(o_ref.dtype)


def _diff(lam, far, q, k, v, bias, subln_g, nseq, seq_len, lam_init, tq, tk):
    n_heads = q.shape[1] // LANES
    nq = seq_len // tq
    assert nq == 1 or nq % 2 == 0
    u_lo, u_hi = _bias_span(tq, tk)
    n_off = u_hi - u_lo + 1
    kv_spec = pl.BlockSpec((seq_len, LANES), lambda h, b, i: (b, h))
    n_kv = seq_len // tk
    n_near = min(_near_count(tq, tk, u_lo, u_hi), n_kv)
    return pl.pallas_call(
        functools.partial(_diff_body, tq=tq, tk=tk, n_kv=n_kv, n_near=n_near, u_lo=u_lo, u_hi=u_hi,
                          out_scale=1.0 - lam_init),
        grid=(n_heads, nseq, nq),
        in_specs=[pl.BlockSpec(memory_space=pltpu.SMEM),
                  pl.BlockSpec(memory_space=pltpu.SMEM),
                  pl.BlockSpec((tq, LANES), lambda h, b, i: (b * nq + i, h)),
                  pl.BlockSpec((tq, LANES), lambda h, b, i: (b * nq + jnp.minimum(i + 1, nq - 1), h)),
                  kv_spec, kv_spec,
                  pl.BlockSpec((1, n_off, tq, tk), lambda h, b, i: (h, 0, 0, 0),
                               pipeline_mode=pl.Buffered(1)),
                  pl.BlockSpec((1, LANES), lambda h, b, i: (0, 0))],
        out_specs=pl.BlockSpec((tq, LANES), lambda h, b, i: (b * nq + i, h)),
        out_shape=jax.ShapeDtypeStruct((nseq * seq_len, q.shape[1]), BF16),
        scratch_shapes=_attn_scratch(2 * tq, tk),
        compiler_params=_params(3),
        name="diff_attn",
    )(lam, far, q, q, k, v, bias, subln_g.reshape(1, LANES))


def _rope_tables(max_len):
    pos = np.arange(max_len)
    half = HEAD_DIM // 2
    inv = ROPE_THETA ** (-jnp.arange(0, half, 2, dtype=F32) / half)
    row = jnp.asarray(pos // GRID_W, F32)
    col = jnp.asarray(pos % GRID_W, F32)
    ang = jnp.concatenate([row[:, None] * inv, col[:, None] * inv], axis=-1)
    cos, sin = jnp.cos(ang), jnp.sin(ang)
    cos_h = jnp.concatenate([cos, cos], axis=1)
    sin_h = jnp.concatenate([-sin, sin], axis=1)
    return jnp.tile(cos_h, (1, LANES // HEAD_DIM)), jnp.tile(sin_h, (1, LANES // HEAD_DIM))


def _column_layout(d_model, d_hy):
    d_gqa = N_Q_HEADS * HEAD_DIM
    d_kv = N_KV_HEADS * HEAD_DIM
    d_diff = N_DIFF_HEADS * 2 * HEAD_DIM
    widths = [3 * d_hy, d_gqa, d_kv, d_kv, d_diff, d_diff, d_diff, 3 * d_model]
    offs = [0]
    for w in widths:
        offs.append(offs[-1] + w)
    deint = np.concatenate([np.arange(0, HEAD_DIM, 2), np.arange(1, HEAD_DIM, 2)])
    q_heads = [kv * GQA_GROUP + g for g in range(GQA_GROUP) for kv in range(N_KV_HEADS)]
    q_cols = np.concatenate([offs[1] + h * HEAD_DIM + deint for h in q_heads])
    k_cols = np.concatenate([offs[2] + h * HEAD_DIM + deint for h in range(N_KV_HEADS)])
    qkv_cols = np.concatenate([q_cols, k_cols, np.arange(offs[3], offs[4])])
    out_rows = np.concatenate([h * HEAD_DIM + np.arange(HEAD_DIM) for h in q_heads])
    return offs, qkv_cols, deint, out_rows


def kernel(x_prompt, x_sample, ffn1_norm, ffn1_w_in, ffn1_w_out, mix_norm, w_in, hy_conv_w, hy_conv_b, hy_filt_w1, hy_filt_b1, hy_filt_w2, hy_filt_b2, hy_filt_w3, hy_filt_freq, hy_skip, gqa_q_norm, gqa_k_norm, diff_lambda, diff_subln, rel_bias, w_branch, w_out, ffn2_norm, ffn2_w_in, ffn2_w_out, final_norm):
    d = x_prompt.shape[-1]
    depth = w_in.shape[0]
    d_hy = hy_skip.shape[1]
    groups = [(x.shape[0], x.shape[1]) for x in (x_prompt, x_sample)]
    xs = [x_prompt.reshape(-1, d), x_sample.reshape(-1, d)]
    lens = sorted({sl for _, sl in groups})
    assert all(sl % GRID_W == 0 for sl in lens)

    offs, qkv_cols, deint, out_rows = _column_layout(d, d_hy)
    cos_t, sin_t = _rope_tables(max(lens))
    bd = jnp.asarray(np.kron(np.eye(LANES // HEAD_DIM), np.full((HEAD_DIM, HEAD_DIM), 1.0 / HEAD_DIM)), BF16)
    tabs = {sl: _dft_tables(sl) for sl in lens}
    diff_tiles = {sl: (min(DIFF_TQ, sl), min(DIFF_TK, sl)) for sl in lens}
    bias = {tt: _bias_tiles(rel_bias, *tt) for tt in sorted(set(diff_tiles.values()))}
    far_bias = rel_bias[jnp.array([N_BUCKETS // 2 - 1, N_BUCKETS - 1])].astype(F32) * LOG2E

    for l in range(depth):
        w1_in, w1_out = ffn1_w_in[l].astype(BF16), ffn1_w_out[l].astype(BF16)
        w2_in, w2_out = ffn2_w_in[l].astype(BF16), ffn2_w_out[l].astype(BF16)
        w_l = w_in[l].astype(BF16)
        w_qkv = w_in[l][:, offs[1]:offs[4]][:, qkv_cols - offs[1]].astype(BF16)
        qn = jnp.tile(gqa_q_norm[l][deint], N_Q_HEADS).reshape(1, -1)
        kn = jnp.tile(gqa_k_norm[l][deint], N_KV_HEADS).reshape(1, -1)
        wb = jnp.stack([w_branch[l][0], w_branch[l][1][out_rows], w_branch[l][2]]).astype(BF16)
        wo = w_out[l].astype(BF16)
        lp = diff_lambda[l].astype(F32)
        lam_init = 0.8 - 0.6 * math.exp(-0.3 * l)
        lam = (jnp.exp(jnp.sum(lp[0] * lp[1])) - jnp.exp(jnp.sum(lp[2] * lp[3])) + lam_init).reshape(1)
        kfs = {sl: _hyena_kf(sl, tabs[sl], hy_filt_w1[l], hy_filt_b1[l], hy_filt_w2[l], hy_filt_b2[l],
                             hy_filt_w3[l], hy_filt_freq[l]) for sl in lens}
        final_g = final_norm if l == depth - 1 else None

        for g, (nseq, sl) in enumerate(groups):
            x = _ffn(xs[g], ffn1_norm[l], w1_in, w1_out)
            z, x0, gq, gk, gv, dq, dk, dv, gates = _inproj(x, mix_norm[l], w_l, w_qkv, hy_conv_w[l],
                                                            hy_conv_b[l], cos_t, sin_t, bd, qn, kn, offs, sl)
            y_hy = _hyena_longconv(z, x0, kfs[sl], hy_skip[l], nseq, sl, tabs[sl])
            y_gqa = _gqa(gq, gk, gv, nseq, sl)
            tq, tk = diff_tiles[sl]
            y_diff = _diff(lam, far_bias, dq, dk, dv, bias[(tq, tk)], diff_subln[l], nseq, sl, lam_init, tq, tk)
            xs[g] = _ffn(x, ffn2_norm[l], w2_in, w2_out, final_g=final_g,
                         merge=(y_hy, y_gqa, y_diff, gates, wb, wo))

    return xs[0].reshape(x_prompt.shape), xs[1].reshape(x_sample.shape)
```
